```python
import math
import jax, jax.numpy as jnp
from jax import lax
import numpy as np

D_MODEL = 4096
BATCH = 2
SEQ = 8192
DEPTH = 4

N_HEADS = 32
HEAD_DIM = 128
N_KV_HEADS = 4
HEADS_PER_KV = N_HEADS // N_KV_HEADS
ATTN_WIDTH = N_HEADS * HEAD_DIM
KV_WIDTH = N_KV_HEADS * HEAD_DIM
D_FF = 5504
FFN_RES = 0.5
N_MIXERS = 3
N_MOD = 9
NUM_BUCKETS = 32
MAX_DISTANCE = 1024
Q_BLK = 128
EPS = 1e-6
NEG = -1e30
BIG = 1e30
IDX_HEADS = 16
IDX_DIM = 128
TOPK_KEYS = 256
CMP_BLOCK = 32
CMP_STRIDE = 16
SEL_BLOCK = 64
SEL_TOPN = 16
WINDOW = 512
D_RNN = 4096
RG_BLOCKS = 16
RG_BLOCK_DIM = D_RNN // RG_BLOCKS
CONV_WIDTH = 4
RG_C = 8.0

kernel_name = 'hybrid_dsa_nsa_rglru_macaron_adaln'


def rms_norm(x, gain):
    xf = x.astype(jnp.float32)
    y = xf * lax.rsqrt(jnp.mean(xf * xf, axis=-1, keepdims=True) + EPS)
    return (y * gain.astype(jnp.float32)).astype(x.dtype)


def ada_norm(x, gain, shift, scale):
    return rms_norm(x, gain) * (1 + scale[:, None, :]) + shift[:, None, :]


def split_cols(a, sizes):
    return jnp.split(a, [int(s) for s in np.cumsum(sizes)[:-1]], axis=-1)


def swiglu(y, w_in, w_out):
    g, u = jnp.split(y @ w_in, 2, axis=-1)
    return (jax.nn.silu(g) * u) @ w_out


def masked_softmax(logits, mask, axis=-1):
    p = jax.nn.softmax(jnp.where(mask, logits, NEG), axis=axis)
    return jnp.where(mask, p, 0.0)


def t5_bucket(dist):
    n = jnp.maximum(dist, 0)
    max_exact = NUM_BUCKETS // 2
    nf = jnp.maximum(n, 1).astype(jnp.float32)
    large = max_exact + (jnp.log(nf / max_exact) / math.log(MAX_DISTANCE / max_exact)
                         * (NUM_BUCKETS - max_exact)).astype(jnp.int32)
    large = jnp.minimum(large, NUM_BUCKETS - 1)
    return jnp.where(n < max_exact, n, large)


def head_bias(rel_bias, dist):
    return rel_bias[t5_bucket(dist)].reshape(dist.shape + (N_KV_HEADS, HEADS_PER_KV))


def dsa_mixer(h, w_in, w_out, q_gain, k_gain, rel_bias):
    B, S, _ = h.shape
    G, R, dh = N_KV_HEADS, HEADS_PER_KV, HEAD_DIM
    q, k, v, qi, ki, wi = split_cols(h @ w_in, [ATTN_WIDTH, KV_WIDTH, KV_WIDTH,
                                                IDX_HEADS * IDX_DIM, IDX_DIM, IDX_HEADS])
    q = rms_norm(q.reshape(B, S, G, R, dh), q_gain)
    k = rms_norm(k.reshape(B, S, G, dh), k_gain)
    v = v.reshape(B, S, G, dh)
    qi = qi.reshape(B, S, IDX_HEADS, IDX_DIM)
    wi = wi.astype(jnp.float32) * (IDX_HEADS ** -0.5 * IDX_DIM ** -0.5)
    n_keep = min(TOPK_KEYS, S // 4)
    key_pos = jnp.arange(S)
    b_ix = jnp.arange(B)[:, None, None]

    def block(i):
        q0 = i * Q_BLK
        t = q0 + jnp.arange(Q_BLK)
        qb = lax.dynamic_slice_in_dim(q, q0, Q_BLK, axis=1)
        qib = lax.dynamic_slice_in_dim(qi, q0, Q_BLK, axis=1)
        wib = lax.dynamic_slice_in_dim(wi, q0, Q_BLK, axis=1)
        dots = jnp.einsum('bqhd,bsd->bqhs', qib, ki, preferred_element_type=jnp.float32)
        score = jnp.einsum('bqhs,bqh->bqs', jax.nn.relu(dots), wib)
        score = jnp.where(key_pos[None, None, :] <= t[None, :, None], score, NEG)
        _, sel = lax.top_k(score, n_keep)
        kg = k[b_ix, sel]
        vg = v[b_ix, sel]
        dist = t[None, :, None] - sel
        logits = jnp.einsum('bqgrd,bqkgd->bqgrk', qb, kg, preferred_element_type=jnp.float32) * dh ** -0.5
        logits = logits + head_bias(rel_bias, dist).transpose(0, 1, 3, 4, 2)
        p = masked_softmax(logits, (dist >= 0)[:, :, None, None, :])
        o = jnp.einsum('bqgrk,bqkgd->bqgrd', p.astype(vg.dtype), vg)
        return o.reshape(B, Q_BLK, ATTN_WIDTH)

    o = lax.map(block, jnp.arange(S // Q_BLK))
    return o.transpose(1, 0, 2, 3).reshape(B, S, ATTN_WIDTH) @ w_out


def nsa_mixer(h, w_in, gate_b, w_out, q_gain, k_gain, cmp_pos, cmp_w1, cmp_w2, rel_bias):
    B, S, _ = h.shape
    G, R, dh = N_KV_HEADS, HEADS_PER_KV, HEAD_DIM
    kv_shape = (B, S, G, dh)
    q, kc, vc, ks, vs, kw, vw, g = split_cols(h @ w_in, [ATTN_WIDTH] + [KV_WIDTH] * 6 + [3 * N_HEADS])
    q = rms_norm(q.reshape(B, S, G, R, dh), q_gain)

    n_cmp = (S - CMP_BLOCK) // CMP_STRIDE + 1
    cmp_start = jnp.arange(n_cmp) * CMP_STRIDE
    cmp_end = cmp_start + CMP_BLOCK - 1
    tok = cmp_start[:, None] + jnp.arange(CMP_BLOCK)[None, :]

    def compress(u, pos, w1, w2):
        blocks = u.reshape(kv_shape)[:, tok] + pos[None, None, :, None, :]
        blocks = blocks.transpose(0, 1, 3, 2, 4).reshape(B, n_cmp, G, CMP_BLOCK * dh)
        return jax.nn.gelu(blocks @ w1) @ w2

    k_cmp = rms_norm(compress(kc, cmp_pos[0], cmp_w1[0], cmp_w2[0]), k_gain[0])
    v_cmp = compress(vc, cmp_pos[1], cmp_w1[1], cmp_w2[1])

    n_sel = S // SEL_BLOCK
    n_pick = min(SEL_TOPN, n_sel)
    sel_start = jnp.arange(n_sel) * SEL_BLOCK
    cover = ((cmp_start[:, None] < sel_start[None, :] + SEL_BLOCK)
             & (cmp_end[:, None] >= sel_start[None, :])).astype(jnp.float32)
    ks_blk = rms_norm(ks.reshape(kv_shape), k_gain[1]).reshape(B, n_sel, SEL_BLOCK, G, dh).transpose(0, 3, 1, 2, 4)
    vs_blk = vs.reshape(B, n_sel, SEL_BLOCK, G, dh).transpose(0, 3, 1, 2, 4)

    pad = ((0, 0), (WINDOW, 0), (0, 0), (0, 0))
    kw_pad = jnp.pad(rms_norm(kw.reshape(kv_shape), k_gain[2]), pad)
    vw_pad = jnp.pad(vw.reshape(kv_shape), pad)

    gates = jax.nn.sigmoid((g + gate_b).astype(jnp.float32)).astype(h.dtype).reshape(B, S, 3, G, R)
    rb_group = rel_bias.reshape(NUM_BUCKETS, G, R)
    b_ix = jnp.arange(B)[:, None, None, None]
    g_ix = jnp.arange(G)[None, None, :, None]
    scale = dh ** -0.5

    def block(i):
        q0 = i * Q_BLK
        t = q0 + jnp.arange(Q_BLK)
        qb = lax.dynamic_slice_in_dim(q, q0, Q_BLK, axis=1)
        d_c = t[:, None] - cmp_end[None, :]
        lc = jnp.einsum('bqgrd,bcgd->bqgrc', qb, k_cmp, preferred_element_type=jnp.float32) * scale
        lc = lc + head_bias(rel_bias, d_c).transpose(0, 2, 3, 1)[None]
        pc = masked_softmax(lc, (d_c >= 0)[None, :, None, None, :])
        o_cmp = jnp.einsum('bqgrc,bcgd->bqgrd', pc.astype(v_cmp.dtype), v_cmp)
        imp = jnp.einsum('bqgrc,cn->bqgn', pc, cover)
        blk = jnp.arange(n_sel)[None, :]
        cur = (t // SEL_BLOCK)[:, None]
        forced = (blk == 0) | (blk == cur) | (blk == cur - 1)
        causal = sel_start[None, :] <= t[:, None]
        imp = jnp.where(forced[None, :, None, :], BIG, jnp.where(causal[None, :, None, :], imp, NEG))
        _, sel = lax.top_k(imp, n_pick)
        ksg = ks_blk[b_ix, g_ix, sel]
        vsg = vs_blk[b_ix, g_ix, sel]
        d_s = t[None, :, None, None, None] - (sel[..., None] * SEL_BLOCK + jnp.arange(SEL_BLOCK))
        ls = jnp.einsum('bqgrd,bqgnld->bqgrnl', qb, ksg, preferred_element_type=jnp.float32) * scale
        ls = ls + rb_group[t5_bucket(d_s), g_ix[..., None]].transpose(0, 1, 2, 5, 3, 4)
        ps = masked_softmax(ls, (d_s >= 0)[:, :, :, None], axis=(-2, -1))
        o_sel = jnp.einsum('bqgrnl,bqgnld->bqgrd', ps.astype(vsg.dtype), vsg)
        kwb = lax.dynamic_slice_in_dim(kw_pad, q0, WINDOW + Q_BLK, axis=1)
        vwb = lax.dynamic_slice_in_dim(vw_pad, q0, WINDOW + Q_BLK, axis=1)
        s_pos = q0 - WINDOW + jnp.arange(WINDOW + Q_BLK)
        d_w = t[:, None] - s_pos[None, :]
        vis_w = (d_w >= 0) & (d_w < WINDOW) & (s_pos[None, :] >= 0)
        lw = jnp.einsum('bqgrd,bkgd->bqgrk', qb, kwb, preferred_element_type=jnp.float32) * scale
        lw = lw + head_bias(rel_bias, d_w).transpose(0, 2, 3, 1)[None]
        pw = masked_softmax(lw, vis_w[None, :, None, None, :])
        o_win = jnp.einsum('bqgrk,bkgd->bqgrd', pw.astype(vwb.dtype), vwb)
        gb = lax.dynamic_slice_in_dim(gates, q0, Q_BLK, axis=1)[..., None]
        o = gb[:, :, 0] * o_cmp + gb[:, :, 1] * o_sel + gb[:, :, 2] * o_win
        return o.reshape(B, Q_BLK, ATTN_WIDTH)

    o = lax.map(block, jnp.arange(S // Q_BLK))
    return o.transpose(1, 0, 2, 3).reshape(B, S, ATTN_WIDTH) @ w_out


def _lru_combine(left, right):
    a_l, u_l = left
    a_r, u_r = right
    return a_l * a_r, a_r * u_l + u_r


def rglru_mixer(h, w_in, conv_w, conv_b, gate_w, gate_b, lam, w_out):
    B, S, _ = h.shape
    gate_branch, xr = jnp.split(h @ w_in, 2, axis=-1)
    xc = lax.conv_general_dilated(xr, conv_w[:, None, :], window_strides=(1,),
                                  padding=[(CONV_WIDTH - 1, 0)],
                                  dimension_numbers=('NWC', 'WIO', 'NWC'),
                                  feature_group_count=D_RNN) + conv_b
    xb = xc.reshape(B, S, RG_BLOCKS, RG_BLOCK_DIM)
    gr = jnp.einsum('bsnk,gnkj->gbsnj', xb, gate_w, preferred_element_type=jnp.float32)
    gr = gr.reshape(2, B, S, D_RNN) + gate_b[:, None, None, :].astype(jnp.float32)
    r = jax.nn.sigmoid(gr[0])
    i_g = jax.nn.sigmoid(gr[1])
    log_a = -RG_C * r * jax.nn.softplus(-lam.astype(jnp.float32))
    a = jnp.exp(log_a)
    u = jnp.sqrt(-jnp.expm1(2.0 * log_a)) * (i_g * xc.astype(jnp.float32))
    _, hs = lax.associative_scan(_lru_combine, (a, u), axis=1)
    return (jax.nn.gelu(gate_branch) * hs.astype(h.dtype)) @ w_out


def setup_inputs(seed: int = 0) -> dict:
    key = jax.random.key(seed)
    keys = iter(jax.random.split(key, 48))

    def nrm(shape, scale):
        return jax.random.normal(next(keys), shape, jnp.float32) * scale

    def gain(shape):
        return 1.0 + nrm(shape, 0.02)

    n_a, n_b, n_c = (DEPTH + 2) // 3, (DEPTH + 1) // 3, DEPTH // 3
    D = D_MODEL
    dh = HEAD_DIM
    dsa_cols = ATTN_WIDTH + 2 * KV_WIDTH + IDX_HEADS * IDX_DIM + IDX_DIM + IDX_HEADS
    nsa_cols = ATTN_WIDTH + 6 * KV_WIDTH + 3 * N_HEADS
    a0 = jax.random.uniform(next(keys), (n_c, D_RNN), jnp.float32, 0.9, 0.999)
    base = a0 ** (1.0 / RG_C)
    return {
        'x': nrm((BATCH, SEQ, D), 1.0),
        'c': nrm((BATCH, D), 1.0),
        'rel_bias': nrm((NUM_BUCKETS, N_HEADS), 0.2),
        'ada_w': nrm((D, N_MOD * D), 0.2 * D ** -0.5),
        'ada_b': nrm((N_MOD * D,), 0.01),
        'ada_table': nrm((DEPTH, N_MOD, D), 0.02).at[:, 2::3].add(1.0),
        'norm_g': gain((DEPTH, 3, D)),
        'ffn_w_in': nrm((DEPTH, 2, D, 2 * D_FF), D ** -0.5),
        'ffn_w_out': nrm((DEPTH, 2, D_FF, D), D_FF ** -0.5),
        'dsa_w_in': nrm((n_a, D, dsa_cols), D ** -0.5),
        'dsa_w_out': nrm((n_a, ATTN_WIDTH, D), ATTN_WIDTH ** -0.5),
        'dsa_q_gain': gain((n_a, dh)),
        'dsa_k_gain': gain((n_a, dh)),
        'nsa_w_in': nrm((n_b, D, nsa_cols), D ** -0.5),
        'nsa_gate_b': nrm((n_b, 3 * N_HEADS), 0.01),
        'nsa_w_out': nrm((n_b, ATTN_WIDTH, D), ATTN_WIDTH ** -0.5),
        'nsa_q_gain': gain((n_b, dh)),
        'nsa_k_gain': gain((n_b, 3, dh)),
        'nsa_cmp_pos': nrm((n_b, 2, CMP_BLOCK, dh), 0.1),
        'nsa_cmp_w1': nrm((n_b, 2, CMP_BLOCK * dh, dh), (CMP_BLOCK * dh) ** -0.5),
        'nsa_cmp_w2': nrm((n_b, 2, dh, dh), dh ** -0.5),
        'rg_w_in': nrm((n_c, D, 2 * D_RNN), D ** -0.5),
        'rg_conv_w': nrm((n_c, CONV_WIDTH, D_RNN), CONV_WIDTH ** -0.5),
        'rg_conv_b': nrm((n_c, D_RNN), 0.01),
        'rg_gate_w': nrm((n_c, 2, RG_BLOCKS, RG_BLOCK_DIM, RG_BLOCK_DIM), RG_BLOCK_DIM ** -0.5),
        'rg_gate_b': nrm((n_c, 2, D_RNN), 0.01),
        'rg_lambda': jnp.log(base) - jnp.log1p(-base),
        'rg_w_out': nrm((n_c, D_RNN, D), D_RNN ** -0.5),
    }


def reference(x, c, rel_bias, ada_w, ada_b, ada_table, norm_g, ffn_w_in, ffn_w_out,
              dsa_w_in, dsa_w_out, dsa_q_gain, dsa_k_gain,
              nsa_w_in, nsa_gate_b, nsa_w_out, nsa_q_gain, nsa_k_gain, nsa_cmp_pos, nsa_cmp_w1, nsa_cmp_w2,
              rg_w_in, rg_conv_w, rg_conv_b, rg_gate_w, rg_gate_b, rg_lambda, rg_w_out):
    B = x.shape[0]
    mod_all = (jax.nn.silu(c) @ ada_w + ada_b).reshape(B, N_MOD, D_MODEL)
    for layer in range(DEPTH):
        mod = mod_all + ada_table[layer]
        y = ada_norm(x, norm_g[layer, 0], mod[:, 0], mod[:, 1])
        x = x + FFN_RES * mod[:, 2][:, None, :] * swiglu(y, ffn_w_in[layer, 0], ffn_w_out[layer, 0])
        y = ada_norm(x, norm_g[layer, 1], mod[:, 3], mod[:, 4])
        kind, j = layer % N_MIXERS, layer // N_MIXERS
        if kind == 0:
            m = dsa_mixer(y, dsa_w_in[j], dsa_w_out[j], dsa_q_gain[j], dsa_k_gain[j], rel_bias)
        elif kind == 1:
            m = nsa_mixer(y, nsa_w_in[j], nsa_gate_b[j], nsa_w_out[j], nsa_q_gain[j], nsa_k_gain[j],
                          nsa_cmp_pos[j], nsa_cmp_w1[j], nsa_cmp_w2[j], rel_bias)
        else:
            m = rglru_mixer(y, rg_w_in[j], rg_conv_w[j], rg_conv_b[j], rg_gate_w[j], rg_gate_b[j],
                            rg_lambda[j], rg_w_out[j])
        x = x + mod[:, 5][:, None, :] * m
        y = ada_norm(x, norm_g[layer, 2], mod[:, 6], mod[:, 7])
        x = x + FFN_RES * mod[:, 8][:, None, :] * swiglu(y, ffn_w_in[layer, 1], ffn_w_out[layer, 1])
    return x
```

```python
import functools
import math

import numpy as np
import jax
import jax.numpy as jnp
from jax import lax
from jax.experimental import pallas as pl
from jax.experimental.pallas import tpu as pltpu

N_HEADS = 32
HEAD_DIM = 128
N_KV_HEADS = 4
FFN_RES = 0.5
N_MOD = 9
NUM_BUCKETS = 32
MAX_DISTANCE = 1024
Q_BLK = 128
EPS = 1e-6
NEG = -1e30
BIG = 1e30
IDX_HEADS = 16
IDX_DIM = 128
TOPK_KEYS = 256
CMP_BLOCK = 32
CMP_STRIDE = 16
SEL_BLOCK = 64
SEL_TOPN = 16
WINDOW = 512
RG_BLOCKS = 16
CONV_WIDTH = 4
RG_C = 8.0

LANES = 128
VMEM_LIMIT_BYTES = 56 * 1024 * 1024

F32 = jnp.float32
BF16 = jnp.bfloat16


def _round_up(n, m):
    return (n + m - 1) // m * m


def _cparams(sem):
    return pltpu.CompilerParams(dimension_semantics=sem, vmem_limit_bytes=VMEM_LIMIT_BYTES)


def _mod_kernel(c_ref, w_ref, b_ref, o_ref):
    c = c_ref[...]
    s = c * jax.nn.sigmoid(c)
    o_ref[...] = jnp.dot(s.astype(BF16), w_ref[...].astype(BF16), preferred_element_type=F32) + b_ref[...]


def _mod_all(c, ada_w, ada_b):
    B, D = c.shape
    N = ada_w.shape[1]
    rows = 16
    tn = next(t for t in (512, 256, 128) if N % t == 0)
    cp = jnp.zeros((rows, D), F32).at[:B].set(c)
    out = pl.pallas_call(
        _mod_kernel,
        grid=(N // tn,),
        in_specs=[pl.BlockSpec((rows, D), lambda j: (0, 0)),
                  pl.BlockSpec((D, tn), lambda j: (0, j)),
                  pl.BlockSpec((1, tn), lambda j: (0, j))],
        out_specs=pl.BlockSpec((rows, tn), lambda j: (0, j)),
        out_shape=jax.ShapeDtypeStruct((rows, N), F32),
        compiler_params=_cparams(("arbitrary",)),
        name="ada_mod",
    )(cp, ada_w, ada_b.reshape(1, N))
    return out[:B].reshape(B, N_MOD, D)


def _ada_norm_block(x, g, mod, tab, k):
    var = jnp.mean(x * x, axis=-1, keepdims=True)
    y = x * lax.rsqrt(var + EPS) * g
    shift = mod[k:k + 1, :] + tab[k:k + 1, :]
    scale = mod[k + 1:k + 2, :] + tab[k + 1:k + 2, :]
    return y * (1.0 + scale) + shift


def _norm_matmul_kernel(x_ref, mod_ref, tab_ref, g_ref, w_ref, gc_ref, *refs, mod_row, epilogue,
                        norm_tiles, has_aux, n_tiles):
    if has_aux:
        o_ref, aux_ref, y_scr = refs
    else:
        o_ref, y_scr = refs
    j = pl.program_id(1)

    @pl.when(j == 0)
    def _():
        y = _ada_norm_block(x_ref[...], g_ref[...], mod_ref[0], tab_ref[...], mod_row)
        y_scr[...] = y.astype(BF16)

    h = jnp.dot(y_scr[...], w_ref[...], preferred_element_type=F32)
    if epilogue == "swiglu":
        tf = h.shape[1] // 2
        g = h[:, :tf]
        u = h[:, tf:]
        o_ref[...] = (g * jax.nn.sigmoid(g) * u).astype(o_ref.dtype)
    else:
        if norm_tiles:
            is_norm = functools.reduce(lambda a, b: a | b, [j == t for t in norm_tiles])

            @pl.when(is_norm)
            def _():
                for c in range(h.shape[1] // LANES):
                    hc = h[:, c * LANES:(c + 1) * LANES]
                    ms = jnp.mean(hc * hc, axis=-1, keepdims=True)
                    yc = hc * lax.rsqrt(ms + EPS) * gc_ref[:, c * LANES:(c + 1) * LANES]
                    o_ref[:, c * LANES:(c + 1) * LANES] = yc.astype(o_ref.dtype)

            @pl.when(jnp.logical_not(is_norm))
            def _():
                o_ref[...] = h.astype(o_ref.dtype)
        else:
            o_ref[...] = h.astype(o_ref.dtype)
        if has_aux:
            @pl.when(j == n_tiles - 1)
            def _():
                aux_ref[...] = h


def _norm_matmul(x, mod_all, tab, gain, w, *, mod_row, seq, epilogue="plain", tn, tm=512,
                 out_dtype=BF16, gain_cols=None, norm_tiles=(), has_aux=False):
    T, D = x.shape
    wn = 2 * tn if epilogue == "swiglu" else tn
    n_tiles = w.shape[1] // wn
    n_out = n_tiles * tn
    tm = min(tm, seq)
    bpb = seq // tm
    if gain_cols is None:
        gain_cols = jnp.zeros((1, n_out), F32)
    out_shape = [jax.ShapeDtypeStruct((T, n_out), out_dtype)]
    out_specs = [pl.BlockSpec((tm, tn), lambda i, j: (i, j))]
    if has_aux:
        out_shape.append(jax.ShapeDtypeStruct((T, tn), F32))
        out_specs.append(pl.BlockSpec((tm, tn), lambda i, j: (i, 0)))
    kern = functools.partial(_norm_matmul_kernel, mod_row=mod_row, epilogue=epilogue,
                             norm_tiles=tuple(norm_tiles), has_aux=has_aux, n_tiles=n_tiles)
    res = pl.pallas_call(
        kern,
        grid=(T // tm, n_tiles),
        in_specs=[pl.BlockSpec((tm, D), lambda i, j: (i, 0)),
                  pl.BlockSpec((1, N_MOD, D), lambda i, j: (i // bpb, 0, 0)),
                  pl.BlockSpec((N_MOD, D), lambda i, j: (0, 0)),
                  pl.BlockSpec((1, D), lambda i, j: (0, 0)),
                  pl.BlockSpec((D, wn), lambda i, j: (0, j)),
                  pl.BlockSpec((1, tn), lambda i, j: (0, j))],
        out_specs=out_specs,
        out_shape=out_shape,
        scratch_shapes=[pltpu.VMEM((tm, D), BF16)],
        compiler_params=_cparams(("arbitrary", "arbitrary")),
        name="norm_matmul_" + epilogue,
    )(x, mod_all, tab, gain.reshape(1, D), w, gain_cols)
    return res if has_aux else res[0]


def _matmul_res_kernel(a_ref, w_ref, x_ref, mod_ref, tab_ref, o_ref, *, mod_row, res_scale):
    h = jnp.dot(a_ref[...], w_ref[...], preferred_element_type=F32)
    gate = mod_ref[0, mod_row:mod_row + 1, :] + tab_ref[mod_row:mod_row + 1, :]
    if res_scale != 1.0:
        gate = res_scale * gate
    o_ref[...] = x_ref[...] + gate * h


def _matmul_res(a, w, x, mod_all, tab, *, mod_row, seq, res_scale=1.0, tm=512, tn=512):
    T, K = a.shape
    D = w.shape[1]
    tm = min(tm, seq)
    tn = min(tn, D)
    bpb = seq // tm
    kern = functools.partial(_matmul_res_kernel, mod_row=mod_row, res_scale=res_scale)
    return pl.pallas_call(
        kern,
        grid=(T // tm, D // tn),
        in_specs=[pl.BlockSpec((tm, K), lambda i, j: (i, 0)),
                  pl.BlockSpec((K, tn), lambda i, j: (0, j)),
                  pl.BlockSpec((tm, tn), lambda i, j: (i, j)),
                  pl.BlockSpec((1, N_MOD, tn), lambda i, j: (i // bpb, 0, j)),
                  pl.BlockSpec((N_MOD, tn), lambda i, j: (0, j))],
        out_specs=pl.BlockSpec((tm, tn), lambda i, j: (i, j)),
        out_shape=jax.ShapeDtypeStruct((T, D), F32),
        compiler_params=_cparams(("arbitrary", "arbitrary")),
        name="matmul_residual",
    )(a, w, x, mod_all, tab)


def _ffn_tile(d_ff):
    return 512 if d_ff >= 512 else _round_up(d_ff, LANES)


def _prep_ffn_weights(w_in, w_out):
    D, two_f = w_in.shape
    F = two_f // 2
    tf = _ffn_tile(F)
    Fp = _round_up(F, tf)
    g = jnp.pad(w_in[:, :F], ((0, 0), (0, Fp - F))).reshape(D, Fp // tf, 1, tf)
    u = jnp.pad(w_in[:, F:], ((0, 0), (0, Fp - F))).reshape(D, Fp // tf, 1, tf)
    w1 = jnp.concatenate([g, u], axis=2).reshape(D, 2 * Fp).astype(BF16)
    w2 = jnp.pad(w_out, ((0, Fp - F), (0, 0))).astype(BF16)
    return w1, w2, tf


def _ffn(x, mod_all, tab, gain, w_in, w_out, *, mod_row, seq):
    w1, w2, tf = _prep_ffn_weights(w_in, w_out)
    a = _norm_matmul(x, mod_all, tab, gain, w1, mod_row=mod_row, seq=seq, epilogue="swiglu", tn=tf)
    return _matmul_res(a, w2, x, mod_all, tab, mod_row=mod_row + 2, seq=seq, res_scale=FFN_RES)


def _rms_norm(x, gain):
    xf = x.astype(F32)
    y = xf * lax.rsqrt(jnp.mean(xf * xf, axis=-1, keepdims=True) + EPS)
    return (y * gain.astype(F32)).astype(x.dtype)


def _split_cols(a, sizes):
    return jnp.split(a, [int(s) for s in np.cumsum(sizes)[:-1]], axis=-1)


def _masked_softmax(logits, mask, axis=-1):
    p = jax.nn.softmax(jnp.where(mask, logits, NEG), axis=axis)
    return jnp.where(mask, p, 0.0)


def _t5_bucket(dist):
    n = jnp.maximum(dist, 0)
    max_exact = NUM_BUCKETS // 2
    nf = jnp.maximum(n, 1).astype(F32)
    large = max_exact + (jnp.log(nf / max_exact) / math.log(MAX_DISTANCE / max_exact)
                         * (NUM_BUCKETS - max_exact)).astype(jnp.int32)
    large = jnp.minimum(large, NUM_BUCKETS - 1)
    return jnp.where(n < max_exact, n, large)


def _head_bias(rel_bias, dist):
    R = N_HEADS // N_KV_HEADS
    return rel_bias[_t5_bucket(dist)].reshape(dist.shape + (N_KV_HEADS, R))


def _dsa_mixer_jnp(h, w_in, w_out, q_gain, k_gain, rel_bias):
    B, S, _ = h.shape
    G, R, dh = N_KV_HEADS, N_HEADS // N_KV_HEADS, HEAD_DIM
    AW, KW = N_HEADS * HEAD_DIM, N_KV_HEADS * HEAD_DIM
    q, k, v, qi, ki, wi = _split_cols(h @ w_in, [AW, KW, KW, IDX_HEADS * IDX_DIM, IDX_DIM, IDX_HEADS])
    q = _rms_norm(q.reshape(B, S, G, R, dh), q_gain)
    k = _rms_norm(k.reshape(B, S, G, dh), k_gain)
    v = v.reshape(B, S, G, dh)
    qi = qi.reshape(B, S, IDX_HEADS, IDX_DIM)
    wi = wi.astype(F32) * (IDX_HEADS ** -0.5 * IDX_DIM ** -0.5)
    n_keep = min(TOPK_KEYS, S // 4)
    key_pos = jnp.arange(S)
    b_ix = jnp.arange(B)[:, None, None]

    def block(i):
        q0 = i * Q_BLK
        t = q0 + jnp.arange(Q_BLK)
        qb = lax.dynamic_slice_in_dim(q, q0, Q_BLK, axis=1)
        qib = lax.dynamic_slice_in_dim(qi, q0, Q_BLK, axis=1)
        wib = lax.dynamic_slice_in_dim(wi, q0, Q_BLK, axis=1)
        dots = jnp.einsum('bqhd,bsd->bqhs', qib, ki, preferred_element_type=F32)
        score = jnp.einsum('bqhs,bqh->bqs', jax.nn.relu(dots), wib)
        score = jnp.where(key_pos[None, None, :] <= t[None, :, None], score, NEG)
        _, sel = lax.top_k(score, n_keep)
        kg = k[b_ix, sel]
        vg = v[b_ix, sel]
        dist = t[None, :, None] - sel
        logits = jnp.einsum('bqgrd,bqkgd->bqgrk', qb, kg, preferred_element_type=F32) * dh ** -0.5
        logits = logits + _head_bias(rel_bias, dist).transpose(0, 1, 3, 4, 2)
        p = _masked_softmax(logits, (dist >= 0)[:, :, None, None, :])
        o = jnp.einsum('bqgrk,bqkgd->bqgrd', p.astype(vg.dtype), vg)
        return o.reshape(B, Q_BLK, AW)

    o = lax.map(block, jnp.arange(S // Q_BLK))
    return o.transpose(1, 0, 2, 3).reshape(B, S, AW) @ w_out


def _nsa_mixer_jnp(h, w_in, gate_b, w_out, q_gain, k_gain, cmp_pos, cmp_w1, cmp_w2, rel_bias):
    B, S, _ = h.shape
    G, R, dh = N_KV_HEADS, N_HEADS // N_KV_HEADS, HEAD_DIM
    AW, KW = N_HEADS * HEAD_DIM, N_KV_HEADS * HEAD_DIM
    kv_shape = (B, S, G, dh)
    q, kc, vc, ks, vs, kw, vw, g = _split_cols(h @ w_in, [AW] + [KW] * 6 + [3 * N_HEADS])
    q = _rms_norm(q.reshape(B, S, G, R, dh), q_gain)
    n_cmp = (S - CMP_BLOCK) // CMP_STRIDE + 1
    cmp_start = jnp.arange(n_cmp) * CMP_STRIDE
    cmp_end = cmp_start + CMP_BLOCK - 1
    tok = cmp_start[:, None] + jnp.arange(CMP_BLOCK)[None, :]

    def compress(u, pos, w1, w2):
        blocks = u.reshape(kv_shape)[:, tok] + pos[None, None, :, None, :]
        blocks = blocks.transpose(0, 1, 3, 2, 4).reshape(B, n_cmp, G, CMP_BLOCK * dh)
        return jax.nn.gelu(blocks @ w1) @ w2

    k_cmp = _rms_norm(compress(kc, cmp_pos[0], cmp_w1[0], cmp_w2[0]), k_gain[0])
    v_cmp = compress(vc, cmp_pos[1], cmp_w1[1], cmp_w2[1])
    n_sel = S // SEL_BLOCK
    n_pick = min(SEL_TOPN, n_sel)
    sel_start = jnp.arange(n_sel) * SEL_BLOCK
    cover = ((cmp_start[:, None] < sel_start[None, :] + SEL_BLOCK)
             & (cmp_end[:, None] >= sel_start[None, :])).astype(F32)
    ks_blk = _rms_norm(ks.reshape(kv_shape), k_gain[1]).reshape(B, n_sel, SEL_BLOCK, G, dh).transpose(0, 3, 1, 2, 4)
    vs_blk = vs.reshape(B, n_sel, SEL_BLOCK, G, dh).transpose(0, 3, 1, 2, 4)
    pad = ((0, 0), (WINDOW, 0), (0, 0), (0, 0))
    kw_pad = jnp.pad(_rms_norm(kw.reshape(kv_shape), k_gain[2]), pad)
    vw_pad = jnp.pad(vw.reshape(kv_shape), pad)
    gates = jax.nn.sigmoid((g + gate_b).astype(F32)).astype(h.dtype).reshape(B, S, 3, G, R)
    rb_group = rel_bias.reshape(NUM_BUCKETS, G, R)
    b_ix = jnp.arange(B)[:, None, None, None]
    g_ix = jnp.arange(G)[None, None, :, None]
    scale = dh ** -0.5

    def block(i):
        q0 = i * Q_BLK
        t = q0 + jnp.arange(Q_BLK)
        qb = lax.dynamic_slice_in_dim(q, q0, Q_BLK, axis=1)
        d_c = t[:, None] - cmp_end[None, :]
        lc = jnp.einsum('bqgrd,bcgd->bqgrc', qb, k_cmp, preferred_element_type=F32) * scale
        lc = lc + _head_bias(rel_bias, d_c).transpose(0, 2, 3, 1)[None]
        pc = _masked_softmax(lc, (d_c >= 0)[None, :, None, None, :])
        o_cmp = jnp.einsum('bqgrc,bcgd->bqgrd', pc.astype(v_cmp.dtype), v_cmp)
        imp = jnp.einsum('bqgrc,cn->bqgn', pc, cover)
        blk = jnp.arange(n_sel)[None, :]
        cur = (t // SEL_BLOCK)[:, None]
        forced = (blk == 0) | (blk == cur) | (blk == cur - 1)
        causal = sel_start[None, :] <= t[:, None]
        imp = jnp.where(forced[None, :, None, :], BIG, jnp.where(causal[None, :, None, :], imp, NEG))
        _, sel = lax.top_k(imp, n_pick)
        ksg = ks_blk[b_ix, g_ix, sel]
        vsg = vs_blk[b_ix, g_ix, sel]
        d_s = t[None, :, None, None, None] - (sel[..., None] * SEL_BLOCK + jnp.arange(SEL_BLOCK))
        ls = jnp.einsum('bqgrd,bqgnld->bqgrnl', qb, ksg, preferred_element_type=F32) * scale
        ls = ls + rb_group[_t5_bucket(d_s), g_ix[..., None]].transpose(0, 1, 2, 5, 3, 4)
        ps = _masked_softmax(ls, (d_s >= 0)[:, :, :, None], axis=(-2, -1))
        o_sel = jnp.einsum('bqgrnl,bqgnld->bqgrd', ps.astype(vsg.dtype), vsg)
        kwb = lax.dynamic_slice_in_dim(kw_pad, q0, WINDOW + Q_BLK, axis=1)
        vwb = lax.dynamic_slice_in_dim(vw_pad, q0, WINDOW + Q_BLK, axis=1)
        s_pos = q0 - WINDOW + jnp.arange(WINDOW + Q_BLK)
        d_w = t[:, None] - s_pos[None, :]
        vis_w = (d_w >= 0) & (d_w < WINDOW) & (s_pos[None, :] >= 0)
        lw = jnp.einsum('bqgrd,bkgd->bqgrk', qb, kwb, preferred_element_type=F32) * scale
        lw = lw + _head_bias(rel_bias, d_w).transpose(0, 2, 3, 1)[None]
        pw = _masked_softmax(lw, vis_w[None, :, None, None, :])
        o_win = jnp.einsum('bqgrk,bkgd->bqgrd', pw.astype(vwb.dtype), vwb)
        gb = lax.dynamic_slice_in_dim(gates, q0, Q_BLK, axis=1)[..., None]
        o = gb[:, :, 0] * o_cmp + gb[:, :, 1] * o_sel + gb[:, :, 2] * o_win
        return o.reshape(B, Q_BLK, AW)

    o = lax.map(block, jnp.arange(S // Q_BLK))
    return o.transpose(1, 0, 2, 3).reshape(B, S, AW) @ w_out


def _lru_combine(left, right):
    a_l, u_l = left
    a_r, u_r = right
    return a_l * a_r, a_r * u_l + u_r


def _rglru_mixer_jnp(h, w_in, conv_w, conv_b, gate_w, gate_b, lam, w_out):
    B, S, _ = h.shape
    d_rnn = conv_w.shape[1]
    gate_branch, xr = jnp.split(h @ w_in, 2, axis=-1)
    xc = lax.conv_general_dilated(xr, conv_w[:, None, :], window_strides=(1,),
                                  padding=[(CONV_WIDTH - 1, 0)],
                                  dimension_numbers=('NWC', 'WIO', 'NWC'),
                                  feature_group_count=d_rnn) + conv_b
    xb = xc.reshape(B, S, RG_BLOCKS, d_rnn // RG_BLOCKS)
    gr = jnp.einsum('bsnk,gnkj->gbsnj', xb, gate_w, preferred_element_type=F32)
    gr = gr.reshape(2, B, S, d_rnn) + gate_b[:, None, None, :].astype(F32)
    r = jax.nn.sigmoid(gr[0])
    i_g = jax.nn.sigmoid(gr[1])
    log_a = -RG_C * r * jax.nn.softplus(-lam.astype(F32))
    a = jnp.exp(log_a)
    u = jnp.sqrt(-jnp.expm1(2.0 * log_a)) * (i_g * xc.astype(F32))
    _, hs = lax.associative_scan(_lru_combine, (a, u), axis=1)
    return (jax.nn.gelu(gate_branch) * hs.astype(h.dtype)) @ w_out


def _ada_norm_jnp(x, gain, shift, scale):
    return _rms_norm(x, gain) * (1 + scale[:, None, :]) + shift[:, None, :]


def kernel(x, c, rel_bias, ada_w, ada_b, ada_table, norm_g, ffn_w_in, ffn_w_out, dsa_w_in, dsa_w_out, dsa_q_gain, dsa_k_gain, nsa_w_in, nsa_gate_b, nsa_w_out, nsa_q_gain, nsa_k_gain, nsa_cmp_pos, nsa_cmp_w1, nsa_cmp_w2, rg_w_in, rg_conv_w, rg_conv_b, rg_gate_w, rg_gate_b, rg_lambda, rg_w_out):
    B, S, D = x.shape
    depth = ada_table.shape[0]
    mod_all = _mod_all(c, ada_w, ada_b)
    xf = x.reshape(B * S, D)
    for layer in range(depth):
        tab = ada_table[layer]
        xf = _ffn(xf, mod_all, tab, norm_g[layer, 0], ffn_w_in[layer, 0], ffn_w_out[layer, 0], mod_row=0, seq=S)
        mod = mod_all + tab
        x3 = xf.reshape(B, S, D)
        y = _ada_norm_jnp(x3, norm_g[layer, 1], mod[:, 3], mod[:, 4])
        kind, j = layer % 3, layer // 3
        if kind == 0:
            m = _dsa_mixer_jnp(y, dsa_w_in[j], dsa_w_out[j], dsa_q_gain[j], dsa_k_gain[j], rel_bias)
        elif kind == 1:
            m = _nsa_mixer_jnp(y, nsa_w_in[j], nsa_gate_b[j], nsa_w_out[j], nsa_q_gain[j], nsa_k_gain[j],
                               nsa_cmp_pos[j], nsa_cmp_w1[j], nsa_cmp_w2[j], rel_bias)
        else:
            m = _rglru_mixer_jnp(y, rg_w_in[j], rg_conv_w[j], rg_conv_b[j], rg_gate_w[j], rg_gate_b[j],
                                 rg_lambda[j], rg_w_out[j])
        x3 = x3 + mod[:, 5][:, None, :] * m
        xf = x3.reshape(B * S, D)
        xf = _ffn(xf, mod_all, tab, norm_g[layer, 2], ffn_w_in[layer, 1], ffn_w_out[layer, 1], mod_row=6, seq=S)
    return xf.reshape(B, S, D)
```

```python
import functools
import math

import numpy as np
import jax
import jax.numpy as jnp
from jax import lax
from jax.experimental import pallas as pl
from jax.experimental.pallas import tpu as pltpu

N_HEADS = 32
HEAD_DIM = 128
N_KV_HEADS = 4
FFN_RES = 0.5
N_MOD = 9
NUM_BUCKETS = 32
MAX_DISTANCE = 1024
EPS = 1e-6
NEG = -1e30
BIG = 1e30
IDX_HEADS = 16
IDX_DIM = 128
TOPK_KEYS = 256
CMP_BLOCK = 32
CMP_STRIDE = 16
SEL_BLOCK = 64
SEL_TOPN = 16
WINDOW = 512
RG_BLOCKS = 16
CONV_WIDTH = 4
RG_C = 8.0

LANES = 128
SUBLANES = 8
VMEM_LIMIT_BYTES = 56 * 1024 * 1024

F32 = jnp.float32
BF16 = jnp.bfloat16
I32 = jnp.int32
INT_MIN = -2 ** 31
LOWEST = -3.0e38

TQ = 128
FAR_DIST = int(math.ceil((NUM_BUCKETS // 2) * (MAX_DISTANCE / (NUM_BUCKETS // 2)) ** ((NUM_BUCKETS // 2 - 1) / (NUM_BUCKETS // 2)))) + 8
NEAR_TILES = -(-(FAR_DIST + TQ - 1) // TQ)
CPQ = TQ // CMP_STRIDE
CMP_LAST = CMP_STRIDE * (CPQ - 1) + CMP_BLOCK - 1
NEAR_CMP = -(-(-(-(FAR_DIST + CMP_LAST) // CMP_STRIDE)) // SUBLANES) * SUBLANES

NT_DIMS = (((1,), (1,)), ((), ()))
TN_DIMS = (((0,), (0,)), ((), ()))


def _round_up(n, m):
    return (n + m - 1) // m * m


def _cparams(sem):
    return pltpu.CompilerParams(dimension_semantics=sem, vmem_limit_bytes=VMEM_LIMIT_BYTES)


def _gelu_tanh(x):
    return 0.5 * x * (1.0 + jnp.tanh(0.7978845608028654 * (x + 0.044715 * x * x * x)))


def _mod_kernel(c_ref, w_ref, b_ref, o_ref):
    c = c_ref[...]
    s = c * jax.nn.sigmoid(c)
    o_ref[...] = jnp.dot(s.astype(BF16), w_ref[...].astype(BF16), preferred_element_type=F32) + b_ref[...]


def _mod_all(c, ada_w, ada_b):
    B, D = c.shape
    N = ada_w.shape[1]
    rows = 16
    tn = next(t for t in (512, 256, 128) if N % t == 0)
    cp = jnp.zeros((rows, D), F32).at[:B].set(c)
    out = pl.pallas_call(
        _mod_kernel,
        grid=(N // tn,),
        in_specs=[pl.BlockSpec((rows, D), lambda j: (0, 0)),
                  pl.BlockSpec((D, tn), lambda j: (0, j)),
                  pl.BlockSpec((1, tn), lambda j: (0, j))],
        out_specs=pl.BlockSpec((rows, tn), lambda j: (0, j)),
        out_shape=jax.ShapeDtypeStruct((rows, N), F32),
        compiler_params=_cparams(("arbitrary",)),
        name="ada_mod",
    )(cp, ada_w, ada_b.reshape(1, N))
    return out[:B].reshape(B, N_MOD, D)


def _ada_norm_block(x, g, mod, tab, k):
    var = jnp.mean(x * x, axis=-1, keepdims=True)
    y = x * lax.rsqrt(var + EPS) * g
    shift = mod[k:k + 1, :] + tab[k:k + 1, :]
    scale = mod[k + 1:k + 2, :] + tab[k + 1:k + 2, :]
    return y * (1.0 + scale) + shift


def _norm_matmul_kernel(x_ref, mod_ref, tab_ref, g_ref, w_ref, gc_ref, *refs, mod_row, epilogue,
                        norm_tiles, n_aux, n_tiles):
    o_ref = refs[0]
    aux_refs = refs[1:1 + n_aux]
    y_scr = refs[1 + n_aux]
    j = pl.program_id(1)

    @pl.when(j == 0)
    def _():
        y = _ada_norm_block(x_ref[...], g_ref[...], mod_ref[0], tab_ref[...], mod_row)
        y_scr[...] = y.astype(BF16)

    h = jnp.dot(y_scr[...], w_ref[...], preferred_element_type=F32)
    if epilogue == "swiglu":
        tf = h.shape[1] // 2
        g = h[:, :tf]
        u = h[:, tf:]
        o_ref[...] = (g * jax.nn.sigmoid(g) * u).astype(o_ref.dtype)
        return
    if norm_tiles:
        is_norm = functools.reduce(lambda a, b: a | b, [j == t for t in norm_tiles])

        @pl.when(is_norm)
        def _():
            for c in range(h.shape[1] // LANES):
                hc = h[:, c * LANES:(c + 1) * LANES]
                ms = jnp.mean(hc * hc, axis=-1, keepdims=True)
                yc = hc * lax.rsqrt(ms + EPS) * gc_ref[:, c * LANES:(c + 1) * LANES]
                o_ref[:, c * LANES:(c + 1) * LANES] = yc.astype(o_ref.dtype)

        @pl.when(jnp.logical_not(is_norm))
        def _():
            o_ref[...] = h.astype(o_ref.dtype)
    else:
        o_ref[...] = h.astype(o_ref.dtype)
    for a in range(n_aux):
        @pl.when(j == n_tiles - n_aux + a)
        def _(a=a):
            aux_refs[a][...] = h


def _norm_matmul(x, mod_all, tab, gain, w, *, mod_row, seq, epilogue="plain", tn, tm=512,
                 out_dtype=BF16, gain_cols=None, norm_tiles=(), n_aux=0):
    T, D = x.shape
    wn = 2 * tn if epilogue == "swiglu" else tn
    n_tiles = w.shape[1] // wn
    n_out = n_tiles * tn
    tm = min(tm, seq)
    bpb = seq // tm
    if gain_cols is None:
        gain_cols = jnp.zeros((1, n_out), F32)
    out_shape = [jax.ShapeDtypeStruct((T, n_out), out_dtype)]
    out_specs = [pl.BlockSpec((tm, tn), lambda i, j: (i, j))]
    for _ in range(n_aux):
        out_shape.append(jax.ShapeDtypeStruct((T, tn), F32))
        out_specs.append(pl.BlockSpec((tm, tn), lambda i, j: (i, 0)))
    kern = functools.partial(_norm_matmul_kernel, mod_row=mod_row, epilogue=epilogue,
                             norm_tiles=tuple(norm_tiles), n_aux=n_aux, n_tiles=n_tiles)
    res = pl.pallas_call(
        kern,
        grid=(T // tm, n_tiles),
        in_specs=[pl.BlockSpec((tm, D), lambda i, j: (i, 0)),
                  pl.BlockSpec((1, N_MOD, D), lambda i, j: (i // bpb, 0, 0)),
                  pl.BlockSpec((N_MOD, D), lambda i, j: (0, 0)),
                  pl.BlockSpec((1, D), lambda i, j: (0, 0)),
                  pl.BlockSpec((D, wn), lambda i, j: (0, j)),
                  pl.BlockSpec((1, tn), lambda i, j: (0, j))],
        out_specs=out_specs,
        out_shape=out_shape,
        scratch_shapes=[pltpu.VMEM((tm, D), BF16)],
        compiler_params=_cparams(("arbitrary", "arbitrary")),
        name="norm_matmul_" + epilogue,
    )(x, mod_all, tab, gain.reshape(1, D), w, gain_cols)
    return res if n_aux else res[0]


def _matmul_res_kernel(a_ref, w_ref, x_ref, mod_ref, tab_ref, o_ref, *, mod_row, res_scale):
    h = jnp.dot(a_ref[...], w_ref[...], preferred_element_type=F32)
    gate = mod_ref[0, mod_row:mod_row + 1, :] + tab_ref[mod_row:mod_row + 1, :]
    if res_scale != 1.0:
        gate = res_scale * gate
    o_ref[...] = x_ref[...] + gate * h


def _matmul_res(a, w, x, mod_all, tab, *, mod_row, seq, res_scale=1.0, tm=512, tn=512):
    T, K = a.shape
    D = w.shape[1]
    tm = min(tm, seq)
    tn = min(tn, D)
    bpb = seq // tm
    kern = functools.partial(_matmul_res_kernel, mod_row=mod_row, res_scale=res_scale)
    return pl.pallas_call(
        kern,
        grid=(T // tm, D // tn),
        in_specs=[pl.BlockSpec((tm, K), lambda i, j: (i, 0)),
                  pl.BlockSpec((K, tn), lambda i, j: (0, j)),
                  pl.BlockSpec((tm, tn), lambda i, j: (i, j)),
                  pl.BlockSpec((1, N_MOD, tn), lambda i, j: (i // bpb, 0, j)),
                  pl.BlockSpec((N_MOD, tn), lambda i, j: (0, j))],
        out_specs=pl.BlockSpec((tm, tn), lambda i, j: (i, j)),
        out_shape=jax.ShapeDtypeStruct((T, D), F32),
        compiler_params=_cparams(("arbitrary", "arbitrary")),
        name="matmul_residual",
    )(a, w, x, mod_all, tab)


def _ffn_tile(d_ff):
    return 512 if d_ff >= 512 else _round_up(d_ff, LANES)


def _prep_ffn_weights(w_in, w_out):
    D, two_f = w_in.shape
    F = two_f // 2
    tf = _ffn_tile(F)
    Fp = _round_up(F, tf)
    g = jnp.pad(w_in[:, :F], ((0, 0), (0, Fp - F))).reshape(D, Fp // tf, 1, tf)
    u = jnp.pad(w_in[:, F:], ((0, 0), (0, Fp - F))).reshape(D, Fp // tf, 1, tf)
    w1 = jnp.concatenate([g, u], axis=2).reshape(D, 2 * Fp).astype(BF16)
    w2 = jnp.pad(w_out, ((0, Fp - F), (0, 0))).astype(BF16)
    return w1, w2, tf


def _ffn(x, mod_all, tab, gain, w_in, w_out, *, mod_row, seq):
    w1, w2, tf = _prep_ffn_weights(w_in, w_out)
    a = _norm_matmul(x, mod_all, tab, gain, w1, mod_row=mod_row, seq=seq, epilogue="swiglu", tn=tf)
    return _matmul_res(a, w2, x, mod_all, tab, mod_row=mod_row + 2, seq=seq, res_scale=FFN_RES)


def _t5_bucket(dist):
    n = jnp.maximum(dist, 0)
    max_exact = NUM_BUCKETS // 2
    nf = jnp.maximum(n, 1).astype(F32)
    large = max_exact + (jnp.log(nf / max_exact) / math.log(MAX_DISTANCE / max_exact)
                         * (NUM_BUCKETS - max_exact)).astype(I32)
    large = jnp.minimum(large, NUM_BUCKETS - 1)
    return jnp.where(n < max_exact, n, large)


def _bias_strip_kernel(rb_ref, bkt_ref, o_ref):
    h = pl.program_id(0)
    bk = bkt_ref[...]
    far = rb_ref[NUM_BUCKETS - 1, h]
    acc = jnp.zeros(bk.shape, F32)
    for k in range(NUM_BUCKETS - 1):
        acc = jnp.where(bk == k, rb_ref[k, h] - far, acc)
    o_ref[...] = acc


def _bias_strip(rel_bias, bkt):
    rows = bkt.shape[0]
    H = rel_bias.shape[1]
    return pl.pallas_call(
        _bias_strip_kernel,
        grid=(H,),
        in_specs=[pl.BlockSpec(memory_space=pltpu.SMEM),
                  pl.BlockSpec((rows, TQ), lambda h: (0, 0))],
        out_specs=pl.BlockSpec((rows, TQ), lambda h: (0, h)),
        out_shape=jax.ShapeDtypeStruct((rows, H * TQ), F32),
        compiler_params=_cparams(("arbitrary",)),
        name="bias_strip",
    )(rel_bias, bkt)


def _bias_strips(rel_bias):
    iq = jnp.arange(TQ, dtype=I32)[None, :]
    x = jnp.arange(NEAR_TILES * TQ, dtype=I32)[:, None]
    bkt_tok = _t5_bucket(iq - x + (NEAR_TILES - 1) * TQ)
    y = jnp.arange(NEAR_CMP, dtype=I32)[:, None]
    bkt_cmp = _t5_bucket(iq + CMP_STRIDE * (NEAR_CMP - 1 - y) - CMP_LAST)
    return _bias_strip(rel_bias, bkt_tok), _bias_strip(rel_bias, bkt_cmp)


def _stack_heads(q_ref, qs_scr, n):
    for r in range(n):
        qs_scr[r * TQ:(r + 1) * TQ, :] = q_ref[:, r * LANES:(r + 1) * LANES]


def _flash_init(m_scr, l_scr, acc_scr):
    m_scr[...] = jnp.full(m_scr.shape, NEG, F32)
    l_scr[...] = jnp.zeros(l_scr.shape, F32)
    acc_scr[...] = jnp.zeros(acc_scr.shape, F32)


def _flash_update(s, v_tile, m_scr, l_scr, acc_scr):
    m_prev = m_scr[...]
    m_new = jnp.maximum(m_prev, jnp.max(s, axis=0, keepdims=True))
    alpha = jnp.exp(m_prev - m_new)
    p = jnp.exp(s - m_new)
    l_scr[...] = alpha * l_scr[...] + jnp.sum(p, axis=0, keepdims=True)
    pv = lax.dot_general(v_tile, p.astype(BF16), TN_DIMS, preferred_element_type=F32)
    acc_scr[...] = alpha * acc_scr[...] + pv
    m_scr[...] = m_new


def _flash_result(m_scr, l_scr, acc_scr):
    inv = jnp.where(m_scr[...] > 0.5 * NEG, 1.0 / l_scr[...], 0.0)
    return acc_scr[...] * inv


def _write_heads(o_ref, o_t, n):
    for r in range(n):
        o_ref[:, r * LANES:(r + 1) * LANES] = o_t[:, r * TQ:(r + 1) * TQ].T.astype(o_ref.dtype)


def _dsa_index_kernel(qi_ref, ki_ref, wi_ref, mask_ref, keys_scr, qis_scr, j_scr, *, n_keep, ih, seq):
    i = pl.program_id(1)
    nkt = i + 1
    _stack_heads(qi_ref, qis_scr, ih)
    w_t = wi_ref[...].T * (ih ** -0.5 * IDX_DIM ** -0.5)
    w_rows = [w_t[h:h + 1, :] for h in range(ih)]
    q_pos = i * TQ + lax.broadcasted_iota(I32, (TQ, TQ), 1)
    k_iota = lax.broadcasted_iota(I32, (TQ, TQ), 0)

    def score_tile(kt, carry):
        k0 = pl.multiple_of(kt * TQ, TQ)
        d = lax.dot_general(ki_ref[pl.ds(k0, TQ), :], qis_scr[...], NT_DIMS, preferred_element_type=F32)
        sc = jnp.zeros((TQ, TQ), F32)
        for h in range(ih):
            sc = sc + jnp.maximum(d[:, h * TQ:(h + 1) * TQ], 0.0) * w_rows[h]
        bits = pltpu.bitcast(sc, I32)
        key = jnp.where(bits < 0, bits ^ 0x7FFFFFFF, bits)
        keys_scr[pl.ds(k0, TQ), :] = jnp.where(k0 + k_iota <= q_pos, key, INT_MIN)
        return carry

    lax.fori_loop(0, nkt, score_tile, 0)

    def count(pred):
        def body(kt, acc):
            k0 = pl.multiple_of(kt * TQ, TQ)
            return acc + jnp.where(pred(keys_scr[pl.ds(k0, TQ), :], k0 + k_iota), 1.0, 0.0)
        acc = lax.fori_loop(0, nkt, body, jnp.zeros((TQ, TQ), F32))
        return jnp.sum(acc, axis=0, keepdims=True)

    def bit_body(b, lo):
        cand = lo + jnp.left_shift(jnp.int32(1), 31 - b)
        cnt = count(lambda kk, idx: kk >= cand)
        return jnp.where(cnt >= n_keep, cand, lo)

    thr = lax.fori_loop(0, 32, bit_body, jnp.full((1, TQ), INT_MIN, I32))
    cnt_ge = count(lambda kk, idx: kk >= thr)
    has_tie = (cnt_ge > n_keep) & (thr > INT_MIN)
    j_scr[...] = jnp.full((1, TQ), seq, I32)

    @pl.when(jnp.max(jnp.where(has_tie, 1.0, 0.0)) > 0.5)
    def _():
        need = n_keep - count(lambda kk, idx: kk > thr)

        def jb(b, lo):
            cand = lo + jnp.left_shift(jnp.int32(1), (seq.bit_length() - 2) - b)
            c = count(lambda kk, idx: (kk == thr) & (idx < cand))
            return jnp.where(c < need, cand, lo)

        j_last = lax.fori_loop(0, seq.bit_length() - 1, jb, jnp.zeros((1, TQ), I32))
        j_scr[...] = jnp.where(has_tie, j_last, seq)

    thr_c = jnp.maximum(thr, INT_MIN + 1)
    j_last = j_scr[...]

    def write_tile(kt, carry):
        k0 = pl.multiple_of(kt * TQ, TQ)
        kk = keys_scr[pl.ds(k0, TQ), :]
        sel = (kk > thr_c) | ((kk == thr_c) & (k0 + k_iota <= j_last))
        mask_ref[pl.ds(k0, TQ), :] = jnp.where(sel, 0.0, NEG).astype(BF16)
        return carry

    lax.fori_loop(0, nkt, write_tile, 0)

    def fill_tile(kt, carry):
        k0 = pl.multiple_of(kt * TQ, TQ)
        mask_ref[pl.ds(k0, TQ), :] = jnp.full((TQ, TQ), NEG, BF16)
        return carry

    lax.fori_loop(nkt, seq // TQ, fill_tile, 0)


def _dsa_index(proj, aux, *, batch, seq, qi_blk, ki_blk, n_keep):
    nq = seq // TQ
    ihw = IDX_HEADS * IDX_DIM
    kern = functools.partial(_dsa_index_kernel, n_keep=n_keep, ih=IDX_HEADS, seq=seq)
    return pl.pallas_call(
        kern,
        grid=(batch, nq),
        in_specs=[pl.BlockSpec((TQ, ihw), lambda b, i: (b * nq + i, qi_blk)),
                  pl.BlockSpec((seq, IDX_DIM), lambda b, i: (b, ki_blk)),
                  pl.BlockSpec((TQ, LANES), lambda b, i: (b * nq + i, 1))],
        out_specs=pl.BlockSpec((None, seq, TQ), lambda b, i: (b, 0, i)),
        out_shape=jax.ShapeDtypeStruct((batch, seq, seq), BF16),
        scratch_shapes=[pltpu.VMEM((seq, TQ), I32),
                        pltpu.VMEM((IDX_HEADS * TQ, IDX_DIM), BF16),
                        pltpu.VMEM((1, TQ), I32)],
        compiler_params=_cparams(("arbitrary", "arbitrary")),
        name="dsa_index",
    )(proj, proj, aux)


def _dsa_attn_kernel(q_ref, k_ref, v_ref, mask_ref, strip_ref, o_ref, qs_scr, m_scr, l_scr, acc_scr, *, r_heads):
    i = pl.program_id(2)
    _stack_heads(q_ref, qs_scr, r_heads)
    _flash_init(m_scr, l_scr, acc_scr)

    def tile(kt, bias):
        k0 = pl.multiple_of(kt * TQ, TQ)
        s = lax.dot_general(k_ref[pl.ds(k0, TQ), :], qs_scr[...], NT_DIMS, preferred_element_type=F32)
        s = s + jnp.tile(mask_ref[pl.ds(k0, TQ), :].astype(F32), (1, r_heads))
        if bias is not None:
            s = s + bias
        _flash_update(s, v_ref[pl.ds(k0, TQ), :], m_scr, l_scr, acc_scr)

    def far_body(kt, carry):
        tile(kt, None)
        return carry

    lax.fori_loop(0, jnp.maximum(i - (NEAR_TILES - 1), 0), far_body, 0)
    for j in range(NEAR_TILES - 1, -1, -1):
        @pl.when(i >= j)
        def _(j=j):
            tile(i - j, strip_ref[(NEAR_TILES - 1 - j) * TQ:(NEAR_TILES - j) * TQ, :])

    _write_heads(o_ref, _flash_result(m_scr, l_scr, acc_scr), r_heads)


def _dsa_attn(proj, mask, strip, *, batch, seq, k_blk, v_blk):
    nq = seq // TQ
    G = N_KV_HEADS
    R = N_HEADS // G
    W = R * TQ
    kern = functools.partial(_dsa_attn_kernel, r_heads=R)
    return pl.pallas_call(
        kern,
        grid=(batch, G, nq),
        in_specs=[pl.BlockSpec((TQ, R * HEAD_DIM), lambda b, g, i: (b * nq + i, g)),
                  pl.BlockSpec((seq, HEAD_DIM), lambda b, g, i: (b, k_blk + g)),
                  pl.BlockSpec((seq, HEAD_DIM), lambda b, g, i: (b, v_blk + g)),
                  pl.BlockSpec((None, seq, TQ), lambda b, g, i: (b, 0, i)),
                  pl.BlockSpec((NEAR_TILES * TQ, W), lambda b, g, i: (0, g))],
        out_specs=pl.BlockSpec((TQ, R * HEAD_DIM), lambda b, g, i: (b * nq + i, g)),
        out_shape=jax.ShapeDtypeStruct((batch * seq, N_HEADS * HEAD_DIM), BF16),
        scratch_shapes=[pltpu.VMEM((W, HEAD_DIM), BF16),
                        pltpu.VMEM((1, W), F32), pltpu.VMEM((1, W), F32), pltpu.VMEM((HEAD_DIM, W), F32)],
        compiler_params=_cparams(("arbitrary", "arbitrary", "arbitrary")),
        name="dsa_attn",
    )(proj, proj, proj, mask, strip)


def _proj_layout(w_in, segments, tn):
    cols = []
    for start, width in segments:
        pad = _round_up(width, tn) - width
        cols.append(jnp.pad(w_in[:, start:start + width], ((0, 0), (0, pad))))
    return jnp.concatenate(cols, axis=1).astype(BF16)


def _dsa_mixer(xf, mod_all, tab, gain, w_in, w_out, q_gain, k_gain, strip, *, batch, seq):
    AW, KW = N_HEADS * HEAD_DIM, N_KV_HEADS * HEAD_DIM
    IW = IDX_HEADS * IDX_DIM
    tn = KW
    wp = _proj_layout(w_in, [(0, AW), (AW + 2 * KW, IW), (AW, KW), (AW + KW, KW), (AW + 2 * KW + IW, IDX_DIM + IDX_HEADS)], tn)
    nq_t, ni_t = AW // tn, IW // tn
    gain_cols = jnp.concatenate([jnp.tile(q_gain * HEAD_DIM ** -0.5, N_HEADS), jnp.zeros((IW,), F32),
                                 jnp.tile(k_gain, N_KV_HEADS), jnp.zeros((2 * tn,), F32)]).reshape(1, -1)
    proj, aux = _norm_matmul(xf, mod_all, tab, gain, wp, mod_row=3, seq=seq, tn=tn, gain_cols=gain_cols,
                             norm_tiles=tuple(range(nq_t)) + (nq_t + ni_t,), n_aux=1)
    k_blk = (nq_t + ni_t) * tn // HEAD_DIM
    mask = _dsa_index(proj, aux, batch=batch, seq=seq, qi_blk=AW // IW, ki_blk=(nq_t + ni_t + 2) * tn // IDX_DIM,
                      n_keep=min(TOPK_KEYS, seq // 4))
    o = _dsa_attn(proj, mask, strip, batch=batch, seq=seq, k_blk=k_blk, v_blk=k_blk + tn // HEAD_DIM)
    return _matmul_res(o, w_out.astype(BF16), xf, mod_all, tab, mod_row=5, seq=seq)


def _nsa_compress_kernel(xk_ref, xkn_ref, xv_ref, xvn_ref, pos_ref, w1_ref, w2_ref, gain_ref, ok_ref, ov_ref, *, groups, kw):
    half = CMP_BLOCK // 2
    rows = xk_ref.shape[0]
    row_id = lax.broadcasted_iota(I32, (rows, HEAD_DIM), 0)

    def branch(br, x_ref, xn_ref, o_ref):
        for g in range(groups):
            p1 = jnp.zeros((rows, HEAD_DIM), F32)
            p2 = jnp.zeros((rows, HEAD_DIM), F32)
            p2n = jnp.zeros((xn_ref.shape[0], HEAD_DIM), F32)
            for r in range(half):
                c0 = r * kw + g * HEAD_DIM
                xa = x_ref[:, c0:c0 + HEAD_DIM]
                w_lo = w1_ref[br, r * HEAD_DIM:(r + 1) * HEAD_DIM, :]
                w_hi = w1_ref[br, (half + r) * HEAD_DIM:(half + r + 1) * HEAD_DIM, :]
                p1 = p1 + jnp.dot((xa + pos_ref[br, r:r + 1, :]).astype(BF16), w_lo, preferred_element_type=F32)
                p2 = p2 + jnp.dot((xa + pos_ref[br, half + r:half + r + 1, :]).astype(BF16), w_hi,
                                  preferred_element_type=F32)
                xb = xn_ref[:, c0:c0 + HEAD_DIM]
                p2n = p2n + jnp.dot((xb + pos_ref[br, half + r:half + r + 1, :]).astype(BF16), w_hi,
                                    preferred_element_type=F32)
            p2s = jnp.where(row_id == rows - 1, p2n[0:1, :], pltpu.roll(p2, rows - 1, 0))
            hid = _gelu_tanh(p1 + p2s)
            out = jnp.dot(hid.astype(BF16), w2_ref[br], preferred_element_type=F32)
            if br == 0:
                ms = jnp.mean(out * out, axis=-1, keepdims=True)
                out = out * lax.rsqrt(ms + EPS) * gain_ref[...]
            o_ref[:, g * HEAD_DIM:(g + 1) * HEAD_DIM] = out.astype(o_ref.dtype)

    branch(0, xk_ref, xkn_ref, ok_ref)
    branch(1, xv_ref, xvn_ref, ov_ref)


def _nsa_compress(kc, vc, cmp_pos, cmp_w1, cmp_w2, k_gain0, *, batch, seq):
    assert CMP_BLOCK == 2 * CMP_STRIDE
    KW = N_KV_HEADS * HEAD_DIM
    ncp = seq // CMP_STRIDE
    rb = min(TQ, ncp)
    nrb = ncp // rb
    nxt = 16
    wide = CMP_STRIDE * KW
    xk = kc.reshape(batch * ncp, wide)
    xv = vc.reshape(batch * ncp, wide)
    last_nxt = batch * ncp // nxt - 1
    main = pl.BlockSpec((rb, wide), lambda b, r: (b * nrb + r, 0))
    ahead = pl.BlockSpec((nxt, wide), lambda b, r: (jnp.minimum((b * nrb + r + 1) * (rb // nxt), last_nxt), 0))
    kern = functools.partial(_nsa_compress_kernel, groups=N_KV_HEADS, kw=KW)
    return pl.pallas_call(
        kern,
        grid=(batch, nrb),
        in_specs=[main, ahead, main, ahead,
                  pl.BlockSpec((2, CMP_BLOCK, HEAD_DIM), lambda b, r: (0, 0, 0)),
                  pl.BlockSpec((2, CMP_BLOCK * HEAD_DIM, HEAD_DIM), lambda b, r: (0, 0, 0)),
                  pl.BlockSpec((2, HEAD_DIM, HEAD_DIM), lambda b, r: (0, 0, 0)),
                  pl.BlockSpec((1, HEAD_DIM), lambda b, r: (0, 0))],
        out_specs=[pl.BlockSpec((rb, KW), lambda b, r: (b * nrb + r, 0))] * 2,
        out_shape=[jax.ShapeDtypeStruct((batch * ncp, KW), BF16)] * 2,
        compiler_params=_cparams(("arbitrary", "arbitrary")),
        name="nsa_compress",
    )(xk, xk, xv, xv, cmp_pos, cmp_w1.astype(BF16), cmp_w2.astype(BF16), k_gain0.reshape(1, HEAD_DIM))


def _nsa_attn_kernel(q_ref, kc_ref, vc_ref, ks_ref, vs_ref, kw_ref, vw_ref, g_ref, gb_ref, strip_ref, ustrip_ref,
                     cover_ref, o_ref, qs_scr, lc_scr, sel_scr, gt_scr, m_scr, l_scr, acc_scr, ocmp_scr, osel_scr,
                     *, r_heads, n_pick):
    g = pl.program_id(1)
    i = pl.program_id(2)
    R = r_heads
    ncp = kc_ref.shape[0]
    n_sel = cover_ref.shape[0]
    pad = NEAR_CMP - CPQ
    _stack_heads(q_ref, qs_scr, R)
    q_iota = lax.broadcasted_iota(I32, (TQ, TQ), 1)
    k_iota = lax.broadcasted_iota(I32, (TQ, TQ), 0)

    lc_scr[0:pad, :] = jnp.zeros((pad, R * TQ), F32)
    lc_scr[pad:pad + ncp, :] = lax.dot_general(kc_ref[...], qs_scr[...], NT_DIMS, preferred_element_type=F32)
    off = pl.multiple_of(i * CPQ, SUBLANES)
    lc_scr[pl.ds(off, NEAR_CMP), :] += ustrip_ref[...]
    c_iota = lax.broadcasted_iota(I32, (ncp, TQ), 0)
    t_pos = i * TQ + lax.broadcasted_iota(I32, (ncp, TQ), 1)
    vis = jnp.where(c_iota * CMP_STRIDE + (CMP_BLOCK - 1) <= t_pos, 0.0, NEG)
    s = lc_scr[pad:pad + ncp, :] + jnp.tile(vis, (1, R))
    m = jnp.max(s, axis=0, keepdims=True)
    p = jnp.exp(s - m)
    inv = jnp.where(m > 0.5 * NEG, 1.0 / jnp.sum(p, axis=0, keepdims=True), 0.0)
    pc = p * inv
    ocmp_scr[...] = lax.dot_general(vc_ref[...], pc.astype(BF16), TN_DIMS, preferred_element_type=F32)

    psum = pc[:, 0:TQ]
    for r in range(1, R):
        psum = psum + pc[:, r * TQ:(r + 1) * TQ]
    hi = psum.astype(BF16)
    lo = (psum - hi.astype(F32)).astype(BF16)
    imp = (jnp.dot(cover_ref[...], hi, preferred_element_type=F32)
           + jnp.dot(cover_ref[...], lo, preferred_element_type=F32))
    n_io = lax.broadcasted_iota(I32, (n_sel, TQ), 0)
    t_sel = i * TQ + lax.broadcasted_iota(I32, (n_sel, TQ), 1)
    cur = jnp.right_shift(t_sel, SEL_BLOCK.bit_length() - 1)
    forced = (n_io == 0) | (n_io == cur) | (n_io == cur - 1)
    val0 = jnp.where(forced, BIG, jnp.where(n_io * SEL_BLOCK <= t_sel, imp, NEG))
    n_f = n_io.astype(F32)

    def pick_round(_, carry):
        val, selm = carry
        cm = jnp.max(val, axis=0, keepdims=True)
        idx = jnp.min(jnp.where(val == cm, n_f, 1e9), axis=0, keepdims=True)
        pick = n_f == idx
        return jnp.where(pick, LOWEST, val), jnp.where(pick, 0.0, selm)

    _, selm = lax.fori_loop(0, n_pick, pick_round, (val0, jnp.full((n_sel, TQ), NEG, F32)))
    sel_scr[...] = selm

    bpt = TQ // SEL_BLOCK
    _flash_init(m_scr, l_scr, acc_scr)

    def sel_tile(kt, bias, diag):
        k0 = pl.multiple_of(kt * TQ, TQ)
        s = lax.dot_general(ks_ref[pl.ds(k0, TQ), :], qs_scr[...], NT_DIMS, preferred_element_type=F32)
        bm = jnp.concatenate([jnp.broadcast_to(sel_scr[pl.ds(kt * bpt + b, 1), :], (SEL_BLOCK, TQ))
                              for b in range(bpt)], axis=0)
        if diag:
            bm = jnp.where(k_iota <= q_iota, bm, NEG)
        s = s + jnp.tile(bm, (1, R))
        if bias is not None:
            s = s + bias
        _flash_update(s, vs_ref[pl.ds(k0, TQ), :], m_scr, l_scr, acc_scr)

    def far_body(kt, carry):
        sel_tile(kt, None, False)
        return carry

    lax.fori_loop(0, jnp.maximum(i - (NEAR_TILES - 1), 0), far_body, 0)
    for j in range(NEAR_TILES - 1, -1, -1):
        @pl.when(i >= j)
        def _(j=j):
            sel_tile(i - j, strip_ref[(NEAR_TILES - 1 - j) * TQ:(NEAR_TILES - j) * TQ, :], j == 0)

    osel_scr[...] = _flash_result(m_scr, l_scr, acc_scr)

    wt = WINDOW // TQ
    _flash_init(m_scr, l_scr, acc_scr)
    for j in range(wt, -1, -1):
        @pl.when(i >= j)
        def _(j=j):
            k0 = pl.multiple_of((i - j) * TQ, TQ)
            s = lax.dot_general(kw_ref[pl.ds(k0, TQ), :], qs_scr[...], NT_DIMS, preferred_element_type=F32)
            s = s + strip_ref[(NEAR_TILES - 1 - j) * TQ:(NEAR_TILES - j) * TQ, :]
            if j == 0:
                s = s + jnp.tile(jnp.where(k_iota <= q_iota, 0.0, NEG), (1, R))
            elif j == wt:
                s = s + jnp.tile(jnp.where(k_iota > q_iota, 0.0, NEG), (1, R))
            _flash_update(s, vw_ref[pl.ds(k0, TQ), :], m_scr, l_scr, acc_scr)

    o_win = _flash_result(m_scr, l_scr, acc_scr)

    gt_scr[...] = jax.nn.sigmoid(g_ref[:, 0:LANES] + gb_ref[...]).T

    def gate_row(br):
        rows = gt_scr[pl.ds(pl.multiple_of(br * N_HEADS + g * R, SUBLANES), R), :]
        return jnp.concatenate([rows[r:r + 1, :] for r in range(R)], axis=1)

    o_t = gate_row(0) * ocmp_scr[...] + gate_row(1) * osel_scr[...] + gate_row(2) * o_win
    _write_heads(o_ref, o_t, R)


def _nsa_attn(proj, aux_g, gate_b, k_cmp, v_cmp, strip, ustrip, *, batch, seq, blks):
    nq = seq // TQ
    G = N_KV_HEADS
    R = N_HEADS // G
    assert R == SUBLANES and 3 * N_HEADS <= LANES and WINDOW % TQ == 0 and WINDOW // TQ < NEAR_TILES
    W = R * TQ
    ncp = seq // CMP_STRIDE
    n_cmp = (seq - CMP_BLOCK) // CMP_STRIDE + 1
    n_sel = seq // SEL_BLOCK
    cs = np.arange(ncp)[None, :] * CMP_STRIDE
    ss = np.arange(n_sel)[:, None] * SEL_BLOCK
    cover = ((cs < ss + SEL_BLOCK) & (cs + CMP_BLOCK - 1 >= ss) & (np.arange(ncp)[None, :] < n_cmp))
    cover = jnp.asarray(cover.astype(np.float32), dtype=BF16)
    gb = jnp.zeros((1, LANES), F32).at[0, :3 * N_HEADS].set(gate_b)
    kv = lambda blk: pl.BlockSpec((seq, HEAD_DIM), lambda b, g, i: (b, blk + g))
    cmp_spec = pl.BlockSpec((ncp, HEAD_DIM), lambda b, g, i: (b, g))
    kern = functools.partial(_nsa_attn_kernel, r_heads=R, n_pick=min(SEL_TOPN, n_sel))
    return pl.pallas_call(
        kern,
        grid=(batch, G, nq),
        in_specs=[pl.BlockSpec((TQ, W), lambda b, g, i: (b * nq + i, g)),
                  cmp_spec, cmp_spec, kv(blks["ks"]), kv(blks["vs"]), kv(blks["kw"]), kv(blks["vw"]),
                  pl.BlockSpec((TQ, aux_g.shape[1]), lambda b, g, i: (b * nq + i, 0)),
                  pl.BlockSpec((1, LANES), lambda b, g, i: (0, 0)),
                  pl.BlockSpec((NEAR_TILES * TQ, W), lambda b, g, i: (0, g)),
                  pl.BlockSpec((NEAR_CMP, W), lambda b, g, i: (0, g)),
                  pl.BlockSpec((n_sel, ncp), lambda b, g, i: (0, 0))],
        out_specs=pl.BlockSpec((TQ, W), lambda b, g, i: (b * nq + i, g)),
        out_shape=jax.ShapeDtypeStruct((batch * seq, N_HEADS * HEAD_DIM), BF16),
        scratch_shapes=[pltpu.VMEM((W, HEAD_DIM), BF16),
                        pltpu.VMEM((NEAR_CMP - CPQ + ncp, W), F32),
                        pltpu.VMEM((n_sel, TQ), F32),
                        pltpu.VMEM((LANES, TQ), F32),
                        pltpu.VMEM((1, W), F32), pltpu.VMEM((1, W), F32), pltpu.VMEM((HEAD_DIM, W), F32),
                        pltpu.VMEM((HEAD_DIM, W), F32), pltpu.VMEM((HEAD_DIM, W), F32)],
        compiler_params=_cparams(("arbitrary", "arbitrary", "arbitrary")),
        name="nsa_attn",
    )(proj, k_cmp, v_cmp, proj, proj, proj, proj, aux_g, gb, strip, ustrip, cover)


def _nsa_mixer(xf, mod_all, tab, gain, w_in, gate_b, w_out, q_gain, k_gain, cmp_pos, cmp_w1, cmp_w2, strip, ustrip,
               *, batch, seq):
    AW, KW = N_HEADS * HEAD_DIM, N_KV_HEADS * HEAD_DIM
    tn = KW
    wp = _proj_layout(w_in, [(0, AW), (AW + 2 * KW, KW), (AW + 3 * KW, KW), (AW + 4 * KW, KW), (AW + 5 * KW, KW),
                             (AW, KW), (AW + KW, KW), (AW + 6 * KW, 3 * N_HEADS)], tn)
    nq_t = AW // tn
    gain_cols = jnp.concatenate([jnp.tile(q_gain * HEAD_DIM ** -0.5, N_HEADS), jnp.tile(k_gain[1], N_KV_HEADS),
                                 jnp.zeros((tn,), F32), jnp.tile(k_gain[2], N_KV_HEADS),
                                 jnp.zeros((4 * tn,), F32)]).reshape(1, -1)
    proj, kc, vc, aux_g = _norm_matmul(xf, mod_all, tab, gain, wp, mod_row=3, seq=seq, tn=tn, gain_cols=gain_cols,
                                       norm_tiles=tuple(range(nq_t)) + (nq_t, nq_t + 2), n_aux=3)
    k_cmp, v_cmp = _nsa_compress(kc, vc, cmp_pos, cmp_w1, cmp_w2, k_gain[0], batch=batch, seq=seq)
    per = tn // HEAD_DIM
    blks = {"ks": nq_t * per, "vs": (nq_t + 1) * per, "kw": (nq_t + 2) * per, "vw": (nq_t + 3) * per}
    o = _nsa_attn(proj, aux_g, gate_b, k_cmp, v_cmp, strip, ustrip, batch=batch, seq=seq, blks=blks)
    return _matmul_res(o, w_out.astype(BF16), xf, mod_all, tab, mod_row=5, seq=seq)


def _rglru_kernel(gbr_ref, xr_ref, cw_ref, cb_ref, gw_ref, gbias_ref, lam_ref, o_ref, h_scr, tail_scr, *, ts, bd):
    @pl.when(pl.program_id(2) == 0)
    def _():
        h_scr[...] = jnp.zeros(h_scr.shape, F32)
        tail_scr[...] = jnp.zeros(tail_scr.shape, F32)

    x = xr_ref[...]
    cbw = x.shape[1]
    xfull = jnp.concatenate([tail_scr[...], x], axis=0)
    xc = cw_ref[CONV_WIDTH - 1:CONV_WIDTH, :] * x + cb_ref[...]
    for w in range(1, CONV_WIDTH):
        xc = xc + cw_ref[CONV_WIDTH - 1 - w:CONV_WIDTH - w, :] * pltpu.roll(xfull, w, 0)[SUBLANES:, :]
    tail_scr[...] = x[ts - SUBLANES:ts, :]

    gr = []
    for gi in range(2):
        parts = [jnp.dot(xc[:, n * bd:(n + 1) * bd].astype(BF16), gw_ref[gi, n], preferred_element_type=F32)
                 for n in range(cbw // bd)]
        gr.append(jnp.concatenate(parts, axis=1) + gbias_ref[gi:gi + 1, :])
    r = jax.nn.sigmoid(gr[0])
    i_g = jax.nn.sigmoid(gr[1])
    nl = -lam_ref[...]
    softplus = jnp.maximum(nl, 0.0) + jnp.log1p(jnp.exp(-jnp.abs(nl)))
    log_a = -RG_C * r * softplus
    a = jnp.exp(log_a)
    u = jnp.sqrt(1.0 - jnp.exp(2.0 * log_a)) * (i_g * xc)

    row = lax.broadcasted_iota(I32, (ts, cbw), 0)
    sft = 1
    while sft < ts:
        keep = row >= sft
        a_sh = jnp.where(keep, pltpu.roll(a, sft, 0), 1.0)
        u_sh = jnp.where(keep, pltpu.roll(u, sft, 0), 0.0)
        u = u + a * u_sh
        a = a * a_sh
        sft *= 2
    hs = u + a * h_scr[...]
    h_scr[...] = hs[ts - 1:ts, :]
    o_ref[...] = (_gelu_tanh(gbr_ref[...]) * hs).astype(o_ref.dtype)


def _rglru_mixer(xf, mod_all, tab, gain, w_in, conv_w, conv_b, gate_w, gate_b, lam, w_out, *, batch, seq):
    d_rnn = conv_w.shape[1]
    bd = d_rnn // RG_BLOCKS
    tn = min(512, d_rnn)
    proj = _norm_matmul(xf, mod_all, tab, gain, w_in.astype(BF16), mod_row=3, seq=seq, tn=tn, out_dtype=F32)
    cbw = min(512, d_rnn)
    ncb = d_rnn // cbw
    ts = min(256, seq)
    nts = seq // ts
    kern = functools.partial(_rglru_kernel, ts=ts, bd=bd)
    y = pl.pallas_call(
        kern,
        grid=(batch, ncb, nts),
        in_specs=[pl.BlockSpec((ts, cbw), lambda b, c, t: (b * nts + t, c)),
                  pl.BlockSpec((ts, cbw), lambda b, c, t: (b * nts + t, ncb + c)),
                  pl.BlockSpec((CONV_WIDTH, cbw), lambda b, c, t: (0, c)),
                  pl.BlockSpec((1, cbw), lambda b, c, t: (0, c)),
                  pl.BlockSpec((2, cbw // bd, bd, bd), lambda b, c, t: (0, c, 0, 0)),
                  pl.BlockSpec((2, cbw), lambda b, c, t: (0, c)),
                  pl.BlockSpec((1, cbw), lambda b, c, t: (0, c))],
        out_specs=pl.BlockSpec((ts, cbw), lambda b, c, t: (b * nts + t, c)),
        out_shape=jax.ShapeDtypeStruct((batch * seq, d_rnn), BF16),
        scratch_shapes=[pltpu.VMEM((1, cbw), F32), pltpu.VMEM((SUBLANES, cbw), F32)],
        compiler_params=_cparams(("arbitrary", "arbitrary", "arbitrary")),
        name="rglru",
    )(proj, proj, conv_w, conv_b.reshape(1, d_rnn), gate_w.astype(BF16), gate_b, lam.reshape(1, d_rnn))
    return _matmul_res(y, w_out.astype(BF16), xf, mod_all, tab, mod_row=5, seq=seq)


def kernel(x, c, rel_bias, ada_w, ada_b, ada_table, norm_g, ffn_w_in, ffn_w_out, dsa_w_in, dsa_w_out, dsa_q_gain, dsa_k_gain, nsa_w_in, nsa_gate_b, nsa_w_out, nsa_q_gain, nsa_k_gain, nsa_cmp_pos, nsa_cmp_w1, nsa_cmp_w2, rg_w_in, rg_conv_w, rg_conv_b, rg_gate_w, rg_gate_b, rg_lambda, rg_w_out):
    B, S, D = x.shape
    depth = ada_table.shape[0]
    assert S % TQ == 0 and S >= NEAR_TILES * TQ
    mod_all = _mod_all(c, ada_w, ada_b)
    strip, ustrip = _bias_strips(rel_bias)
    xf = x.reshape(B * S, D)
    for layer in range(depth):
        tab = ada_table[layer]
        xf = _ffn(xf, mod_all, tab, norm_g[layer, 0], ffn_w_in[layer, 0], ffn_w_out[layer, 0], mod_row=0, seq=S)
        kind, j = layer % 3, layer // 3
        if kind == 0:
            xf = _dsa_mixer(xf, mod_all, tab, norm_g[layer, 1], dsa_w_in[j], dsa_w_out[j], dsa_q_gain[j],
                            dsa_k_gain[j], strip, batch=B, seq=S)
        elif kind == 1:
            xf = _nsa_mixer(xf, mod_all, tab, norm_g[layer, 1], nsa_w_in[j], nsa_gate_b[j], nsa_w_out[j],
                            nsa_q_gain[j], nsa_k_gain[j], nsa_cmp_pos[j], nsa_cmp_w1[j], nsa_cmp_w2[j],
                            strip, ustrip, batch=B, seq=S)
        else:
            xf = _rglru_mixer(xf, mod_all, tab, norm_g[layer, 1], rg_w_in[j], rg_conv_w[j], rg_conv_b[j],
                              rg_gate_w[j], rg_gate_b[j], rg_lambda[j], rg_w_out[j], batch=B, seq=S)
        xf = _ffn(xf, mod_all, tab, norm_g[layer, 2], ffn_w_in[layer, 1], ffn_w_out[layer, 1], mod_row=6, seq=S)
    return xf.reshape(B, S, D)
```

```python
import functools
import math

import numpy as np
import jax
import jax.numpy as jnp
from jax import lax
from jax.experimental import pallas as pl
from jax.experimental.pallas import tpu as pltpu

N_HEADS = 32
HEAD_DIM = 128
N_KV_HEADS = 4
FFN_RES = 0.5
N_MOD = 9
NUM_BUCKETS = 32
MAX_DISTANCE = 1024
EPS = 1e-6
NEG = -1e30
BIG = 1e30
IDX_HEADS = 16
IDX_DIM = 128
TOPK_KEYS = 256
CMP_BLOCK = 32
CMP_STRIDE = 16
SEL_BLOCK = 64
SEL_TOPN = 16
WINDOW = 512
RG_BLOCKS = 16
CONV_WIDTH = 4
RG_C = 8.0

LANES = 128
SUBLANES = 8
VMEM_LIMIT_BYTES = 56 * 1024 * 1024

F32 = jnp.float32
BF16 = jnp.bfloat16
I32 = jnp.int32
INT_MIN = -2 ** 31
LOWEST = -3.0e38

TQ = 128
TK = 512
KPT = TK // TQ
HPC = 8
LOG2E = 1.4426950408889634
FAR_DIST = int(math.ceil((NUM_BUCKETS // 2) * (MAX_DISTANCE / (NUM_BUCKETS // 2)) ** ((NUM_BUCKETS // 2 - 1) / (NUM_BUCKETS // 2)))) + 8
NEAR_TILES = -(-(FAR_DIST + TQ - 1) // TQ)
CPQ = TQ // CMP_STRIDE
CMP_LAST = CMP_STRIDE * (CPQ - 1) + CMP_BLOCK - 1
NEAR_CMP = -(-(-(-(FAR_DIST + CMP_LAST) // CMP_STRIDE)) // SUBLANES) * SUBLANES

STRIP_PAD = TK - TQ
STRIP_ROWS = NEAR_TILES * TQ + 2 * STRIP_PAD
NT_DIMS = (((1,), (1,)), ((), ()))
TN_DIMS = (((0,), (0,)), ((), ()))


def _round_up(n, m):
    return (n + m - 1) // m * m


def _cparams(sem):
    return pltpu.CompilerParams(dimension_semantics=sem, vmem_limit_bytes=VMEM_LIMIT_BYTES)


def _gelu_tanh(x):
    return 0.5 * x * (1.0 + jnp.tanh(0.7978845608028654 * (x + 0.044715 * x * x * x)))


def _mod_kernel(c_ref, w_ref, b_ref, o_ref):
    c = c_ref[...]
    s = c * jax.nn.sigmoid(c)
    o_ref[...] = jnp.dot(s.astype(BF16), w_ref[...].astype(BF16), preferred_element_type=F32) + b_ref[...]


def _mod_all(c, ada_w, ada_b):
    B, D = c.shape
    N = ada_w.shape[1]
    rows = 16
    tn = next(t for t in (512, 256, 128) if N % t == 0)
    cp = jnp.zeros((rows, D), F32).at[:B].set(c)
    out = pl.pallas_call(
        _mod_kernel,
        grid=(N // tn,),
        in_specs=[pl.BlockSpec((rows, D), lambda j: (0, 0)),
                  pl.BlockSpec((D, tn), lambda j: (0, j)),
                  pl.BlockSpec((1, tn), lambda j: (0, j))],
        out_specs=pl.BlockSpec((rows, tn), lambda j: (0, j)),
        out_shape=jax.ShapeDtypeStruct((rows, N), F32),
        compiler_params=_cparams(("arbitrary",)),
        name="ada_mod",
    )(cp, ada_w, ada_b.reshape(1, N))
    return out[:B].reshape(B, N_MOD, D)


def _ada_norm_block(x, g, mod, tab, k):
    var = jnp.mean(x * x, axis=-1, keepdims=True)
    y = x * lax.rsqrt(var + EPS) * g
    shift = mod[k:k + 1, :] + tab[k:k + 1, :]
    scale = mod[k + 1:k + 2, :] + tab[k + 1:k + 2, :]
    return y * (1.0 + scale) + shift


def _norm_matmul_kernel(x_ref, mod_ref, tab_ref, g_ref, w_ref, gc_ref, *refs, mod_row, epilogue,
                        norm_tiles, n_aux, n_tiles):
    if epilogue == "swiglu":
        wu_ref, refs = refs[0], refs[1:]
    o_ref = refs[0]
    aux_refs = refs[1:1 + n_aux]
    y_scr = refs[1 + n_aux]
    j = pl.program_id(1)

    @pl.when(j == 0)
    def _():
        y = _ada_norm_block(x_ref[...], g_ref[...], mod_ref[0], tab_ref[...], mod_row)
        y_scr[...] = y.astype(BF16)

    h = jnp.dot(y_scr[...], w_ref[...], preferred_element_type=F32)
    if epilogue == "swiglu":
        u = jnp.dot(y_scr[...], wu_ref[...], preferred_element_type=F32)
        o_ref[...] = (h * jax.nn.sigmoid(h) * u).astype(o_ref.dtype)
        return
    if norm_tiles:
        is_norm = functools.reduce(lambda a, b: a | b, [j == t for t in norm_tiles])

        @pl.when(is_norm)
        def _():
            for c in range(h.shape[1] // LANES):
                hc = h[:, c * LANES:(c + 1) * LANES]
                ms = jnp.mean(hc * hc, axis=-1, keepdims=True)
                yc = hc * lax.rsqrt(ms + EPS) * gc_ref[:, c * LANES:(c + 1) * LANES]
                o_ref[:, c * LANES:(c + 1) * LANES] = yc.astype(o_ref.dtype)

        @pl.when(jnp.logical_not(is_norm))
        def _():
            o_ref[...] = h.astype(o_ref.dtype)
    else:
        o_ref[...] = h.astype(o_ref.dtype)
    for a in range(n_aux):
        @pl.when(j == n_tiles - n_aux + a)
        def _(a=a):
            aux_refs[a][...] = h


def _norm_matmul(x, mod_all, tab, gain, w, *, mod_row, seq, epilogue="plain", tn, tm=512,
                 out_dtype=BF16, gain_cols=None, norm_tiles=(), n_aux=0):
    T, D = x.shape
    swiglu = epilogue == "swiglu"
    n_tiles = w.shape[1] // (2 * tn if swiglu else tn)
    n_out = n_tiles * tn
    tm = min(tm, seq)
    bpb = seq // tm
    if gain_cols is None:
        gain_cols = jnp.zeros((1, n_out), F32)
    out_shape = [jax.ShapeDtypeStruct((T, n_out), out_dtype)]
    out_specs = [pl.BlockSpec((tm, tn), lambda i, j: (i, j))]
    for _ in range(n_aux):
        out_shape.append(jax.ShapeDtypeStruct((T, tn), F32))
        out_specs.append(pl.BlockSpec((tm, tn), lambda i, j: (i, 0)))
    kern = functools.partial(_norm_matmul_kernel, mod_row=mod_row, epilogue=epilogue,
                             norm_tiles=tuple(norm_tiles), n_aux=n_aux, n_tiles=n_tiles)
    in_specs = [pl.BlockSpec((tm, D), lambda i, j: (i, 0)),
                pl.BlockSpec((1, N_MOD, D), lambda i, j: (i // bpb, 0, 0)),
                pl.BlockSpec((N_MOD, D), lambda i, j: (0, 0)),
                pl.BlockSpec((1, D), lambda i, j: (0, 0)),
                pl.BlockSpec((D, tn), lambda i, j: (0, j)),
                pl.BlockSpec((1, tn), lambda i, j: (0, j))]
    args = [x, mod_all, tab, gain.reshape(1, D), w, gain_cols]
    if swiglu:
        in_specs.append(pl.BlockSpec((D, tn), lambda i, j: (0, n_tiles + j)))
        args.append(w)
    res = pl.pallas_call(
        kern,
        grid=(T // tm, n_tiles),
        in_specs=in_specs,
        out_specs=out_specs,
        out_shape=out_shape,
        scratch_shapes=[pltpu.VMEM((tm, D), BF16)],
        compiler_params=_cparams(("arbitrary", "arbitrary")),
        name="norm_matmul_" + epilogue,
    )(*args)
    return res if n_aux else res[0]


def _matmul_res_kernel(a_ref, w_ref, x_ref, mod_ref, tab_ref, o_ref, *, mod_row, res_scale):
    h = jnp.dot(a_ref[...], w_ref[...], preferred_element_type=F32)
    gate = mod_ref[0, mod_row:mod_row + 1, :] + tab_ref[mod_row:mod_row + 1, :]
    if res_scale != 1.0:
        gate = res_scale * gate
    o_ref[...] = x_ref[...] + gate * h


def _matmul_res(a, w, x, mod_all, tab, *, mod_row, seq, res_scale=1.0, tm=512, tn=512):
    T, K = a.shape
    D = w.shape[1]
    tm = min(tm, seq)
    tn = min(tn, D)
    bpb = seq // tm
    kern = functools.partial(_matmul_res_kernel, mod_row=mod_row, res_scale=res_scale)
    return pl.pallas_call(
        kern,
        grid=(T // tm, D // tn),
        in_specs=[pl.BlockSpec((tm, K), lambda i, j: (i, 0)),
                  pl.BlockSpec((K, tn), lambda i, j: (0, j)),
                  pl.BlockSpec((tm, tn), lambda i, j: (i, j)),
                  pl.BlockSpec((1, N_MOD, tn), lambda i, j: (i // bpb, 0, j)),
                  pl.BlockSpec((N_MOD, tn), lambda i, j: (0, j))],
        out_specs=pl.BlockSpec((tm, tn), lambda i, j: (i, j)),
        out_shape=jax.ShapeDtypeStruct((T, D), F32),
        compiler_params=_cparams(("arbitrary", "arbitrary")),
        name="matmul_residual",
    )(a, w, x, mod_all, tab)


def _ffn_tile(d_ff):
    return 512 if d_ff >= 512 else _round_up(d_ff, LANES)


def _prep_ffn_weights(w_in, w_out):
    D, two_f = w_in.shape
    F = two_f // 2
    tf = _ffn_tile(F)
    Fp = _round_up(F, tf)
    g = jnp.pad(w_in[:, :F], ((0, 0), (0, Fp - F)))
    u = jnp.pad(w_in[:, F:], ((0, 0), (0, Fp - F)))
    w1 = jnp.concatenate([g, u], axis=1).astype(BF16)
    w2 = jnp.pad(w_out, ((0, Fp - F), (0, 0))).astype(BF16)
    return w1, w2, tf


def _ffn(x, mod_all, tab, gain, w_in, w_out, *, mod_row, seq):
    w1, w2, tf = _prep_ffn_weights(w_in, w_out)
    a = _norm_matmul(x, mod_all, tab, gain, w1, mod_row=mod_row, seq=seq, epilogue="swiglu", tn=tf)
    return _matmul_res(a, w2, x, mod_all, tab, mod_row=mod_row + 2, seq=seq, res_scale=FFN_RES)


def _t5_bucket(dist):
    n = jnp.maximum(dist, 0)
    max_exact = NUM_BUCKETS // 2
    nf = jnp.maximum(n, 1).astype(F32)
    large = max_exact + (jnp.log(nf / max_exact) / math.log(MAX_DISTANCE / max_exact)
                         * (NUM_BUCKETS - max_exact)).astype(I32)
    large = jnp.minimum(large, NUM_BUCKETS - 1)
    return jnp.where(n < max_exact, n, large)


def _bias_strip_kernel(rb_ref, bkt_ref, o_ref):
    h = pl.program_id(0)
    bk = bkt_ref[...]
    far = rb_ref[NUM_BUCKETS - 1, h]
    acc = jnp.zeros(bk.shape, F32)
    for k in range(NUM_BUCKETS - 1):
        acc = jnp.where(bk == k, LOG2E * (rb_ref[k, h] - far), acc)
    o_ref[...] = acc


def _bias_strip(rel_bias, bkt):
    rows = bkt.shape[0]
    H = rel_bias.shape[1]
    return pl.pallas_call(
        _bias_strip_kernel,
        grid=(H,),
        in_specs=[pl.BlockSpec(memory_space=pltpu.SMEM),
                  pl.BlockSpec((rows, TQ), lambda h: (0, 0))],
        out_specs=pl.BlockSpec((rows, TQ), lambda h: (0, h)),
        out_shape=jax.ShapeDtypeStruct((rows, H * TQ), F32),
        compiler_params=_cparams(("arbitrary",)),
        name="bias_strip",
    )(rel_bias, bkt)


def _bias_strips(rel_bias):
    iq = jnp.arange(TQ, dtype=I32)[None, :]
    x = jnp.arange(-STRIP_PAD, NEAR_TILES * TQ + STRIP_PAD, dtype=I32)[:, None]
    bkt_tok = _t5_bucket(iq - x + (NEAR_TILES - 1) * TQ)
    y = jnp.arange(NEAR_CMP, dtype=I32)[:, None]
    bkt_cmp = _t5_bucket(iq + CMP_STRIDE * (NEAR_CMP - 1 - y) - CMP_LAST)
    return _bias_strip(rel_bias, bkt_tok), _bias_strip(rel_bias, bkt_cmp)


def _stack_heads(q_ref, qs_scr, n):
    for r in range(n):
        qs_scr[r * TQ:(r + 1) * TQ, :] = q_ref[:, r * LANES:(r + 1) * LANES]


def _num_far_tiles(i):
    return jnp.maximum(i - NEAR_TILES + 1, 0) // KPT


def _strip_row(i, kt):
    return pl.multiple_of((KPT + NEAR_TILES - 2 - (i - kt * KPT)) * TQ, TQ)


def _flash_init(m_scr, l_scr, acc_scr):
    m_scr[...] = jnp.full(m_scr.shape, NEG, F32)
    l_scr[...] = jnp.zeros(l_scr.shape, F32)
    acc_scr[...] = jnp.zeros(acc_scr.shape, F32)


def _flash_tile(k_tile, v_tile, qs_scr, madd, strip_ref, strip_row, m_scr, l_scr, acc_scr, n_heads):
    cw = HPC * TQ
    mtile = jnp.tile(madd, (1, HPC))
    for c0 in range(0, n_heads * TQ, cw):
        s = lax.dot_general(k_tile, qs_scr[c0:c0 + cw, :], NT_DIMS, preferred_element_type=F32) + mtile
        if strip_ref is not None:
            s = s + strip_ref[pl.ds(strip_row, TK), c0:c0 + cw]
        m_prev = m_scr[:, c0:c0 + cw]
        m_new = jnp.maximum(m_prev, jnp.max(s, axis=0, keepdims=True))
        alpha = jnp.exp2(m_prev - m_new)
        p = jnp.exp2(s - m_new)
        l_scr[:, c0:c0 + cw] = alpha * l_scr[:, c0:c0 + cw] + jnp.sum(p, axis=0, keepdims=True)
        pv = lax.dot_general(v_tile, p.astype(BF16), TN_DIMS, preferred_element_type=F32)
        acc_scr[:, c0:c0 + cw] = alpha * acc_scr[:, c0:c0 + cw] + pv
        m_scr[:, c0:c0 + cw] = m_new


def _flash_result(m_scr, l_scr, acc_scr):
    inv = jnp.where(m_scr[...] > 0.5 * NEG, 1.0 / l_scr[...], 0.0)
    return acc_scr[...] * inv


def _write_heads(o_ref, o_t, n):
    for r in range(n):
        o_ref[:, r * LANES:(r + 1) * LANES] = o_t[:, r * TQ:(r + 1) * TQ].T.astype(o_ref.dtype)


def _dsa_index_kernel(qi_ref, ki_ref, wi_ref, mask_ref, keys_scr, qis_scr, j_scr, *, n_keep, ih, seq):
    i = pl.program_id(1)
    nkt = i + 1
    _stack_heads(qi_ref, qis_scr, ih)
    w_t = wi_ref[...].T * (ih ** -0.5 * IDX_DIM ** -0.5)
    w_rows = [w_t[h:h + 1, :] for h in range(ih)]
    q_pos = i * TQ + lax.broadcasted_iota(I32, (TQ, TQ), 1)
    k_iota = lax.broadcasted_iota(I32, (TQ, TQ), 0)

    def score_tile(kt, carry):
        k0 = pl.multiple_of(kt * TQ, TQ)
        d = lax.dot_general(ki_ref[pl.ds(k0, TQ), :], qis_scr[...], NT_DIMS, preferred_element_type=F32)
        sc = jnp.zeros((TQ, TQ), F32)
        for h in range(ih):
            sc = sc + jnp.maximum(d[:, h * TQ:(h + 1) * TQ], 0.0) * w_rows[h]
        bits = pltpu.bitcast(sc, I32)
        key = jnp.where(bits < 0, bits ^ 0x7FFFFFFF, bits)
        keys_scr[pl.ds(k0, TQ), :] = jnp.where(k0 + k_iota <= q_pos, key, INT_MIN)
        return carry

    lax.fori_loop(0, nkt, score_tile, 0)

    def count(pred):
        def body(kt, acc):
            k0 = pl.multiple_of(kt * TQ, TQ)
            return acc + jnp.where(pred(keys_scr[pl.ds(k0, TQ), :], k0 + k_iota), 1.0, 0.0)
        acc = lax.fori_loop(0, nkt, body, jnp.zeros((TQ, TQ), F32))
        return jnp.sum(acc, axis=0, keepdims=True)

    def bit_body(b, lo):
        cand = lo + jnp.left_shift(jnp.int32(1), 31 - b)
        cnt = count(lambda kk, idx: kk >= cand)
        return jnp.where(cnt >= n_keep, cand, lo)

    thr = lax.fori_loop(0, 32, bit_body, jnp.full((1, TQ), INT_MIN, I32))
    cnt_ge = count(lambda kk, idx: kk >= thr)
    has_tie = (cnt_ge > n_keep) & (thr > INT_MIN)
    j_scr[...] = jnp.full((1, TQ), seq, I32)

    @pl.when(jnp.max(jnp.where(has_tie, 1.0, 0.0)) > 0.5)
    def _():
        need = n_keep - count(lambda kk, idx: kk > thr)

        def jb(b, lo):
            cand = lo + jnp.left_shift(jnp.int32(1), (seq.bit_length() - 2) - b)
            c = count(lambda kk, idx: (kk == thr) & (idx < cand))
            return jnp.where(c < need, cand, lo)

        j_last = lax.fori_loop(0, seq.bit_length() - 1, jb, jnp.zeros((1, TQ), I32))
        j_scr[...] = jnp.where(has_tie, j_last, seq)

    thr_c = jnp.maximum(thr, INT_MIN + 1)
    j_last = j_scr[...]

    def write_tile(kt, carry):
        k0 = pl.multiple_of(kt * TQ, TQ)
        kk = keys_scr[pl.ds(k0, TQ), :]
        sel = (kk > thr_c) | ((kk == thr_c) & (k0 + k_iota <= j_last))
        mask_ref[pl.ds(k0, TQ), :] = jnp.where(sel, 0.0, NEG).astype(BF16)
        return carry

    lax.fori_loop(0, nkt, write_tile, 0)

    def fill_tile(kt, carry):
        k0 = pl.multiple_of(kt * TQ, TQ)
        mask_ref[pl.ds(k0, TQ), :] = jnp.full((TQ, TQ), NEG, BF16)
        return carry

    lax.fori_loop(nkt, seq // TQ, fill_tile, 0)


def _dsa_index(proj, aux, *, batch, seq, qi_blk, ki_blk, n_keep):
    nq = seq // TQ
    ihw = IDX_HEADS * IDX_DIM
    kern = functools.partial(_dsa_index_kernel, n_keep=n_keep, ih=IDX_HEADS, seq=seq)
    return pl.pallas_call(
        kern,
        grid=(batch, nq),
        in_specs=[pl.BlockSpec((TQ, ihw), lambda b, i: (b * nq + i, qi_blk)),
                  pl.BlockSpec((seq, IDX_DIM), lambda b, i: (b, ki_blk)),
                  pl.BlockSpec((TQ, LANES), lambda b, i: (b * nq + i, 1))],
        out_specs=pl.BlockSpec((None, seq, TQ), lambda b, i: (b, 0, i)),
        out_shape=jax.ShapeDtypeStruct((batch, seq, seq), BF16),
        scratch_shapes=[pltpu.VMEM((seq, TQ), I32),
                        pltpu.VMEM((IDX_HEADS * TQ, IDX_DIM), BF16),
                        pltpu.VMEM((1, TQ), I32)],
        compiler_params=_cparams(("arbitrary", "arbitrary")),
        name="dsa_index",
    )(proj, proj, aux)


def _dsa_attn_kernel(q_ref, k_ref, v_ref, mask_ref, strip_ref, o_ref, qs_scr, m_scr, l_scr, acc_scr, *, r_heads):
    i = pl.program_id(2)
    _stack_heads(q_ref, qs_scr, r_heads)
    _flash_init(m_scr, l_scr, acc_scr)

    def tile(kt, near):
        k0 = pl.multiple_of(kt * TK, TK)
        _flash_tile(k_ref[pl.ds(k0, TK), :], v_ref[pl.ds(k0, TK), :], qs_scr, mask_ref[pl.ds(k0, TK), :].astype(F32),
                    strip_ref if near else None, _strip_row(i, kt), m_scr, l_scr, acc_scr, r_heads)

    def far_body(kt, carry):
        tile(kt, False)
        return carry

    def near_body(kt, carry):
        tile(kt, True)
        return carry

    n_far = _num_far_tiles(i)
    lax.fori_loop(0, n_far, far_body, 0)
    lax.fori_loop(n_far, i // KPT + 1, near_body, 0)
    _write_heads(o_ref, _flash_result(m_scr, l_scr, acc_scr), r_heads)


def _dsa_attn(proj, mask, strip, *, batch, seq, k_blk, v_blk):
    nq = seq // TQ
    G = N_KV_HEADS
    R = N_HEADS // G
    W = R * TQ
    kern = functools.partial(_dsa_attn_kernel, r_heads=R)
    return pl.pallas_call(
        kern,
        grid=(batch, G, nq),
        in_specs=[pl.BlockSpec((TQ, R * HEAD_DIM), lambda b, g, i: (b * nq + i, g)),
                  pl.BlockSpec((seq, HEAD_DIM), lambda b, g, i: (b, k_blk + g)),
                  pl.BlockSpec((seq, HEAD_DIM), lambda b, g, i: (b, v_blk + g)),
                  pl.BlockSpec((None, seq, TQ), lambda b, g, i: (b, 0, i)),
                  pl.BlockSpec((STRIP_ROWS, W), lambda b, g, i: (0, g))],
        out_specs=pl.BlockSpec((TQ, R * HEAD_DIM), lambda b, g, i: (b * nq + i, g)),
        out_shape=jax.ShapeDtypeStruct((batch * seq, N_HEADS * HEAD_DIM), BF16),
        scratch_shapes=[pltpu.VMEM((W, HEAD_DIM), BF16),
                        pltpu.VMEM((1, W), F32), pltpu.VMEM((1, W), F32), pltpu.VMEM((HEAD_DIM, W), F32)],
        compiler_params=_cparams(("arbitrary", "arbitrary", "arbitrary")),
        name="dsa_attn",
    )(proj, proj, proj, mask, strip)


def _proj_layout(w_in, segments, tn):
    cols = []
    for start, width in segments:
        pad = _round_up(width, tn) - width
        cols.append(jnp.pad(w_in[:, start:start + width], ((0, 0), (0, pad))))
    return jnp.concatenate(cols, axis=1).astype(BF16)


def _dsa_mixer(xf, mod_all, tab, gain, w_in, w_out, q_gain, k_gain, strip, *, batch, seq):
    AW, KW = N_HEADS * HEAD_DIM, N_KV_HEADS * HEAD_DIM
    IW = IDX_HEADS * IDX_DIM
    tn = KW
    wp = _proj_layout(w_in, [(0, AW), (AW + 2 * KW, IW), (AW, KW), (AW + KW, KW), (AW + 2 * KW + IW, IDX_DIM + IDX_HEADS)], tn)
    nq_t, ni_t = AW // tn, IW // tn
    gain_cols = jnp.concatenate([jnp.tile(q_gain * (HEAD_DIM ** -0.5 * LOG2E), N_HEADS), jnp.zeros((IW,), F32),
                                 jnp.tile(k_gain, N_KV_HEADS), jnp.zeros((2 * tn,), F32)]).reshape(1, -1)
    proj, aux = _norm_matmul(xf, mod_all, tab, gain, wp, mod_row=3, seq=seq, tn=tn, gain_cols=gain_cols,
                             norm_tiles=tuple(range(nq_t)) + (nq_t + ni_t,), n_aux=1)
    k_blk = (nq_t + ni_t) * tn // HEAD_DIM
    mask = _dsa_index(proj, aux, batch=batch, seq=seq, qi_blk=AW // IW, ki_blk=(nq_t + ni_t + 2) * tn // IDX_DIM,
                      n_keep=min(TOPK_KEYS, seq // 4))
    o = _dsa_attn(proj, mask, strip, batch=batch, seq=seq, k_blk=k_blk, v_blk=k_blk + tn // HEAD_DIM)
    return _matmul_res(o, w_out.astype(BF16), xf, mod_all, tab, mod_row=5, seq=seq)


def _nsa_compress_kernel(xk_ref, xkn_ref, xv_ref, xvn_ref, pos_ref, w1_ref, w2_ref, gain_ref, ok_ref, ov_ref, *, groups, kw):
    half = CMP_BLOCK // 2
    rows = xk_ref.shape[0]
    row_id = lax.broadcasted_iota(I32, (rows, HEAD_DIM), 0)

    def branch(br, x_ref, xn_ref, o_ref):
        for g in range(groups):
            p1 = jnp.zeros((rows, HEAD_DIM), F32)
            p2 = jnp.zeros((rows, HEAD_DIM), F32)
            p2n = jnp.zeros((xn_ref.shape[0], HEAD_DIM), F32)
            for r in range(half):
                c0 = r * kw + g * HEAD_DIM
                xa = x_ref[:, c0:c0 + HEAD_DIM]
                w_lo = w1_ref[br, r * HEAD_DIM:(r + 1) * HEAD_DIM, :]
                w_hi = w1_ref[br, (half + r) * HEAD_DIM:(half + r + 1) * HEAD_DIM, :]
                p1 = p1 + jnp.dot((xa + pos_ref[br, r:r + 1, :]).astype(BF16), w_lo, preferred_element_type=F32)
                p2 = p2 + jnp.dot((xa + pos_ref[br, half + r:half + r + 1, :]).astype(BF16), w_hi,
                                  preferred_element_type=F32)
                xb = xn_ref[:, c0:c0 + HEAD_DIM]
                p2n = p2n + jnp.dot((xb + pos_ref[br, half + r:half + r + 1, :]).astype(BF16), w_hi,
                                    preferred_element_type=F32)
            p2s = jnp.where(row_id == rows - 1, p2n[0:1, :], pltpu.roll(p2, rows - 1, 0))
            hid = _gelu_tanh(p1 + p2s)
            out = jnp.dot(hid.astype(BF16), w2_ref[br], preferred_element_type=F32)
            if br == 0:
                ms = jnp.mean(out * out, axis=-1, keepdims=True)
                out = out * lax.rsqrt(ms + EPS) * gain_ref[...]
            o_ref[:, g * HEAD_DIM:(g + 1) * HEAD_DIM] = out.astype(o_ref.dtype)

    branch(0, xk_ref, xkn_ref, ok_ref)
    branch(1, xv_ref, xvn_ref, ov_ref)


def _nsa_compress(kc, vc, cmp_pos, cmp_w1, cmp_w2, k_gain0, *, batch, seq):
    assert CMP_BLOCK == 2 * CMP_STRIDE
    KW = N_KV_HEADS * HEAD_DIM
    ncp = seq // CMP_STRIDE
    rb = min(TQ, ncp)
    nrb = ncp // rb
    nxt = 16
    wide = CMP_STRIDE * KW
    xk = kc.reshape(batch * ncp, wide)
    xv = vc.reshape(batch * ncp, wide)
    last_nxt = batch * ncp // nxt - 1
    main = pl.BlockSpec((rb, wide), lambda b, r: (b * nrb + r, 0))
    ahead = pl.BlockSpec((nxt, wide), lambda b, r: (jnp.minimum((b * nrb + r + 1) * (rb // nxt), last_nxt), 0))
    kern = functools.partial(_nsa_compress_kernel, groups=N_KV_HEADS, kw=KW)
    return pl.pallas_call(
        kern,
        grid=(batch, nrb),
        in_specs=[main, ahead, main, ahead,
                  pl.BlockSpec((2, CMP_BLOCK, HEAD_DIM), lambda b, r: (0, 0, 0)),
                  pl.BlockSpec((2, CMP_BLOCK * HEAD_DIM, HEAD_DIM), lambda b, r: (0, 0, 0)),
                  pl.BlockSpec((2, HEAD_DIM, HEAD_DIM), lambda b, r: (0, 0, 0)),
                  pl.BlockSpec((1, HEAD_DIM), lambda b, r: (0, 0))],
        out_specs=[pl.BlockSpec((rb, KW), lambda b, r: (b * nrb + r, 0))] * 2,
        out_shape=[jax.ShapeDtypeStruct((batch * ncp, KW), BF16)] * 2,
        compiler_params=_cparams(("arbitrary", "arbitrary")),
        name="nsa_compress",
    )(xk, xk, xv, xv, cmp_pos, cmp_w1.astype(BF16), cmp_w2.astype(BF16), k_gain0.reshape(1, HEAD_DIM))


def _nsa_attn_kernel(q_ref, kc_ref, vc_ref, ks_ref, vs_ref, kw_ref, vw_ref, g_ref, gb_ref, strip_ref, ustrip_ref,
                     cover_ref, o_ref, qs_scr, lc_scr, sel_scr, gt_scr, m_scr, l_scr, acc_scr, ocmp_scr, osel_scr,
                     *, r_heads, n_pick):
    g = pl.program_id(1)
    i = pl.program_id(2)
    R = r_heads
    ncp = kc_ref.shape[0]
    n_sel = cover_ref.shape[0]
    pad = NEAR_CMP - CPQ
    _stack_heads(q_ref, qs_scr, R)
    dk_iota = lax.broadcasted_iota(I32, (TK, TQ), 0) - lax.broadcasted_iota(I32, (TK, TQ), 1)

    lc_scr[0:pad, :] = jnp.zeros((pad, R * TQ), F32)
    lc_scr[pad:pad + ncp, :] = lax.dot_general(kc_ref[...], qs_scr[...], NT_DIMS, preferred_element_type=F32)
    off = pl.multiple_of(i * CPQ, SUBLANES)
    lc_scr[pl.ds(off, NEAR_CMP), :] += ustrip_ref[...]
    c_iota = lax.broadcasted_iota(I32, (ncp, TQ), 0)
    t_pos = i * TQ + lax.broadcasted_iota(I32, (ncp, TQ), 1)
    vis = jnp.where(c_iota * CMP_STRIDE + (CMP_BLOCK - 1) <= t_pos, 0.0, NEG)
    s = lc_scr[pad:pad + ncp, :] + jnp.tile(vis, (1, R))
    m = jnp.max(s, axis=0, keepdims=True)
    p = jnp.exp2(s - m)
    inv = jnp.where(m > 0.5 * NEG, 1.0 / jnp.sum(p, axis=0, keepdims=True), 0.0)
    pc = p * inv
    ocmp_scr[...] = lax.dot_general(vc_ref[...], pc.astype(BF16), TN_DIMS, preferred_element_type=F32)

    psum = pc[:, 0:TQ]
    for r in range(1, R):
        psum = psum + pc[:, r * TQ:(r + 1) * TQ]
    hi = psum.astype(BF16)
    lo = (psum - hi.astype(F32)).astype(BF16)
    imp = (jnp.dot(cover_ref[...], hi, preferred_element_type=F32)
           + jnp.dot(cover_ref[...], lo, preferred_element_type=F32))
    n_io = lax.broadcasted_iota(I32, (n_sel, TQ), 0)
    t_sel = i * TQ + lax.broadcasted_iota(I32, (n_sel, TQ), 1)
    cur = jnp.right_shift(t_sel, SEL_BLOCK.bit_length() - 1)
    forced = (n_io == 0) | (n_io == cur) | (n_io == cur - 1)
    val0 = jnp.where(forced, BIG, jnp.where(n_io * SEL_BLOCK <= t_sel, imp, NEG))
    n_f = n_io.astype(F32)

    def pick_round(_, carry):
        val, selm = carry
        cm = jnp.max(val, axis=0, keepdims=True)
        idx = jnp.min(jnp.where(val == cm, n_f, 1e9), axis=0, keepdims=True)
        pick = n_f == idx
        return jnp.where(pick, LOWEST, val), jnp.where(pick, 0.0, selm)

    _, selm = lax.fori_loop(0, n_pick, pick_round, (val0, jnp.full((n_sel, TQ), NEG, F32)))
    sel_scr[...] = selm

    bpt = TK // SEL_BLOCK
    _flash_init(m_scr, l_scr, acc_scr)

    def sel_tile(kt, near):
        k0 = pl.multiple_of(kt * TK, TK)
        bm = jnp.concatenate([jnp.broadcast_to(sel_scr[pl.ds(kt * bpt + b, 1), :], (SEL_BLOCK, TQ))
                              for b in range(bpt)], axis=0)
        if near:
            bm = jnp.where(dk_iota <= (i - kt * KPT) * TQ, bm, NEG)
        _flash_tile(ks_ref[pl.ds(k0, TK), :], vs_ref[pl.ds(k0, TK), :], qs_scr, bm,
                    strip_ref if near else None, _strip_row(i, kt), m_scr, l_scr, acc_scr, R)

    def far_body(kt, carry):
        sel_tile(kt, False)
        return carry

    def near_body(kt, carry):
        sel_tile(kt, True)
        return carry

    n_far = _num_far_tiles(i)
    kt_last = i // KPT
    lax.fori_loop(0, n_far, far_body, 0)
    lax.fori_loop(n_far, kt_last + 1, near_body, 0)
    osel_scr[...] = _flash_result(m_scr, l_scr, acc_scr)

    _flash_init(m_scr, l_scr, acc_scr)

    def win_body(kt, carry):
        k0 = pl.multiple_of(kt * TK, TK)
        dist = (i - kt * KPT) * TQ - dk_iota
        vis_w = jnp.where((dist >= 0) & (dist < WINDOW), 0.0, NEG)
        _flash_tile(kw_ref[pl.ds(k0, TK), :], vw_ref[pl.ds(k0, TK), :], qs_scr, vis_w,
                    strip_ref, _strip_row(i, kt), m_scr, l_scr, acc_scr, R)
        return carry

    lax.fori_loop(jnp.maximum(kt_last - 1, 0), kt_last + 1, win_body, 0)
    o_win = _flash_result(m_scr, l_scr, acc_scr)

    gt_scr[...] = jax.nn.sigmoid(g_ref[:, 0:LANES] + gb_ref[...]).T

    def gate_row(br):
        rows = gt_scr[pl.ds(pl.multiple_of(br * N_HEADS + g * R, SUBLANES), R), :]
        return jnp.concatenate([rows[r:r + 1, :] for r in range(R)], axis=1)

    o_t = gate_row(0) * ocmp_scr[...] + gate_row(1) * osel_scr[...] + gate_row(2) * o_win
    _write_heads(o_ref, o_t, R)


def _nsa_attn(proj, aux_g, gate_b, k_cmp, v_cmp, strip, ustrip, *, batch, seq, blks):
    nq = seq // TQ
    G = N_KV_HEADS
    R = N_HEADS // G
    assert R == SUBLANES and 3 * N_HEADS <= LANES and WINDOW <= TK and WINDOW <= (NEAR_TILES - 1) * TQ
    W = R * TQ
    ncp = seq // CMP_STRIDE
    n_cmp = (seq - CMP_BLOCK) // CMP_STRIDE + 1
    n_sel = seq // SEL_BLOCK
    cs = np.arange(ncp)[None, :] * CMP_STRIDE
    ss = np.arange(n_sel)[:, None] * SEL_BLOCK
    cover = ((cs < ss + SEL_BLOCK) & (cs + CMP_BLOCK - 1 >= ss) & (np.arange(ncp)[None, :] < n_cmp))
    cover = jnp.asarray(cover.astype(np.float32), dtype=BF16)
    gb = jnp.zeros((1, LANES), F32).at[0, :3 * N_HEADS].set(gate_b)
    kv = lambda blk: pl.BlockSpec((seq, HEAD_DIM), lambda b, g, i: (b, blk + g))
    cmp_spec = pl.BlockSpec((ncp, HEAD_DIM), lambda b, g, i: (b, g))
    kern = functools.partial(_nsa_attn_kernel, r_heads=R, n_pick=min(SEL_TOPN, n_sel))
    return pl.pallas_call(
        kern,
        grid=(batch, G, nq),
        in_specs=[pl.BlockSpec((TQ, W), lambda b, g, i: (b * nq + i, g)),
                  cmp_spec, cmp_spec, kv(blks["ks"]), kv(blks["vs"]), kv(blks["kw"]), kv(blks["vw"]),
                  pl.BlockSpec((TQ, aux_g.shape[1]), lambda b, g, i: (b * nq + i, 0)),
                  pl.BlockSpec((1, LANES), lambda b, g, i: (0, 0)),
                  pl.BlockSpec((STRIP_ROWS, W), lambda b, g, i: (0, g)),
                  pl.BlockSpec((NEAR_CMP, W), lambda b, g, i: (0, g)),
                  pl.BlockSpec((n_sel, ncp), lambda b, g, i: (0, 0))],
        out_specs=pl.BlockSpec((TQ, W), lambda b, g, i: (b * nq + i, g)),
        out_shape=jax.ShapeDtypeStruct((batch * seq, N_HEADS * HEAD_DIM), BF16),
        scratch_shapes=[pltpu.VMEM((W, HEAD_DIM), BF16),
                        pltpu.VMEM((NEAR_CMP - CPQ + ncp, W), F32),
                        pltpu.VMEM((n_sel, TQ), F32),
                        pltpu.VMEM((LANES, TQ), F32),
                        pltpu.VMEM((1, W), F32), pltpu.VMEM((1, W), F32), pltpu.VMEM((HEAD_DIM, W), F32),
                        pltpu.VMEM((HEAD_DIM, W), F32), pltpu.VMEM((HEAD_DIM, W), F32)],
        compiler_params=_cparams(("arbitrary", "arbitrary", "arbitrary")),
        name="nsa_attn",
    )(proj, k_cmp, v_cmp, proj, proj, proj, proj, aux_g, gb, strip, ustrip, cover)


def _nsa_mixer(xf, mod_all, tab, gain, w_in, gate_b, w_out, q_gain, k_gain, cmp_pos, cmp_w1, cmp_w2, strip, ustrip,
               *, batch, seq):
    AW, KW = N_HEADS * HEAD_DIM, N_KV_HEADS * HEAD_DIM
    tn = KW
    wp = _proj_layout(w_in, [(0, AW), (AW + 2 * KW, KW), (AW + 3 * KW, KW), (AW + 4 * KW, KW), (AW + 5 * KW, KW),
                             (AW, KW), (AW + KW, KW), (AW + 6 * KW, 3 * N_HEADS)], tn)
    nq_t = AW // tn
    gain_cols = jnp.concatenate([jnp.tile(q_gain * (HEAD_DIM ** -0.5 * LOG2E), N_HEADS), jnp.tile(k_gain[1], N_KV_HEADS),
                                 jnp.zeros((tn,), F32), jnp.tile(k_gain[2], N_KV_HEADS),
                                 jnp.zeros((4 * tn,), F32)]).reshape(1, -1)
    proj, kc, vc, aux_g = _norm_matmul(xf, mod_all, tab, gain, wp, mod_row=3, seq=seq, tn=tn, gain_cols=gain_cols,
                                       norm_tiles=tuple(range(nq_t)) + (nq_t, nq_t + 2), n_aux=3)
    k_cmp, v_cmp = _nsa_compress(kc, vc, cmp_pos, cmp_w1, cmp_w2, k_gain[0], batch=batch, seq=seq)
    per = tn // HEAD_DIM
    blks = {"ks": nq_t * per, "vs": (nq_t + 1) * per, "kw": (nq_t + 2) * per, "vw": (nq_t + 3) * per}
    o = _nsa_attn(proj, aux_g, gate_b, k_cmp, v_cmp, strip, ustrip, batch=batch, seq=seq, blks=blks)
    return _matmul_res(o, w_out.astype(BF16), xf, mod_all, tab, mod_row=5, seq=seq)


def _rglru_kernel(gbr_ref, xr_ref, cw_ref, cb_ref, gw_ref, gbias_ref, lam_ref, o_ref, h_scr, tail_scr, *, ts, bd):
    @pl.when(pl.program_id(2) == 0)
    def _():
        h_scr[...] = jnp.zeros(h_scr.shape, F32)
        tail_scr[...] = jnp.zeros(tail_scr.shape, F32)

    x = xr_ref[...]
    cbw = x.shape[1]
    xfull = jnp.concatenate([tail_scr[...], x], axis=0)
    xc = cw_ref[CONV_WIDTH - 1:CONV_WIDTH, :] * x + cb_ref[...]
    for w in range(1, CONV_WIDTH):
        xc = xc + cw_ref[CONV_WIDTH - 1 - w:CONV_WIDTH - w, :] * pltpu.roll(xfull, w, 0)[SUBLANES:, :]
    tail_scr[...] = x[ts - SUBLANES:ts, :]

    gr = []
    for gi in range(2):
        parts = [jnp.dot(xc[:, n * bd:(n + 1) * bd].astype(BF16), gw_ref[gi, n], preferred_element_type=F32)
                 for n in range(cbw // bd)]
        gr.append(jnp.concatenate(parts, axis=1) + gbias_ref[gi:gi + 1, :])
    r = jax.nn.sigmoid(gr[0])
    i_g = jax.nn.sigmoid(gr[1])
    nl = -lam_ref[...]
    softplus = jnp.maximum(nl, 0.0) + jnp.log1p(jnp.exp(-jnp.abs(nl)))
    log_a = -RG_C * r * softplus
    a = jnp.exp(log_a)
    u = jnp.sqrt(1.0 - jnp.exp(2.0 * log_a)) * (i_g * xc)

    row = lax.broadcasted_iota(I32, (ts, cbw), 0)
    sft = 1
    while sft < ts:
        keep = row >= sft
        a_sh = jnp.where(keep, pltpu.roll(a, sft, 0), 1.0)
        u_sh = jnp.where(keep, pltpu.roll(u, sft, 0), 0.0)
        u = u + a * u_sh
        a = a * a_sh
        sft *= 2
    hs = u + a * h_scr[...]
    h_scr[...] = hs[ts - 1:ts, :]
    o_ref[...] = (_gelu_tanh(gbr_ref[...]) * hs).astype(o_ref.dtype)


def _rglru_mixer(xf, mod_all, tab, gain, w_in, conv_w, conv_b, gate_w, gate_b, lam, w_out, *, batch, seq):
    d_rnn = conv_w.shape[1]
    bd = d_rnn // RG_BLOCKS
    tn = min(512, d_rnn)
    proj = _norm_matmul(xf, mod_all, tab, gain, w_in.astype(BF16), mod_row=3, seq=seq, tn=tn, out_dtype=F32)
    cbw = min(512, d_rnn)
    ncb = d_rnn // cbw
    ts = min(256, seq)
    nts = seq // ts
    kern = functools.partial(_rglru_kernel, ts=ts, bd=bd)
    y = pl.pallas_call(
        kern,
        grid=(batch, ncb, nts),
        in_specs=[pl.BlockSpec((ts, cbw), lambda b, c, t: (b * nts + t, c)),
                  pl.BlockSpec((ts, cbw), lambda b, c, t: (b * nts + t, ncb + c)),
                  pl.BlockSpec((CONV_WIDTH, cbw), lambda b, c, t: (0, c)),
                  pl.BlockSpec((1, cbw), lambda b, c, t: (0, c)),
                  pl.BlockSpec((2, cbw // bd, bd, bd), lambda b, c, t: (0, c, 0, 0)),
                  pl.BlockSpec((2, cbw), lambda b, c, t: (0, c)),
                  pl.BlockSpec((1, cbw), lambda b, c, t: (0, c))],
        out_specs=pl.BlockSpec((ts, cbw), lambda b, c, t: (b * nts + t, c)),
        out_shape=jax.ShapeDtypeStruct((batch * seq, d_rnn), BF16),
        scratch_shapes=[pltpu.VMEM((1, cbw), F32), pltpu.VMEM((SUBLANES, cbw), F32)],
        compiler_params=_cparams(("arbitrary", "arbitrary", "arbitrary")),
        name="rglru",
    )(proj, proj, conv_w, conv_b.reshape(1, d_rnn), gate_w.astype(BF16), gate_b, lam.reshape(1, d_rnn))
    return _matmul_res(y, w_out.astype(BF16), xf, mod_all, tab, mod_row=5, seq=seq)


def kernel(x, c, rel_bias, ada_w, ada_b, ada_table, norm_g, ffn_w_in, ffn_w_out, dsa_w_in, dsa_w_out, dsa_q_gain, dsa_k_gain, nsa_w_in, nsa_gate_b, nsa_w_out, nsa_q_gain, nsa_k_gain, nsa_cmp_pos, nsa_cmp_w1, nsa_cmp_w2, rg_w_in, rg_conv_w, rg_conv_b, rg_gate_w, rg_gate_b, rg_lambda, rg_w_out):
    B, S, D = x.shape
    depth = ada_table.shape[0]
    assert S % TQ == 0 and S >= NEAR_TILES * TQ
    mod_all = _mod_all(c, ada_w, ada_b)
    strip, ustrip = _bias_strips(rel_bias)
    xf = x.reshape(B * S, D)
    for layer in range(depth):
        tab = ada_table[layer]
        xf = _ffn(xf, mod_all, tab, norm_g[layer, 0], ffn_w_in[layer, 0], ffn_w_out[layer, 0], mod_row=0, seq=S)
        kind, j = layer % 3, layer // 3
        if kind == 0:
            xf = _dsa_mixer(xf, mod_all, tab, norm_g[layer, 1], dsa_w_in[j], dsa_w_out[j], dsa_q_gain[j],
                            dsa_k_gain[j], strip, batch=B, seq=S)
        elif kind == 1:
            xf = _nsa_mixer(xf, mod_all, tab, norm_g[layer, 1], nsa_w_in[j], nsa_gate_b[j], nsa_w_out[j],
                            nsa_q_gain[j], nsa_k_gain[j], nsa_cmp_pos[j], nsa_cmp_w1[j], nsa_cmp_w2[j],
                            strip, ustrip, batch=B, seq=S)
        else:
            xf = _rglru_mixer(xf, mod_all, tab, norm_g[layer, 1], rg_w_in[j], rg_conv_w[j], rg_conv_b[j],
                              rg_gate_w[j], rg_gate_b[j], rg_lambda[j], rg_w_out[j], batch=B, seq=S)
        xf = _ffn(xf, mod_all, tab, norm_g[layer, 2], ffn_w_in[layer, 1], ffn_w_out[layer, 1], mod_row=6, seq=S)
    return xf.reshape(B, S, D)
```

```python
import functools
import math

import numpy as np
import jax
import jax.numpy as jnp
from jax import lax
from jax.experimental import pallas as pl
from jax.experimental.pallas import tpu as pltpu

N_HEADS = 32
HEAD_DIM = 128
N_KV_HEADS = 4
FFN_RES = 0.5
N_MOD = 9
NUM_BUCKETS = 32
MAX_DISTANCE = 1024
EPS = 1e-6
NEG = -1e30
BIG = 1e30
IDX_HEADS = 16
IDX_DIM = 128
TOPK_KEYS = 256
CMP_BLOCK = 32
CMP_STRIDE = 16
SEL_BLOCK = 64
SEL_TOPN = 16
WINDOW = 512
RG_BLOCKS = 16
CONV_WIDTH = 4
RG_C = 8.0

LANES = 128
SUBLANES = 8
VMEM_LIMIT_BYTES = 56 * 1024 * 1024

F32 = jnp.float32
BF16 = jnp.bfloat16
I32 = jnp.int32
INT_MIN = -2 ** 31
LOWEST = -3.0e38

TQ = 128
TK = 512
KPT = TK // TQ
SAFE_LOGIT = 60.0
LOG2E = 1.4426950408889634
FAR_DIST = int(math.ceil((NUM_BUCKETS // 2) * (MAX_DISTANCE / (NUM_BUCKETS // 2)) ** ((NUM_BUCKETS // 2 - 1) / (NUM_BUCKETS // 2)))) + 8
NEAR_TILES = -(-(FAR_DIST + TQ - 1) // TQ)
CPQ = TQ // CMP_STRIDE
CMP_LAST = CMP_STRIDE * (CPQ - 1) + CMP_BLOCK - 1
NEAR_CMP = -(-(-(-(FAR_DIST + CMP_LAST) // CMP_STRIDE)) // SUBLANES) * SUBLANES

STRIP_PAD = TK - TQ
STRIP_ROWS = NEAR_TILES * TQ + 2 * STRIP_PAD
NT_DIMS = (((1,), (1,)), ((), ()))
TN_DIMS = (((0,), (0,)), ((), ()))


def _round_up(n, m):
    return (n + m - 1) // m * m


def _cparams(sem):
    return pltpu.CompilerParams(dimension_semantics=sem, vmem_limit_bytes=VMEM_LIMIT_BYTES)


def _gelu_tanh(x):
    return 0.5 * x * (1.0 + jnp.tanh(0.7978845608028654 * (x + 0.044715 * x * x * x)))


def _mod_kernel(c_ref, w_ref, b_ref, o_ref):
    c = c_ref[...]
    s = c * jax.nn.sigmoid(c)
    o_ref[...] = jnp.dot(s.astype(BF16), w_ref[...].astype(BF16), preferred_element_type=F32) + b_ref[...]


def _mod_all(c, ada_w, ada_b):
    B, D = c.shape
    N = ada_w.shape[1]
    rows = 16
    tn = next(t for t in (512, 256, 128) if N % t == 0)
    cp = jnp.zeros((rows, D), F32).at[:B].set(c)
    out = pl.pallas_call(
        _mod_kernel,
        grid=(N // tn,),
        in_specs=[pl.BlockSpec((rows, D), lambda j: (0, 0)),
                  pl.BlockSpec((D, tn), lambda j: (0, j)),
                  pl.BlockSpec((1, tn), lambda j: (0, j))],
        out_specs=pl.BlockSpec((rows, tn), lambda j: (0, j)),
        out_shape=jax.ShapeDtypeStruct((rows, N), F32),
        compiler_params=_cparams(("arbitrary",)),
        name="ada_mod",
    )(cp, ada_w, ada_b.reshape(1, N))
    return out[:B].reshape(B, N_MOD, D)


def _ada_norm_block(x, g, mod, tab, k):
    var = jnp.mean(x * x, axis=-1, keepdims=True)
    y = x * lax.rsqrt(var + EPS) * g
    shift = mod[k:k + 1, :] + tab[k:k + 1, :]
    scale = mod[k + 1:k + 2, :] + tab[k + 1:k + 2, :]
    return y * (1.0 + scale) + shift


def _norm_matmul_kernel(x_ref, mod_ref, tab_ref, g_ref, w_ref, gc_ref, *refs, mod_row, epilogue,
                        norm_tiles, n_aux, n_tiles):
    if epilogue == "swiglu":
        wu_ref, refs = refs[0], refs[1:]
    o_ref = refs[0]
    aux_refs = refs[1:1 + n_aux]
    y_scr = refs[1 + n_aux]
    j = pl.program_id(1)

    @pl.when(j == 0)
    def _():
        y = _ada_norm_block(x_ref[...], g_ref[...], mod_ref[0], tab_ref[...], mod_row)
        y_scr[...] = y.astype(BF16)

    h = jnp.dot(y_scr[...], w_ref[...], preferred_element_type=F32)
    if epilogue == "swiglu":
        u = jnp.dot(y_scr[...], wu_ref[...], preferred_element_type=F32)
        o_ref[...] = (h * jax.nn.sigmoid(h) * u).astype(o_ref.dtype)
        return
    if norm_tiles:
        is_norm = functools.reduce(lambda a, b: a | b, [j == t for t in norm_tiles])

        @pl.when(is_norm)
        def _():
            for c in range(h.shape[1] // LANES):
                hc = h[:, c * LANES:(c + 1) * LANES]
                ms = jnp.mean(hc * hc, axis=-1, keepdims=True)
                yc = hc * lax.rsqrt(ms + EPS) * gc_ref[:, c * LANES:(c + 1) * LANES]
                o_ref[:, c * LANES:(c + 1) * LANES] = yc.astype(o_ref.dtype)

        @pl.when(jnp.logical_not(is_norm))
        def _():
            o_ref[...] = h.astype(o_ref.dtype)
    else:
        o_ref[...] = h.astype(o_ref.dtype)
    for a in range(n_aux):
        @pl.when(j == n_tiles - n_aux + a)
        def _(a=a):
            aux_refs[a][...] = h


def _norm_matmul(x, mod_all, tab, gain, w, *, mod_row, seq, epilogue="plain", tn, tm=512,
                 out_dtype=BF16, gain_cols=None, norm_tiles=(), n_aux=0):
    T, D = x.shape
    swiglu = epilogue == "swiglu"
    n_tiles = w.shape[1] // (2 * tn if swiglu else tn)
    n_out = n_tiles * tn
    tm = min(tm, seq)
    bpb = seq // tm
    if gain_cols is None:
        gain_cols = jnp.zeros((1, n_out), F32)
    out_shape = [jax.ShapeDtypeStruct((T, n_out), out_dtype)]
    out_specs = [pl.BlockSpec((tm, tn), lambda i, j: (i, j))]
    for _ in range(n_aux):
        out_shape.append(jax.ShapeDtypeStruct((T, tn), F32))
        out_specs.append(pl.BlockSpec((tm, tn), lambda i, j: (i, 0)))
    kern = functools.partial(_norm_matmul_kernel, mod_row=mod_row, epilogue=epilogue,
                             norm_tiles=tuple(norm_tiles), n_aux=n_aux, n_tiles=n_tiles)
    in_specs = [pl.BlockSpec((tm, D), lambda i, j: (i, 0)),
                pl.BlockSpec((1, N_MOD, D), lambda i, j: (i // bpb, 0, 0)),
                pl.BlockSpec((N_MOD, D), lambda i, j: (0, 0)),
                pl.BlockSpec((1, D), lambda i, j: (0, 0)),
                pl.BlockSpec((D, tn), lambda i, j: (0, j)),
                pl.BlockSpec((1, tn), lambda i, j: (0, j))]
    args = [x, mod_all, tab, gain.reshape(1, D), w, gain_cols]
    if swiglu:
        in_specs.append(pl.BlockSpec((D, tn), lambda i, j: (0, n_tiles + j)))
        args.append(w)
    res = pl.pallas_call(
        kern,
        grid=(T // tm, n_tiles),
        in_specs=in_specs,
        out_specs=out_specs,
        out_shape=out_shape,
        scratch_shapes=[pltpu.VMEM((tm, D), BF16)],
        compiler_params=_cparams(("arbitrary", "arbitrary")),
        name="norm_matmul_" + epilogue,
    )(*args)
    return res if n_aux else res[0]


def _matmul_res_kernel(a_ref, w_ref, x_ref, mod_ref, tab_ref, o_ref, *, mod_row, res_scale):
    h = jnp.dot(a_ref[...], w_ref[...], preferred_element_type=F32)
    gate = mod_ref[0, mod_row:mod_row + 1, :] + tab_ref[mod_row:mod_row + 1, :]
    if res_scale != 1.0:
        gate = res_scale * gate
    o_ref[...] = x_ref[...] + gate * h


def _matmul_res(a, w, x, mod_all, tab, *, mod_row, seq, res_scale=1.0, tm=512, tn=1024):
    T, K = a.shape
    D = w.shape[1]
    tm = min(tm, seq)
    tn = min(tn, D)
    bpb = seq // tm
    kern = functools.partial(_matmul_res_kernel, mod_row=mod_row, res_scale=res_scale)
    return pl.pallas_call(
        kern,
        grid=(T // tm, D // tn),
        in_specs=[pl.BlockSpec((tm, K), lambda i, j: (i, 0)),
                  pl.BlockSpec((K, tn), lambda i, j: (0, j)),
                  pl.BlockSpec((tm, tn), lambda i, j: (i, j)),
                  pl.BlockSpec((1, N_MOD, tn), lambda i, j: (i // bpb, 0, j)),
                  pl.BlockSpec((N_MOD, tn), lambda i, j: (0, j))],
        out_specs=pl.BlockSpec((tm, tn), lambda i, j: (i, j)),
        out_shape=jax.ShapeDtypeStruct((T, D), F32),
        compiler_params=_cparams(("arbitrary", "arbitrary")),
        name="matmul_residual",
    )(a, w, x, mod_all, tab)


def _ffn_tile(d_ff):
    return 512 if d_ff >= 512 else _round_up(d_ff, LANES)


def _prep_ffn_weights(w_in, w_out):
    D, two_f = w_in.shape
    F = two_f // 2
    tf = _ffn_tile(F)
    Fp = _round_up(F, tf)
    g = jnp.pad(w_in[:, :F], ((0, 0), (0, Fp - F)))
    u = jnp.pad(w_in[:, F:], ((0, 0), (0, Fp - F)))
    w1 = jnp.concatenate([g, u], axis=1).astype(BF16)
    w2 = jnp.pad(w_out, ((0, Fp - F), (0, 0))).astype(BF16)
    return w1, w2, tf


def _ffn(x, mod_all, tab, gain, w_in, w_out, *, mod_row, seq):
    w1, w2, tf = _prep_ffn_weights(w_in, w_out)
    a = _norm_matmul(x, mod_all, tab, gain, w1, mod_row=mod_row, seq=seq, epilogue="swiglu", tn=tf)
    return _matmul_res(a, w2, x, mod_all, tab, mod_row=mod_row + 2, seq=seq, res_scale=FFN_RES)


def _t5_bucket(dist):
    n = jnp.maximum(dist, 0)
    max_exact = NUM_BUCKETS // 2
    nf = jnp.maximum(n, 1).astype(F32)
    large = max_exact + (jnp.log(nf / max_exact) / math.log(MAX_DISTANCE / max_exact)
                         * (NUM_BUCKETS - max_exact)).astype(I32)
    large = jnp.minimum(large, NUM_BUCKETS - 1)
    return jnp.where(n < max_exact, n, large)


def _bias_strip_kernel(rb_ref, bkt_ref, o_ref, amax_ref):
    h = pl.program_id(0)
    bk = bkt_ref[...]
    far = rb_ref[NUM_BUCKETS - 1, h]
    acc = jnp.zeros(bk.shape, F32)
    for k in range(NUM_BUCKETS - 1):
        acc = jnp.where(bk == k, LOG2E * (rb_ref[k, h] - far), acc)
    o_ref[...] = acc
    amax_ref[...] = jnp.max(jnp.abs(acc), axis=0, keepdims=True)


def _bias_strip(rel_bias, bkt):
    rows = bkt.shape[0]
    H = rel_bias.shape[1]
    return pl.pallas_call(
        _bias_strip_kernel,
        grid=(H,),
        in_specs=[pl.BlockSpec(memory_space=pltpu.SMEM),
                  pl.BlockSpec((rows, TQ), lambda h: (0, 0))],
        out_specs=[pl.BlockSpec((rows, TQ), lambda h: (0, h)), pl.BlockSpec((1, TQ), lambda h: (0, h))],
        out_shape=[jax.ShapeDtypeStruct((rows, H * TQ), F32), jax.ShapeDtypeStruct((1, H * TQ), F32)],
        compiler_params=_cparams(("arbitrary",)),
        name="bias_strip",
    )(rel_bias, bkt)


def _bias_strips(rel_bias):
    iq = jnp.arange(TQ, dtype=I32)[None, :]
    x = jnp.arange(-STRIP_PAD, NEAR_TILES * TQ + STRIP_PAD, dtype=I32)[:, None]
    bkt_tok = _t5_bucket(iq - x + (NEAR_TILES - 1) * TQ)
    y = jnp.arange(NEAR_CMP, dtype=I32)[:, None]
    bkt_cmp = _t5_bucket(iq + CMP_STRIDE * (NEAR_CMP - 1 - y) - CMP_LAST)
    strip, strip_amax = _bias_strip(rel_bias, bkt_tok)
    ustrip, _ = _bias_strip(rel_bias, bkt_cmp)
    return strip, strip_amax, ustrip


def _stack_heads(q_ref, qs_scr, n):
    for r in range(n):
        qs_scr[r * TQ:(r + 1) * TQ, :] = q_ref[:, r * LANES:(r + 1) * LANES]


def _num_far_tiles(i):
    return jnp.maximum(i - NEAR_TILES + 1, 0) // KPT


def _strip_row(i, kt):
    return pl.multiple_of((KPT + NEAR_TILES - 2 - (i - kt * KPT)) * TQ, TQ)


def _flash_init(m_scr, l_scr, acc_scr):
    m_scr[...] = jnp.full(m_scr.shape, NEG, F32)
    l_scr[...] = jnp.zeros(l_scr.shape, F32)
    acc_scr[...] = jnp.zeros(acc_scr.shape, F32)


def _max_row_sumsq(k_ref):
    rows = k_ref.shape[0]
    ch = min(rows, 1024)

    def body(c, best):
        kk = k_ref[pl.ds(pl.multiple_of(c * ch, ch), ch), :].astype(F32)
        return jnp.maximum(best, jnp.max(jnp.sum(kk * kk, axis=1, keepdims=True)))

    return lax.fori_loop(0, rows // ch, body, jnp.float32(0.0))


def _logits_bounded(qs_scr, k_sumsq, amax_row):
    qf = qs_scr[...].astype(F32)
    q_sumsq = lax.dot_general(jnp.ones((SUBLANES, HEAD_DIM), BF16), (qf * qf).astype(BF16), NT_DIMS,
                              preferred_element_type=F32)[0:1, :]
    bound = jnp.sqrt(q_sumsq * k_sumsq) * 1.02 + amax_row
    return jnp.max(bound) <= SAFE_LOGIT


def _flash_tile(k_tile, v_tile, qs_scr, madd, strip_ref, strip_row, m_scr, l_scr, acc_scr, n_heads, online):
    s = lax.dot_general(k_tile, qs_scr[...], NT_DIMS, preferred_element_type=F32) + jnp.tile(madd, (1, n_heads))
    if strip_ref is not None:
        s = s + strip_ref[pl.ds(strip_row, TK), :]
    if not online:
        p = jnp.exp2(s)
        l_scr[...] += jnp.sum(p, axis=0, keepdims=True)
        acc_scr[...] += lax.dot_general(v_tile, p.astype(BF16), TN_DIMS, preferred_element_type=F32)
        return
    m_prev = m_scr[...]
    m_new = jnp.maximum(m_prev, jnp.max(s, axis=0, keepdims=True))
    alpha = jnp.exp2(m_prev - m_new)
    p = jnp.exp2(s - m_new)
    l_scr[...] = alpha * l_scr[...] + jnp.sum(p, axis=0, keepdims=True)
    pv = lax.dot_general(v_tile, p.astype(BF16), TN_DIMS, preferred_element_type=F32)
    acc_scr[...] = alpha * acc_scr[...] + pv
    m_scr[...] = m_new


def _flash_sweep(bounded, sweep, m_scr, l_scr, acc_scr):
    _flash_init(m_scr, l_scr, acc_scr)

    @pl.when(bounded)
    def _():
        sweep(False)

    @pl.when(jnp.logical_not(bounded))
    def _():
        sweep(True)
        l_scr[...] = jnp.where(m_scr[...] > 0.5 * NEG, l_scr[...], 0.0)

    l = l_scr[...]
    return acc_scr[...] * jnp.where(l > 0.0, 1.0 / l, 0.0)


def _write_heads(o_ref, o_t, n):
    for r in range(n):
        o_ref[:, r * LANES:(r + 1) * LANES] = o_t[:, r * TQ:(r + 1) * TQ].T.astype(o_ref.dtype)


def _dsa_index_kernel(qi_ref, ki_ref, wi_ref, mask_ref, keys_scr, qis_scr, j_scr, *, n_keep, ih, seq):
    i = pl.program_id(1)
    nkt = i + 1
    _stack_heads(qi_ref, qis_scr, ih)
    w_t = wi_ref[...].T * (ih ** -0.5 * IDX_DIM ** -0.5)
    w_rows = [w_t[h:h + 1, :] for h in range(ih)]
    q_pos = i * TQ + lax.broadcasted_iota(I32, (TQ, TQ), 1)
    k_iota = lax.broadcasted_iota(I32, (TQ, TQ), 0)

    def score_tile(kt, carry):
        k0 = pl.multiple_of(kt * TQ, TQ)
        d = lax.dot_general(ki_ref[pl.ds(k0, TQ), :], qis_scr[...], NT_DIMS, preferred_element_type=F32)
        sc = jnp.zeros((TQ, TQ), F32)
        for h in range(ih):
            sc = sc + jnp.maximum(d[:, h * TQ:(h + 1) * TQ], 0.0) * w_rows[h]
        bits = pltpu.bitcast(sc, I32)
        key = jnp.where(bits < 0, bits ^ 0x7FFFFFFF, bits)
        keys_scr[pl.ds(k0, TQ), :] = jnp.where(k0 + k_iota <= q_pos, key, INT_MIN)
        return carry

    lax.fori_loop(0, nkt, score_tile, 0)

    def count(pred):
        def body(kt, acc):
            k0 = pl.multiple_of(kt * TQ, TQ)
            return acc + jnp.where(pred(keys_scr[pl.ds(k0, TQ), :], k0 + k_iota), 1.0, 0.0)
        acc = lax.fori_loop(0, nkt, body, jnp.zeros((TQ, TQ), F32))
        return jnp.sum(acc, axis=0, keepdims=True)

    def bit_body(b, lo):
        cand = lo + jnp.left_shift(jnp.int32(1), 31 - b)
        cnt = count(lambda kk, idx: kk >= cand)
        return jnp.where(cnt >= n_keep, cand, lo)

    thr = lax.fori_loop(0, 32, bit_body, jnp.full((1, TQ), INT_MIN, I32))
    cnt_ge = count(lambda kk, idx: kk >= thr)
    has_tie = (cnt_ge > n_keep) & (thr > INT_MIN)
    j_scr[...] = jnp.full((1, TQ), seq, I32)

    @pl.when(jnp.max(jnp.where(has_tie, 1.0, 0.0)) > 0.5)
    def _():
        need = n_keep - count(lambda kk, idx: kk > thr)

        def jb(b, lo):
            cand = lo + jnp.left_shift(jnp.int32(1), (seq.bit_length() - 2) - b)
            c = count(lambda kk, idx: (kk == thr) & (idx < cand))
            return jnp.where(c < need, cand, lo)

        j_last = lax.fori_loop(0, seq.bit_length() - 1, jb, jnp.zeros((1, TQ), I32))
        j_scr[...] = jnp.where(has_tie, j_last, seq)

    thr_c = jnp.maximum(thr, INT_MIN + 1)
    j_last = j_scr[...]

    def write_tile(kt, carry):
        k0 = pl.multiple_of(kt * TQ, TQ)
        kk = keys_scr[pl.ds(k0, TQ), :]
        sel = (kk > thr_c) | ((kk == thr_c) & (k0 + k_iota <= j_last))
        mask_ref[pl.ds(k0, TQ), :] = jnp.where(sel, 0.0, NEG).astype(BF16)
        return carry

    lax.fori_loop(0, nkt, write_tile, 0)

    def fill_tile(kt, carry):
        k0 = pl.multiple_of(kt * TQ, TQ)
        mask_ref[pl.ds(k0, TQ), :] = jnp.full((TQ, TQ), NEG, BF16)
        return carry

    lax.fori_loop(nkt, seq // TQ, fill_tile, 0)


def _dsa_index(proj, aux, *, batch, seq, qi_blk, ki_blk, n_keep):
    nq = seq // TQ
    ihw = IDX_HEADS * IDX_DIM
    kern = functools.partial(_dsa_index_kernel, n_keep=n_keep, ih=IDX_HEADS, seq=seq)
    return pl.pallas_call(
        kern,
        grid=(batch, nq),
        in_specs=[pl.BlockSpec((TQ, ihw), lambda b, i: (b * nq + i, qi_blk)),
                  pl.BlockSpec((seq, IDX_DIM), lambda b, i: (b, ki_blk)),
                  pl.BlockSpec((TQ, LANES), lambda b, i: (b * nq + i, 1))],
        out_specs=pl.BlockSpec((None, seq, TQ), lambda b, i: (b, 0, i)),
        out_shape=jax.ShapeDtypeStruct((batch, seq, seq), BF16),
        scratch_shapes=[pltpu.VMEM((seq, TQ), I32),
                        pltpu.VMEM((IDX_HEADS * TQ, IDX_DIM), BF16),
                        pltpu.VMEM((1, TQ), I32)],
        compiler_params=_cparams(("arbitrary", "arbitrary")),
        name="dsa_index",
    )(proj, proj, aux)


def _dsa_attn_kernel(q_ref, k_ref, v_ref, mask_ref, strip_ref, amax_ref, o_ref, qs_scr, m_scr, l_scr, acc_scr,
                     ksq_scr, *, r_heads):
    i = pl.program_id(2)

    @pl.when(i == 0)
    def _():
        ksq_scr[0] = _max_row_sumsq(k_ref)

    _stack_heads(q_ref, qs_scr, r_heads)
    n_far = _num_far_tiles(i)

    def sweep(online):
        def tile(kt, near):
            k0 = pl.multiple_of(kt * TK, TK)
            _flash_tile(k_ref[pl.ds(k0, TK), :], v_ref[pl.ds(k0, TK), :], qs_scr,
                        mask_ref[pl.ds(k0, TK), :].astype(F32), strip_ref if near else None, _strip_row(i, kt),
                        m_scr, l_scr, acc_scr, r_heads, online)

        def far_body(kt, carry):
            tile(kt, False)
            return carry

        def near_body(kt, carry):
            tile(kt, True)
            return carry

        lax.fori_loop(0, n_far, far_body, 0)
        lax.fori_loop(n_far, i // KPT + 1, near_body, 0)

    bounded = _logits_bounded(qs_scr, ksq_scr[0], amax_ref[...])
    _write_heads(o_ref, _flash_sweep(bounded, sweep, m_scr, l_scr, acc_scr), r_heads)


def _dsa_attn(proj, mask, strip, strip_amax, *, batch, seq, k_blk, v_blk):
    nq = seq // TQ
    G = N_KV_HEADS
    R = N_HEADS // G
    W = R * TQ
    kern = functools.partial(_dsa_attn_kernel, r_heads=R)
    return pl.pallas_call(
        kern,
        grid=(batch, G, nq),
        in_specs=[pl.BlockSpec((TQ, R * HEAD_DIM), lambda b, g, i: (b * nq + i, g)),
                  pl.BlockSpec((seq, HEAD_DIM), lambda b, g, i: (b, k_blk + g)),
                  pl.BlockSpec((seq, HEAD_DIM), lambda b, g, i: (b, v_blk + g)),
                  pl.BlockSpec((None, seq, TQ), lambda b, g, i: (b, 0, i)),
                  pl.BlockSpec((STRIP_ROWS, W), lambda b, g, i: (0, g)),
                  pl.BlockSpec((1, W), lambda b, g, i: (0, g))],
        out_specs=pl.BlockSpec((TQ, R * HEAD_DIM), lambda b, g, i: (b * nq + i, g)),
        out_shape=jax.ShapeDtypeStruct((batch * seq, N_HEADS * HEAD_DIM), BF16),
        scratch_shapes=[pltpu.VMEM((W, HEAD_DIM), BF16),
                        pltpu.VMEM((1, W), F32), pltpu.VMEM((1, W), F32), pltpu.VMEM((HEAD_DIM, W), F32),
                        pltpu.SMEM((1,), F32)],
        compiler_params=_cparams(("arbitrary", "arbitrary", "arbitrary")),
        name="dsa_attn",
    )(proj, proj, proj, mask, strip, strip_amax)


def _proj_layout(w_in, segments, tn):
    cols = []
    for start, width in segments:
        pad = _round_up(width, tn) - width
        cols.append(jnp.pad(w_in[:, start:start + width], ((0, 0), (0, pad))))
    return jnp.concatenate(cols, axis=1).astype(BF16)


def _dsa_mixer(xf, mod_all, tab, gain, w_in, w_out, q_gain, k_gain, strip, strip_amax, *, batch, seq):
    AW, KW = N_HEADS * HEAD_DIM, N_KV_HEADS * HEAD_DIM
    IW = IDX_HEADS * IDX_DIM
    tn = KW
    wp = _proj_layout(w_in, [(0, AW), (AW + 2 * KW, IW), (AW, KW), (AW + KW, KW), (AW + 2 * KW + IW, IDX_DIM + IDX_HEADS)], tn)
    nq_t, ni_t = AW // tn, IW // tn
    gain_cols = jnp.concatenate([jnp.tile(q_gain * (HEAD_DIM ** -0.5 * LOG2E), N_HEADS), jnp.zeros((IW,), F32),
                                 jnp.tile(k_gain, N_KV_HEADS), jnp.zeros((2 * tn,), F32)]).reshape(1, -1)
    proj, aux = _norm_matmul(xf, mod_all, tab, gain, wp, mod_row=3, seq=seq, tn=tn, gain_cols=gain_cols,
                             norm_tiles=tuple(range(nq_t)) + (nq_t + ni_t,), n_aux=1)
    k_blk = (nq_t + ni_t) * tn // HEAD_DIM
    mask = _dsa_index(proj, aux, batch=batch, seq=seq, qi_blk=AW // IW, ki_blk=(nq_t + ni_t + 2) * tn // IDX_DIM,
                      n_keep=min(TOPK_KEYS, seq // 4))
    o = _dsa_attn(proj, mask, strip, strip_amax, batch=batch, seq=seq, k_blk=k_blk, v_blk=k_blk + tn // HEAD_DIM)
    return _matmul_res(o, w_out.astype(BF16), xf, mod_all, tab, mod_row=5, seq=seq)


def _nsa_compress_kernel(xk_ref, xkn_ref, xv_ref, xvn_ref, pos_ref, w1_ref, w2_ref, gain_ref, ok_ref, ov_ref, *, groups, kw):
    half = CMP_BLOCK // 2
    rows = xk_ref.shape[0]
    row_id = lax.broadcasted_iota(I32, (rows, HEAD_DIM), 0)

    def branch(br, x_ref, xn_ref, o_ref):
        for g in range(groups):
            p1 = jnp.zeros((rows, HEAD_DIM), F32)
            p2 = jnp.zeros((rows, HEAD_DIM), F32)
            p2n = jnp.zeros((xn_ref.shape[0], HEAD_DIM), F32)
            for r in range(half):
                c0 = r * kw + g * HEAD_DIM
                xa = x_ref[:, c0:c0 + HEAD_DIM]
                w_lo = w1_ref[br, r * HEAD_DIM:(r + 1) * HEAD_DIM, :]
                w_hi = w1_ref[br, (half + r) * HEAD_DIM:(half + r + 1) * HEAD_DIM, :]
                p1 = p1 + jnp.dot((xa + pos_ref[br, r:r + 1, :]).astype(BF16), w_lo, preferred_element_type=F32)
                p2 = p2 + jnp.dot((xa + pos_ref[br, half + r:half + r + 1, :]).astype(BF16), w_hi,
                                  preferred_element_type=F32)
                xb = xn_ref[:, c0:c0 + HEAD_DIM]
                p2n = p2n + jnp.dot((xb + pos_ref[br, half + r:half + r + 1, :]).astype(BF16), w_hi,
                                    preferred_element_type=F32)
            p2s = jnp.where(row_id == rows - 1, p2n[0:1, :], pltpu.roll(p2, rows - 1, 0))
            hid = _gelu_tanh(p1 + p2s)
            out = jnp.dot(hid.astype(BF16), w2_ref[br], preferred_element_type=F32)
            if br == 0:
                ms = jnp.mean(out * out, axis=-1, keepdims=True)
                out = out * lax.rsqrt(ms + EPS) * gain_ref[...]
            o_ref[:, g * HEAD_DIM:(g + 1) * HEAD_DIM] = out.astype(o_ref.dtype)

    branch(0, xk_ref, xkn_ref, ok_ref)
    branch(1, xv_ref, xvn_ref, ov_ref)


def _nsa_compress(kc, vc, cmp_pos, cmp_w1, cmp_w2, k_gain0, *, batch, seq):
    assert CMP_BLOCK == 2 * CMP_STRIDE
    KW = N_KV_HEADS * HEAD_DIM
    ncp = seq // CMP_STRIDE
    rb = min(TQ, ncp)
    nrb = ncp // rb
    nxt = 16
    wide = CMP_STRIDE * KW
    xk = kc.reshape(batch * ncp, wide)
    xv = vc.reshape(batch * ncp, wide)
    last_nxt = batch * ncp // nxt - 1
    main = pl.BlockSpec((rb, wide), lambda b, r: (b * nrb + r, 0))
    ahead = pl.BlockSpec((nxt, wide), lambda b, r: (jnp.minimum((b * nrb + r + 1) * (rb // nxt), last_nxt), 0))
    kern = functools.partial(_nsa_compress_kernel, groups=N_KV_HEADS, kw=KW)
    return pl.pallas_call(
        kern,
        grid=(batch, nrb),
        in_specs=[main, ahead, main, ahead,
                  pl.BlockSpec((2, CMP_BLOCK, HEAD_DIM), lambda b, r: (0, 0, 0)),
                  pl.BlockSpec((2, CMP_BLOCK * HEAD_DIM, HEAD_DIM), lambda b, r: (0, 0, 0)),
                  pl.BlockSpec((2, HEAD_DIM, HEAD_DIM), lambda b, r: (0, 0, 0)),
                  pl.BlockSpec((1, HEAD_DIM), lambda b, r: (0, 0))],
        out_specs=[pl.BlockSpec((rb, KW), lambda b, r: (b * nrb + r, 0))] * 2,
        out_shape=[jax.ShapeDtypeStruct((batch * ncp, KW), BF16)] * 2,
        compiler_params=_cparams(("arbitrary", "arbitrary")),
        name="nsa_compress",
    )(xk, xk, xv, xv, cmp_pos, cmp_w1.astype(BF16), cmp_w2.astype(BF16), k_gain0.reshape(1, HEAD_DIM))


def _nsa_attn_kernel(q_ref, kc_ref, vc_ref, ks_ref, vs_ref, kw_ref, vw_ref, g_ref, gb_ref, strip_ref, amax_ref,
                     ustrip_ref, cover_ref, o_ref, qs_scr, lc_scr, sel_scr, gt_scr, m_scr, l_scr, acc_scr, ocmp_scr,
                     osel_scr, ksq_scr, *, r_heads, n_pick):
    g = pl.program_id(1)
    i = pl.program_id(2)
    R = r_heads

    @pl.when(i == 0)
    def _():
        ksq_scr[0] = _max_row_sumsq(ks_ref)
        ksq_scr[1] = _max_row_sumsq(kw_ref)

    ncp = kc_ref.shape[0]
    n_sel = cover_ref.shape[0]
    pad = NEAR_CMP - CPQ
    _stack_heads(q_ref, qs_scr, R)
    dk_iota = lax.broadcasted_iota(I32, (TK, TQ), 0) - lax.broadcasted_iota(I32, (TK, TQ), 1)

    lc_scr[0:pad, :] = jnp.zeros((pad, R * TQ), F32)
    lc_scr[pad:pad + ncp, :] = lax.dot_general(kc_ref[...], qs_scr[...], NT_DIMS, preferred_element_type=F32)
    off = pl.multiple_of(i * CPQ, SUBLANES)
    lc_scr[pl.ds(off, NEAR_CMP), :] += ustrip_ref[...]
    c_iota = lax.broadcasted_iota(I32, (ncp, TQ), 0)
    t_pos = i * TQ + lax.broadcasted_iota(I32, (ncp, TQ), 1)
    vis = jnp.where(c_iota * CMP_STRIDE + (CMP_BLOCK - 1) <= t_pos, 0.0, NEG)
    s = lc_scr[pad:pad + ncp, :] + jnp.tile(vis, (1, R))
    m = jnp.max(s, axis=0, keepdims=True)
    p = jnp.exp2(s - m)
    inv = jnp.where(m > 0.5 * NEG, 1.0 / jnp.sum(p, axis=0, keepdims=True), 0.0)
    pc = p * inv
    ocmp_scr[...] = lax.dot_general(vc_ref[...], pc.astype(BF16), TN_DIMS, preferred_element_type=F32)

    psum = pc[:, 0:TQ]
    for r in range(1, R):
        psum = psum + pc[:, r * TQ:(r + 1) * TQ]
    hi = psum.astype(BF16)
    lo = (psum - hi.astype(F32)).astype(BF16)
    imp = (jnp.dot(cover_ref[...], hi, preferred_element_type=F32)
           + jnp.dot(cover_ref[...], lo, preferred_element_type=F32))
    n_io = lax.broadcasted_iota(I32, (n_sel, TQ), 0)
    t_sel = i * TQ + lax.broadcasted_iota(I32, (n_sel, TQ), 1)
    cur = jnp.right_shift(t_sel, SEL_BLOCK.bit_length() - 1)
    forced = (n_io == 0) | (n_io == cur) | (n_io == cur - 1)
    val0 = jnp.where(forced, BIG, jnp.where(n_io * SEL_BLOCK <= t_sel, imp, NEG))
    n_f = n_io.astype(F32)

    def pick_round(_, carry):
        val, selm = carry
        cm = jnp.max(val, axis=0, keepdims=True)
        idx = jnp.min(jnp.where(val == cm, n_f, 1e9), axis=0, keepdims=True)
        pick = n_f == idx
        return jnp.where(pick, LOWEST, val), jnp.where(pick, 0.0, selm)

    _, selm = lax.fori_loop(0, n_pick, pick_round, (val0, jnp.full((n_sel, TQ), NEG, F32)))
    sel_scr[...] = selm

    bpt = TK // SEL_BLOCK
    n_far = _num_far_tiles(i)
    kt_last = i // KPT

    def sel_sweep(online):
        def sel_tile(kt, near):
            k0 = pl.multiple_of(kt * TK, TK)
            bm = jnp.concatenate([jnp.broadcast_to(sel_scr[pl.ds(kt * bpt + b, 1), :], (SEL_BLOCK, TQ))
                                  for b in range(bpt)], axis=0)
            if near:
                bm = jnp.where(dk_iota <= (i - kt * KPT) * TQ, bm, NEG)
            _flash_tile(ks_ref[pl.ds(k0, TK), :], vs_ref[pl.ds(k0, TK), :], qs_scr, bm,
                        strip_ref if near else None, _strip_row(i, kt), m_scr, l_scr, acc_scr, R, online)

        def far_body(kt, carry):
            sel_tile(kt, False)
            return carry

        def near_body(kt, carry):
            sel_tile(kt, True)
            return carry

        lax.fori_loop(0, n_far, far_body, 0)
        lax.fori_loop(n_far, kt_last + 1, near_body, 0)

    sel_bounded = _logits_bounded(qs_scr, ksq_scr[0], amax_ref[...])
    osel_scr[...] = _flash_sweep(sel_bounded, sel_sweep, m_scr, l_scr, acc_scr)

    def win_sweep(online):
        def win_body(kt, carry):
            k0 = pl.multiple_of(kt * TK, TK)
            dist = (i - kt * KPT) * TQ - dk_iota
            vis_w = jnp.where((dist >= 0) & (dist < WINDOW), 0.0, NEG)
            _flash_tile(kw_ref[pl.ds(k0, TK), :], vw_ref[pl.ds(k0, TK), :], qs_scr, vis_w,
                        strip_ref, _strip_row(i, kt), m_scr, l_scr, acc_scr, R, online)
            return carry

        lax.fori_loop(jnp.maximum(kt_last - 1, 0), kt_last + 1, win_body, 0)

    win_bounded = _logits_bounded(qs_scr, ksq_scr[1], amax_ref[...])
    o_win = _flash_sweep(win_bounded, win_sweep, m_scr, l_scr, acc_scr)

    gt_scr[...] = jax.nn.sigmoid(g_ref[:, 0:LANES] + gb_ref[...]).T

    def gate_row(br):
        rows = gt_scr[pl.ds(pl.multiple_of(br * N_HEADS + g * R, SUBLANES), R), :]
        return jnp.concatenate([rows[r:r + 1, :] for r in range(R)], axis=1)

    o_t = gate_row(0) * ocmp_scr[...] + gate_row(1) * osel_scr[...] + gate_row(2) * o_win
    _write_heads(o_ref, o_t, R)


def _nsa_attn(proj, aux_g, gate_b, k_cmp, v_cmp, strip, strip_amax, ustrip, *, batch, seq, blks):
    nq = seq // TQ
    G = N_KV_HEADS
    R = N_HEADS // G
    assert R == SUBLANES and 3 * N_HEADS <= LANES and WINDOW <= TK and WINDOW <= (NEAR_TILES - 1) * TQ
    W = R * TQ
    ncp = seq // CMP_STRIDE
    n_cmp = (seq - CMP_BLOCK) // CMP_STRIDE + 1
    n_sel = seq // SEL_BLOCK
    cs = np.arange(ncp)[None, :] * CMP_STRIDE
    ss = np.arange(n_sel)[:, None] * SEL_BLOCK
    cover = ((cs < ss + SEL_BLOCK) & (cs + CMP_BLOCK - 1 >= ss) & (np.arange(ncp)[None, :] < n_cmp))
    cover = jnp.asarray(cover.astype(np.float32), dtype=BF16)
    gb = jnp.zeros((1, LANES), F32).at[0, :3 * N_HEADS].set(gate_b)
    kv = lambda blk: pl.BlockSpec((seq, HEAD_DIM), lambda b, g, i: (b, blk + g))
    cmp_spec = pl.BlockSpec((ncp, HEAD_DIM), lambda b, g, i: (b, g))
    kern = functools.partial(_nsa_attn_kernel, r_heads=R, n_pick=min(SEL_TOPN, n_sel))
    return pl.pallas_call(
        kern,
        grid=(batch, G, nq),
        in_specs=[pl.BlockSpec((TQ, W), lambda b, g, i: (b * nq + i, g)),
                  cmp_spec, cmp_spec, kv(blks["ks"]), kv(blks["vs"]), kv(blks["kw"]), kv(blks["vw"]),
                  pl.BlockSpec((TQ, aux_g.shape[1]), lambda b, g, i: (b * nq + i, 0)),
                  pl.BlockSpec((1, LANES), lambda b, g, i: (0, 0)),
                  pl.BlockSpec((STRIP_ROWS, W), lambda b, g, i: (0, g)),
                  pl.BlockSpec((1, W), lambda b, g, i: (0, g)),
                  pl.BlockSpec((NEAR_CMP, W), lambda b, g, i: (0, g)),
                  pl.BlockSpec((n_sel, ncp), lambda b, g, i: (0, 0))],
        out_specs=pl.BlockSpec((TQ, W), lambda b, g, i: (b * nq + i, g)),
        out_shape=jax.ShapeDtypeStruct((batch * seq, N_HEADS * HEAD_DIM), BF16),
        scratch_shapes=[pltpu.VMEM((W, HEAD_DIM), BF16),
                        pltpu.VMEM((NEAR_CMP - CPQ + ncp, W), F32),
                        pltpu.VMEM((n_sel, TQ), F32),
                        pltpu.VMEM((LANES, TQ), F32),
                        pltpu.VMEM((1, W), F32), pltpu.VMEM((1, W), F32), pltpu.VMEM((HEAD_DIM, W), F32),
                        pltpu.VMEM((HEAD_DIM, W), F32), pltpu.VMEM((HEAD_DIM, W), F32),
                        pltpu.SMEM((2,), F32)],
        compiler_params=_cparams(("arbitrary", "arbitrary", "arbitrary")),
        name="nsa_attn",
    )(proj, k_cmp, v_cmp, proj, proj, proj, proj, aux_g, gb, strip, strip_amax, ustrip, cover)


def _nsa_mixer(xf, mod_all, tab, gain, w_in, gate_b, w_out, q_gain, k_gain, cmp_pos, cmp_w1, cmp_w2, strip, strip_amax,
               ustrip, *, batch, seq):
    AW, KW = N_HEADS * HEAD_DIM, N_KV_HEADS * HEAD_DIM
    tn = KW
    wp = _proj_layout(w_in, [(0, AW), (AW + 2 * KW, KW), (AW + 3 * KW, KW), (AW + 4 * KW, KW), (AW + 5 * KW, KW),
                             (AW, KW), (AW + KW, KW), (AW + 6 * KW, 3 * N_HEADS)], tn)
    nq_t = AW // tn
    gain_cols = jnp.concatenate([jnp.tile(q_gain * (HEAD_DIM ** -0.5 * LOG2E), N_HEADS), jnp.tile(k_gain[1], N_KV_HEADS),
                                 jnp.zeros((tn,), F32), jnp.tile(k_gain[2], N_KV_HEADS),
                                 jnp.zeros((4 * tn,), F32)]).reshape(1, -1)
    proj, kc, vc, aux_g = _norm_matmul(xf, mod_all, tab, gain, wp, mod_row=3, seq=seq, tn=tn, gain_cols=gain_cols,
                                       norm_tiles=tuple(range(nq_t)) + (nq_t, nq_t + 2), n_aux=3)
    k_cmp, v_cmp = _nsa_compress(kc, vc, cmp_pos, cmp_w1, cmp_w2, k_gain[0], batch=batch, seq=seq)
    per = tn // HEAD_DIM
    blks = {"ks": nq_t * per, "vs": (nq_t + 1) * per, "kw": (nq_t + 2) * per, "vw": (nq_t + 3) * per}
    o = _nsa_attn(proj, aux_g, gate_b, k_cmp, v_cmp, strip, strip_amax, ustrip, batch=batch, seq=seq, blks=blks)
    return _matmul_res(o, w_out.astype(BF16), xf, mod_all, tab, mod_row=5, seq=seq)


def _rglru_kernel(gbr_ref, xr_ref, cw_ref, cb_ref, gw_ref, gbias_ref, lam_ref, o_ref, h_scr, tail_scr, *, ts, bd):
    @pl.when(pl.program_id(2) == 0)
    def _():
        h_scr[...] = jnp.zeros(h_scr.shape, F32)
        tail_scr[...] = jnp.zeros(tail_scr.shape, F32)

    x = xr_ref[...]
    cbw = x.shape[1]
    xfull = jnp.concatenate([tail_scr[...], x], axis=0)
    xc = cw_ref[CONV_WIDTH - 1:CONV_WIDTH, :] * x + cb_ref[...]
    for w in range(1, CONV_WIDTH):
        xc = xc + cw_ref[CONV_WIDTH - 1 - w:CONV_WIDTH - w, :] * pltpu.roll(xfull, w, 0)[SUBLANES:, :]
    tail_scr[...] = x[ts - SUBLANES:ts, :]

    gr = []
    for gi in range(2):
        parts = [jnp.dot(xc[:, n * bd:(n + 1) * bd].astype(BF16), gw_ref[gi, n], preferred_element_type=F32)
                 for n in range(cbw // bd)]
        gr.append(jnp.concatenate(parts, axis=1) + gbias_ref[gi:gi + 1, :])
    r = jax.nn.sigmoid(gr[0])
    i_g = jax.nn.sigmoid(gr[1])
    nl = -lam_ref[...]
    softplus = jnp.maximum(nl, 0.0) + jnp.log1p(jnp.exp(-jnp.abs(nl)))
    log_a = -RG_C * r * softplus
    a = jnp.exp(log_a)
    u = jnp.sqrt(1.0 - jnp.exp(2.0 * log_a)) * (i_g * xc)

    row = lax.broadcasted_iota(I32, (ts, cbw), 0)
    sft = 1
    while sft < ts:
        keep = row >= sft
        a_sh = jnp.where(keep, pltpu.roll(a, sft, 0), 1.0)
        u_sh = jnp.where(keep, pltpu.roll(u, sft, 0), 0.0)
        u = u + a * u_sh
        a = a * a_sh
        sft *= 2
    hs = u + a * h_scr[...]
    h_scr[...] = hs[ts - 1:ts, :]
    o_ref[...] = (_gelu_tanh(gbr_ref[...]) * hs).astype(o_ref.dtype)


def _rglru_mixer(xf, mod_all, tab, gain, w_in, conv_w, conv_b, gate_w, gate_b, lam, w_out, *, batch, seq):
    d_rnn = conv_w.shape[1]
    bd = d_rnn // RG_BLOCKS
    tn = min(512, d_rnn)
    proj = _norm_matmul(xf, mod_all, tab, gain, w_in.astype(BF16), mod_row=3, seq=seq, tn=tn, out_dtype=F32)
    cbw = min(512, d_rnn)
    ncb = d_rnn // cbw
    ts = min(256, seq)
    nts = seq // ts
    kern = functools.partial(_rglru_kernel, ts=ts, bd=bd)
    y = pl.pallas_call(
        kern,
        grid=(batch, ncb, nts),
        in_specs=[pl.BlockSpec((ts, cbw), lambda b, c, t: (b * nts + t, c)),
                  pl.BlockSpec((ts, cbw), lambda b, c, t: (b * nts + t, ncb + c)),
                  pl.BlockSpec((CONV_WIDTH, cbw), lambda b, c, t: (0, c)),
                  pl.BlockSpec((1, cbw), lambda b, c, t: (0, c)),
                  pl.BlockSpec((2, cbw // bd, bd, bd), lambda b, c, t: (0, c, 0, 0)),
                  pl.BlockSpec((2, cbw), lambda b, c, t: (0, c)),
                  pl.BlockSpec((1, cbw), lambda b, c, t: (0, c))],
        out_specs=pl.BlockSpec((ts, cbw), lambda b, c, t: (b * nts + t, c)),
        out_shape=jax.ShapeDtypeStruct((batch * seq, d_rnn), BF16),
        scratch_shapes=[pltpu.VMEM((1, cbw), F32), pltpu.VMEM((SUBLANES, cbw), F32)],
        compiler_params=_cparams(("arbitrary", "arbitrary", "arbitrary")),
        name="rglru",
    )(proj, proj, conv_w, conv_b.reshape(1, d_rnn), gate_w.astype(BF16), gate_b, lam.reshape(1, d_rnn))
    return _matmul_res(y, w_out.astype(BF16), xf, mod_all, tab, mod_row=5, seq=seq)


def kernel(x, c, rel_bias, ada_w, ada_b, ada_table, norm_g, ffn_w_in, ffn_w_out, dsa_w_in, dsa_w_out, dsa_q_gain, dsa_k_gain, nsa_w_in, nsa_gate_b, nsa_w_out, nsa_q_gain, nsa_k_gain, nsa_cmp_pos, nsa_cmp_w1, nsa_cmp_w2, rg_w_in, rg_conv_w, rg_conv_b, rg_gate_w, rg_gate_b, rg_lambda, rg_w_out):
    B, S, D = x.shape
    depth = ada_table.shape[0]
    assert S % TQ == 0 and S >= NEAR_TILES * TQ
    mod_all = _mod_all(c, ada_w, ada_b)
    strip, strip_amax, ustrip = _bias_strips(rel_bias)
    xf = x.reshape(B * S, D)
    for layer in range(depth):
        tab = ada_table[layer]
        xf = _ffn(xf, mod_all, tab, norm_g[layer, 0], ffn_w_in[layer, 0], ffn_w_out[layer, 0], mod_row=0, seq=S)
        kind, j = layer % 3, layer // 3
        if kind == 0:
            xf = _dsa_mixer(xf, mod_all, tab, norm_g[layer, 1], dsa_w_in[j], dsa_w_out[j], dsa_q_gain[j],
                            dsa_k_gain[j], strip, strip_amax, batch=B, seq=S)
        elif kind == 1:
            xf = _nsa_mixer(xf, mod_all, tab, norm_g[layer, 1], nsa_w_in[j], nsa_gate_b[j], nsa_w_out[j],
                            nsa_q_gain[j], nsa_k_gain[j], nsa_cmp_pos[j], nsa_cmp_w1[j], nsa_cmp_w2[j],
                            strip, strip_amax, ustrip, batch=B, seq=S)
        else:
            xf = _rglru_mixer(xf, mod_all, tab, norm_g[layer, 1], rg_w_in[j], rg_conv_w[j], rg_conv_b[j],
                              rg_gate_w[j], rg_gate_b[j], rg_lambda[j], rg_w_out[j], batch=B, seq=S)
        xf = _ffn(xf, mod_all, tab, norm_g[layer, 2], ffn_w_in[layer, 1], ffn_w_out[layer, 1], mod_row=6, seq=S)
    return xf.reshape(B, S, D)
```

```python
import functools
import math

import numpy as np
import jax
import jax.numpy as jnp
from jax import lax
from jax.experimental import pallas as pl
from jax.experimental.pallas import tpu as pltpu

N_HEADS = 32
HEAD_DIM = 128
N_KV_HEADS = 4
FFN_RES = 0.5
N_MOD = 9
NUM_BUCKETS = 32
MAX_DISTANCE = 1024
EPS = 1e-6
NEG = -1e30
BIG = 1e30
IDX_HEADS = 16
IDX_DIM = 128
TOPK_KEYS = 256
CMP_BLOCK = 32
CMP_STRIDE = 16
SEL_BLOCK = 64
SEL_TOPN = 16
WINDOW = 512
RG_BLOCKS = 16
CONV_WIDTH = 4
RG_C = 8.0

LANES = 128
SUBLANES = 8
VMEM_LIMIT_BYTES = 56 * 1024 * 1024

F32 = jnp.float32
BF16 = jnp.bfloat16
I32 = jnp.int32
INT_MIN = -2 ** 31
LOWEST = -3.0e38

TQ = 128
TK = 512
KPT = TK // TQ
SAFE_LOGIT = 60.0
LOG2E = 1.4426950408889634
FAR_DIST = int(math.ceil((NUM_BUCKETS // 2) * (MAX_DISTANCE / (NUM_BUCKETS // 2)) ** ((NUM_BUCKETS // 2 - 1) / (NUM_BUCKETS // 2)))) + 8
NEAR_TILES = -(-(FAR_DIST + TQ - 1) // TQ)
CPQ = TQ // CMP_STRIDE
CMP_LAST = CMP_STRIDE * (CPQ - 1) + CMP_BLOCK - 1
NEAR_CMP = -(-(-(-(FAR_DIST + CMP_LAST) // CMP_STRIDE)) // SUBLANES) * SUBLANES

STRIP_PAD = TK - TQ
STRIP_ROWS = NEAR_TILES * TQ + 2 * STRIP_PAD
NT_DIMS = (((1,), (1,)), ((), ()))
TN_DIMS = (((0,), (0,)), ((), ()))


def _round_up(n, m):
    return (n + m - 1) // m * m


def _cparams(sem):
    return pltpu.CompilerParams(dimension_semantics=sem, vmem_limit_bytes=VMEM_LIMIT_BYTES)


def _gelu_tanh(x):
    return 0.5 * x * (1.0 + jnp.tanh(0.7978845608028654 * (x + 0.044715 * x * x * x)))


def _mod_kernel(c_ref, w_ref, b_ref, o_ref):
    c = c_ref[...]
    s = c * jax.nn.sigmoid(c)
    o_ref[...] = jnp.dot(s.astype(BF16), w_ref[...].astype(BF16), preferred_element_type=F32) + b_ref[...]


def _mod_all(c, ada_w, ada_b):
    B, D = c.shape
    N = ada_w.shape[1]
    rows = 16
    tn = next(t for t in (512, 256, 128) if N % t == 0)
    cp = jnp.zeros((rows, D), F32).at[:B].set(c)
    out = pl.pallas_call(
        _mod_kernel,
        grid=(N // tn,),
        in_specs=[pl.BlockSpec((rows, D), lambda j: (0, 0)),
                  pl.BlockSpec((D, tn), lambda j: (0, j)),
                  pl.BlockSpec((1, tn), lambda j: (0, j))],
        out_specs=pl.BlockSpec((rows, tn), lambda j: (0, j)),
        out_shape=jax.ShapeDtypeStruct((rows, N), F32),
        compiler_params=_cparams(("arbitrary",)),
        name="ada_mod",
    )(cp, ada_w, ada_b.reshape(1, N))
    return out[:B].reshape(B, N_MOD, D)


def _ada_norm_block(x, g, mod, tab, k):
    var = jnp.mean(x * x, axis=-1, keepdims=True)
    y = x * lax.rsqrt(var + EPS) * g
    shift = mod[k:k + 1, :] + tab[k:k + 1, :]
    scale = mod[k + 1:k + 2, :] + tab[k + 1:k + 2, :]
    return y * (1.0 + scale) + shift


def _norm_matmul_kernel(x_ref, mod_ref, tab_ref, g_ref, w_ref, gc_ref, *refs, mod_row, epilogue,
                        norm_tiles, n_aux, n_tiles):
    if epilogue == "swiglu":
        wu_ref, refs = refs[0], refs[1:]
    o_ref = refs[0]
    aux_refs = refs[1:1 + n_aux]
    y_scr = refs[1 + n_aux]
    j = pl.program_id(1)

    @pl.when(j == 0)
    def _():
        y = _ada_norm_block(x_ref[...], g_ref[...], mod_ref[0], tab_ref[...], mod_row)
        y_scr[...] = y.astype(BF16)

    h = jnp.dot(y_scr[...], w_ref[...], preferred_element_type=F32)
    if epilogue == "swiglu":
        u = jnp.dot(y_scr[...], wu_ref[...], preferred_element_type=F32)
        o_ref[...] = (h * jax.nn.sigmoid(h) * u).astype(o_ref.dtype)
        return
    if norm_tiles:
        is_norm = functools.reduce(lambda a, b: a | b, [j == t for t in norm_tiles])

        @pl.when(is_norm)
        def _():
            for c in range(h.shape[1] // LANES):
                hc = h[:, c * LANES:(c + 1) * LANES]
                ms = jnp.mean(hc * hc, axis=-1, keepdims=True)
                yc = hc * lax.rsqrt(ms + EPS) * gc_ref[:, c * LANES:(c + 1) * LANES]
                o_ref[:, c * LANES:(c + 1) * LANES] = yc.astype(o_ref.dtype)

        @pl.when(jnp.logical_not(is_norm))
        def _():
            o_ref[...] = h.astype(o_ref.dtype)
    else:
        o_ref[...] = h.astype(o_ref.dtype)
    for a in range(n_aux):
        @pl.when(j == n_tiles - n_aux + a)
        def _(a=a):
            aux_refs[a][...] = h


def _weight_spec(w, rows, cols, w_index, col_block):
    if w.ndim == 2:
        return pl.BlockSpec((rows, cols), lambda i, j: (0, col_block(j)))
    return pl.BlockSpec((None, rows, cols), lambda i, j: (w_index, 0, col_block(j)))


def _norm_matmul(x, mod_all, tab, gain, w, *, mod_row, seq, epilogue="plain", tn, tm=512,
                 out_dtype=BF16, gain_cols=None, norm_tiles=(), n_aux=0, w_index=0):
    T, D = x.shape
    swiglu = epilogue == "swiglu"
    n_tiles = w.shape[-1] // (2 * tn if swiglu else tn)
    n_out = n_tiles * tn
    tm = min(tm, seq)
    bpb = seq // tm
    if gain_cols is None:
        gain_cols = jnp.zeros((1, n_out), F32)
    out_shape = [jax.ShapeDtypeStruct((T, n_out), out_dtype)]
    out_specs = [pl.BlockSpec((tm, tn), lambda i, j: (i, j))]
    for _ in range(n_aux):
        out_shape.append(jax.ShapeDtypeStruct((T, tn), F32))
        out_specs.append(pl.BlockSpec((tm, tn), lambda i, j: (i, 0)))
    kern = functools.partial(_norm_matmul_kernel, mod_row=mod_row, epilogue=epilogue,
                             norm_tiles=tuple(norm_tiles), n_aux=n_aux, n_tiles=n_tiles)
    in_specs = [pl.BlockSpec((tm, D), lambda i, j: (i, 0)),
                pl.BlockSpec((1, N_MOD, D), lambda i, j: (i // bpb, 0, 0)),
                pl.BlockSpec((N_MOD, D), lambda i, j: (0, 0)),
                pl.BlockSpec((1, D), lambda i, j: (0, 0)),
                _weight_spec(w, D, tn, w_index, lambda j: j),
                pl.BlockSpec((1, tn), lambda i, j: (0, j))]
    args = [x, mod_all, tab, gain.reshape(1, D), w, gain_cols]
    if swiglu:
        in_specs.append(_weight_spec(w, D, tn, w_index, lambda j: n_tiles + j))
        args.append(w)
    res = pl.pallas_call(
        kern,
        grid=(T // tm, n_tiles),
        in_specs=in_specs,
        out_specs=out_specs,
        out_shape=out_shape,
        scratch_shapes=[pltpu.VMEM((tm, D), BF16)],
        compiler_params=_cparams(("arbitrary", "arbitrary")),
        name="norm_matmul_" + epilogue,
    )(*args)
    return res if n_aux else res[0]


def _matmul_res_kernel(a_ref, w_ref, x_ref, mod_ref, tab_ref, o_ref, *, mod_row, res_scale):
    h = jnp.dot(a_ref[...], w_ref[...], preferred_element_type=F32)
    gate = mod_ref[0, mod_row:mod_row + 1, :] + tab_ref[mod_row:mod_row + 1, :]
    if res_scale != 1.0:
        gate = res_scale * gate
    o_ref[...] = x_ref[...] + gate * h


def _matmul_res(a, w, x, mod_all, tab, *, mod_row, seq, res_scale=1.0, tm=512, tn=1024, w_index=0):
    T, K = a.shape
    D = w.shape[-1]
    tm = min(tm, seq)
    tn = min(tn, D)
    bpb = seq // tm
    kern = functools.partial(_matmul_res_kernel, mod_row=mod_row, res_scale=res_scale)
    return pl.pallas_call(
        kern,
        grid=(T // tm, D // tn),
        in_specs=[pl.BlockSpec((tm, K), lambda i, j: (i, 0)),
                  _weight_spec(w, K, tn, w_index, lambda j: j),
                  pl.BlockSpec((tm, tn), lambda i, j: (i, j)),
                  pl.BlockSpec((1, N_MOD, tn), lambda i, j: (i // bpb, 0, j)),
                  pl.BlockSpec((N_MOD, tn), lambda i, j: (0, j))],
        out_specs=pl.BlockSpec((tm, tn), lambda i, j: (i, j)),
        out_shape=jax.ShapeDtypeStruct((T, D), F32),
        compiler_params=_cparams(("arbitrary", "arbitrary")),
        name="matmul_residual",
    )(a, w, x, mod_all, tab)


def _ffn_tile(d_ff):
    return 512 if d_ff >= 512 else _round_up(d_ff, LANES)


def _pad_cols_kernel(x_ref, o_ref):
    f = x_ref.shape[1]
    o_ref[:, :f] = x_ref[...].astype(o_ref.dtype)
    if o_ref.shape[1] > f:
        o_ref[:, f:] = jnp.zeros((o_ref.shape[0], o_ref.shape[1] - f), o_ref.dtype)


def _pad_rows_kernel(x_ref, o_ref, *, rows_valid):
    tr = o_ref.shape[0]
    row = pl.program_id(1) * tr + lax.broadcasted_iota(I32, o_ref.shape, 0)
    o_ref[...] = jnp.where(row < rows_valid, x_ref[...], 0.0).astype(o_ref.dtype)


def _prep_ffn_weights(ffn_w_in, ffn_w_out):
    L, two, D, two_f = ffn_w_in.shape
    F = two_f // 2
    tf = _ffn_tile(F)
    Fp = _round_up(F, tf)
    rows = L * two * D
    tr = 256
    w1 = pl.pallas_call(
        _pad_cols_kernel,
        grid=(rows // tr, 2),
        in_specs=[pl.BlockSpec((tr, F), lambda r, h: (r, h))],
        out_specs=pl.BlockSpec((tr, Fp), lambda r, h: (r, h)),
        out_shape=jax.ShapeDtypeStruct((rows, 2 * Fp), BF16),
        compiler_params=_cparams(("arbitrary", "arbitrary")),
        name="ffn_w_in_prep",
    )(ffn_w_in.reshape(rows, two_f))
    w2 = pl.pallas_call(
        functools.partial(_pad_rows_kernel, rows_valid=F),
        grid=(L * two, Fp // tf),
        in_specs=[pl.BlockSpec((None, tf, D), lambda l, r: (l, r, 0))],
        out_specs=pl.BlockSpec((None, tf, D), lambda l, r: (l, r, 0)),
        out_shape=jax.ShapeDtypeStruct((L * two, Fp, D), BF16),
        compiler_params=_cparams(("arbitrary", "arbitrary")),
        name="ffn_w_out_prep",
    )(ffn_w_out.reshape(L * two, F, D))
    return w1.reshape(L * two, D, 2 * Fp), w2, tf


def _ffn(x, mod_all, tab, gain, w1, w2, tf, w_index, *, mod_row, seq):
    a = _norm_matmul(x, mod_all, tab, gain, w1, mod_row=mod_row, seq=seq, epilogue="swiglu", tn=tf, w_index=w_index)
    return _matmul_res(a, w2, x, mod_all, tab, mod_row=mod_row + 2, seq=seq, res_scale=FFN_RES, w_index=w_index)


def _t5_bucket(dist):
    n = jnp.maximum(dist, 0)
    max_exact = NUM_BUCKETS // 2
    nf = jnp.maximum(n, 1).astype(F32)
    large = max_exact + (jnp.log(nf / max_exact) / math.log(MAX_DISTANCE / max_exact)
                         * (NUM_BUCKETS - max_exact)).astype(I32)
    large = jnp.minimum(large, NUM_BUCKETS - 1)
    return jnp.where(n < max_exact, n, large)


def _bias_strip_kernel(rb_ref, bkt_ref, o_ref, amax_ref):
    h = pl.program_id(0)
    bk = bkt_ref[...]
    far = rb_ref[NUM_BUCKETS - 1, h]
    acc = jnp.zeros(bk.shape, F32)
    for k in range(NUM_BUCKETS - 1):
        acc = jnp.where(bk == k, LOG2E * (rb_ref[k, h] - far), acc)
    o_ref[...] = acc
    amax_ref[...] = jnp.max(jnp.abs(acc), axis=0, keepdims=True)


def _bias_strip(rel_bias, bkt):
    rows = bkt.shape[0]
    H = rel_bias.shape[1]
    return pl.pallas_call(
        _bias_strip_kernel,
        grid=(H,),
        in_specs=[pl.BlockSpec(memory_space=pltpu.SMEM),
                  pl.BlockSpec((rows, TQ), lambda h: (0, 0))],
        out_specs=[pl.BlockSpec((rows, TQ), lambda h: (0, h)), pl.BlockSpec((1, TQ), lambda h: (0, h))],
        out_shape=[jax.ShapeDtypeStruct((rows, H * TQ), F32), jax.ShapeDtypeStruct((1, H * TQ), F32)],
        compiler_params=_cparams(("arbitrary",)),
        name="bias_strip",
    )(rel_bias, bkt)


def _bias_strips(rel_bias):
    iq = jnp.arange(TQ, dtype=I32)[None, :]
    x = jnp.arange(-STRIP_PAD, NEAR_TILES * TQ + STRIP_PAD, dtype=I32)[:, None]
    bkt_tok = _t5_bucket(iq - x + (NEAR_TILES - 1) * TQ)
    y = jnp.arange(NEAR_CMP, dtype=I32)[:, None]
    bkt_cmp = _t5_bucket(iq + CMP_STRIDE * (NEAR_CMP - 1 - y) - CMP_LAST)
    strip, strip_amax = _bias_strip(rel_bias, bkt_tok)
    ustrip, _ = _bias_strip(rel_bias, bkt_cmp)
    return strip, strip_amax, ustrip


def _stack_heads(q_ref, qs_scr, n):
    for r in range(n):
        qs_scr[r * TQ:(r + 1) * TQ, :] = q_ref[:, r * LANES:(r + 1) * LANES]


def _num_far_tiles(i):
    return jnp.maximum(i - NEAR_TILES + 1, 0) // KPT


def _strip_row(i, kt):
    return pl.multiple_of((KPT + NEAR_TILES - 2 - (i - kt * KPT)) * TQ, TQ)


def _flash_init(m_scr, l_scr, acc_scr):
    m_scr[...] = jnp.full(m_scr.shape, NEG, F32)
    l_scr[...] = jnp.zeros(l_scr.shape, F32)
    acc_scr[...] = jnp.zeros(acc_scr.shape, F32)


def _max_row_sumsq(k_ref):
    rows = k_ref.shape[0]
    ch = min(rows, 1024)

    def body(c, best):
        kk = k_ref[pl.ds(pl.multiple_of(c * ch, ch), ch), :].astype(F32)
        return jnp.maximum(best, jnp.max(jnp.sum(kk * kk, axis=1, keepdims=True)))

    return lax.fori_loop(0, rows // ch, body, jnp.float32(0.0))


def _logits_bounded(qs_scr, k_sumsq, amax_row):
    qf = qs_scr[...].astype(F32)
    q_sumsq = lax.dot_general(jnp.ones((SUBLANES, HEAD_DIM), BF16), (qf * qf).astype(BF16), NT_DIMS,
                              preferred_element_type=F32)[0:1, :]
    bound = jnp.sqrt(q_sumsq * k_sumsq) * 1.02 + amax_row
    return jnp.max(bound) <= SAFE_LOGIT


def _flash_tile(k_tile, v_tile, qs_scr, madd, strip_ref, strip_row, m_scr, l_scr, acc_scr, n_heads, online):
    s = lax.dot_general(k_tile, qs_scr[...], NT_DIMS, preferred_element_type=F32) + jnp.tile(madd, (1, n_heads))
    if strip_ref is not None:
        s = s + strip_ref[pl.ds(strip_row, TK), :]
    if not online:
        p = jnp.exp2(s)
        l_scr[...] += jnp.sum(p, axis=0, keepdims=True)
        acc_scr[...] += lax.dot_general(v_tile, p.astype(BF16), TN_DIMS, preferred_element_type=F32)
        return
    m_prev = m_scr[...]
    m_new = jnp.maximum(m_prev, jnp.max(s, axis=0, keepdims=True))
    alpha = jnp.exp2(m_prev - m_new)
    p = jnp.exp2(s - m_new)
    l_scr[...] = alpha * l_scr[...] + jnp.sum(p, axis=0, keepdims=True)
    pv = lax.dot_general(v_tile, p.astype(BF16), TN_DIMS, preferred_element_type=F32)
    acc_scr[...] = alpha * acc_scr[...] + pv
    m_scr[...] = m_new


def _flash_sweep(bounded, sweep, m_scr, l_scr, acc_scr):
    _flash_init(m_scr, l_scr, acc_scr)

    @pl.when(bounded)
    def _():
        sweep(False)

    @pl.when(jnp.logical_not(bounded))
    def _():
        sweep(True)
        l_scr[...] = jnp.where(m_scr[...] > 0.5 * NEG, l_scr[...], 0.0)

    l = l_scr[...]
    return acc_scr[...] * jnp.where(l > 0.0, 1.0 / l, 0.0)


def _write_heads(o_ref, o_t, n):
    for r in range(n):
        o_ref[:, r * LANES:(r + 1) * LANES] = o_t[:, r * TQ:(r + 1) * TQ].T.astype(o_ref.dtype)


def _dsa_index_kernel(qi_ref, ki_ref, wi_ref, mask_ref, keys_scr, qis_scr, j_scr, *, n_keep, ih, seq):
    i = pl.program_id(1)
    nkt = i // KPT + 1
    _stack_heads(qi_ref, qis_scr, ih)
    w_t = wi_ref[...].T * (ih ** -0.5 * IDX_DIM ** -0.5)
    w_rows = [w_t[h:h + 1, :] for h in range(ih)]
    q_pos = i * TQ + lax.broadcasted_iota(I32, (TQ, TQ), 1)
    k_iota = lax.broadcasted_iota(I32, (TQ, TQ), 0)

    def sub_rows(kt, u):
        return pl.multiple_of(kt * TK + u * TQ, TQ)

    def score_tile(kt, carry):
        k0 = pl.multiple_of(kt * TK, TK)
        d = lax.dot_general(ki_ref[pl.ds(k0, TK), :], qis_scr[...], NT_DIMS, preferred_element_type=F32)
        for u in range(KPT):
            sc = jnp.zeros((TQ, TQ), F32)
            for h in range(ih):
                sc = sc + jnp.maximum(d[u * TQ:(u + 1) * TQ, h * TQ:(h + 1) * TQ], 0.0) * w_rows[h]
            bits = pltpu.bitcast(sc, I32)
            key = jnp.where(bits < 0, bits ^ 0x7FFFFFFF, bits)
            r0 = sub_rows(kt, u)
            keys_scr[pl.ds(r0, TQ), :] = jnp.where(r0 + k_iota <= q_pos, key, INT_MIN)
        return carry

    lax.fori_loop(0, nkt, score_tile, 0)

    def count(pred):
        def body(kt, acc):
            for u in range(KPT):
                r0 = sub_rows(kt, u)
                acc = acc + jnp.where(pred(keys_scr[pl.ds(r0, TQ), :], r0 + k_iota), 1.0, 0.0)
            return acc
        acc = lax.fori_loop(0, nkt, body, jnp.zeros((TQ, TQ), F32))
        return jnp.sum(acc, axis=0, keepdims=True)

    def bit_body(b, lo):
        cand = lo + jnp.left_shift(jnp.int32(1), 31 - b)
        cnt = count(lambda kk, idx: kk >= cand)
        return jnp.where(cnt >= n_keep, cand, lo)

    thr = lax.fori_loop(0, 32, bit_body, jnp.full((1, TQ), INT_MIN, I32))
    cnt_ge = count(lambda kk, idx: kk >= thr)
    has_tie = (cnt_ge > n_keep) & (thr > INT_MIN)
    j_scr[...] = jnp.full((1, TQ), seq, I32)

    @pl.when(jnp.max(jnp.where(has_tie, 1.0, 0.0)) > 0.5)
    def _():
        need = n_keep - count(lambda kk, idx: kk > thr)

        def jb(b, lo):
            cand = lo + jnp.left_shift(jnp.int32(1), (seq.bit_length() - 2) - b)
            c = count(lambda kk, idx: (kk == thr) & (idx < cand))
            return jnp.where(c < need, cand, lo)

        j_last = lax.fori_loop(0, seq.bit_length() - 1, jb, jnp.zeros((1, TQ), I32))
        j_scr[...] = jnp.where(has_tie, j_last, seq)

    thr_c = jnp.maximum(thr, INT_MIN + 1)
    j_last = j_scr[...]

    def write_tile(kt, carry):
        for u in range(KPT):
            r0 = sub_rows(kt, u)
            kk = keys_scr[pl.ds(r0, TQ), :]
            sel = (kk > thr_c) | ((kk == thr_c) & (r0 + k_iota <= j_last))
            mask_ref[pl.ds(r0, TQ), :] = jnp.where(sel, 0.0, NEG).astype(BF16)
        return carry

    lax.fori_loop(0, nkt, write_tile, 0)

    def fill_tile(kt, carry):
        k0 = pl.multiple_of(kt * TK, TK)
        mask_ref[pl.ds(k0, TK), :] = jnp.full((TK, TQ), NEG, BF16)
        return carry

    lax.fori_loop(nkt, seq // TK, fill_tile, 0)


def _dsa_index(proj, aux, *, batch, seq, qi_blk, ki_blk, n_keep):
    nq = seq // TQ
    ihw = IDX_HEADS * IDX_DIM
    kern = functools.partial(_dsa_index_kernel, n_keep=n_keep, ih=IDX_HEADS, seq=seq)
    return pl.pallas_call(
        kern,
        grid=(batch, nq),
        in_specs=[pl.BlockSpec((TQ, ihw), lambda b, i: (b * nq + i, qi_blk)),
                  pl.BlockSpec((seq, IDX_DIM), lambda b, i: (b, ki_blk)),
                  pl.BlockSpec((TQ, LANES), lambda b, i: (b * nq + i, 1))],
        out_specs=pl.BlockSpec((None, seq, TQ), lambda b, i: (b, 0, i)),
        out_shape=jax.ShapeDtypeStruct((batch, seq, seq), BF16),
        scratch_shapes=[pltpu.VMEM((seq, TQ), I32),
                        pltpu.VMEM((IDX_HEADS * TQ, IDX_DIM), BF16),
                        pltpu.VMEM((1, TQ), I32)],
        compiler_params=_cparams(("arbitrary", "arbitrary")),
        name="dsa_index",
    )(proj, proj, aux)


def _dsa_attn_kernel(q_ref, k_ref, v_ref, mask_ref, strip_ref, amax_ref, o_ref, qs_scr, m_scr, l_scr, acc_scr,
                     ksq_scr, *, r_heads):
    i = pl.program_id(2)

    @pl.when(i == 0)
    def _():
        ksq_scr[0] = _max_row_sumsq(k_ref)

    _stack_heads(q_ref, qs_scr, r_heads)
    n_far = _num_far_tiles(i)

    def sweep(online):
        def tile(kt, near):
            k0 = pl.multiple_of(kt * TK, TK)
            _flash_tile(k_ref[pl.ds(k0, TK), :], v_ref[pl.ds(k0, TK), :], qs_scr,
                        mask_ref[pl.ds(k0, TK), :].astype(F32), strip_ref if near else None, _strip_row(i, kt),
                        m_scr, l_scr, acc_scr, r_heads, online)

        def far_body(kt, carry):
            tile(kt, False)
            return carry

        def near_body(kt, carry):
            tile(kt, True)
            return carry

        lax.fori_loop(0, n_far, far_body, 0)
        lax.fori_loop(n_far, i // KPT + 1, near_body, 0)

    bounded = _logits_bounded(qs_scr, ksq_scr[0], amax_ref[...])
    _write_heads(o_ref, _flash_sweep(bounded, sweep, m_scr, l_scr, acc_scr), r_heads)


def _dsa_attn(proj, mask, strip, strip_amax, *, batch, seq, k_blk, v_blk):
    nq = seq // TQ
    G = N_KV_HEADS
    R = N_HEADS // G
    W = R * TQ
    kern = functools.partial(_dsa_attn_kernel, r_heads=R)
    return pl.pallas_call(
        kern,
        grid=(batch, G, nq),
        in_specs=[pl.BlockSpec((TQ, R * HEAD_DIM), lambda b, g, i: (b * nq + i, g)),
                  pl.BlockSpec((seq, HEAD_DIM), lambda b, g, i: (b, k_blk + g)),
                  pl.BlockSpec((seq, HEAD_DIM), lambda b, g, i: (b, v_blk + g)),
                  pl.BlockSpec((None, seq, TQ), lambda b, g, i: (b, 0, i)),
                  pl.BlockSpec((STRIP_ROWS, W), lambda b, g, i: (0, g)),
                  pl.BlockSpec((1, W), lambda b, g, i: (0, g))],
        out_specs=pl.BlockSpec((TQ, R * HEAD_DIM), lambda b, g, i: (b * nq + i, g)),
        out_shape=jax.ShapeDtypeStruct((batch * seq, N_HEADS * HEAD_DIM), BF16),
        scratch_shapes=[pltpu.VMEM((W, HEAD_DIM), BF16),
                        pltpu.VMEM((1, W), F32), pltpu.VMEM((1, W), F32), pltpu.VMEM((HEAD_DIM, W), F32),
                        pltpu.SMEM((1,), F32)],
        compiler_params=_cparams(("arbitrary", "arbitrary", "arbitrary")),
        name="dsa_attn",
    )(proj, proj, proj, mask, strip, strip_amax)


def _proj_layout(w_in, segments, tn):
    cols = []
    for start, width in segments:
        pad = _round_up(width, tn) - width
        cols.append(jnp.pad(w_in[:, start:start + width], ((0, 0), (0, pad))))
    return jnp.concatenate(cols, axis=1).astype(BF16)


def _dsa_mixer(xf, mod_all, tab, gain, w_in, w_out, q_gain, k_gain, strip, strip_amax, *, batch, seq):
    AW, KW = N_HEADS * HEAD_DIM, N_KV_HEADS * HEAD_DIM
    IW = IDX_HEADS * IDX_DIM
    tn = KW
    wp = _proj_layout(w_in, [(0, AW), (AW + 2 * KW, IW), (AW, KW), (AW + KW, KW), (AW + 2 * KW + IW, IDX_DIM + IDX_HEADS)], tn)
    nq_t, ni_t = AW // tn, IW // tn
    gain_cols = jnp.concatenate([jnp.tile(q_gain * (HEAD_DIM ** -0.5 * LOG2E), N_HEADS), jnp.zeros((IW,), F32),
                                 jnp.tile(k_gain, N_KV_HEADS), jnp.zeros((2 * tn,), F32)]).reshape(1, -1)
    proj, aux = _norm_matmul(xf, mod_all, tab, gain, wp, mod_row=3, seq=seq, tn=tn, gain_cols=gain_cols,
                             norm_tiles=tuple(range(nq_t)) + (nq_t + ni_t,), n_aux=1)
    k_blk = (nq_t + ni_t) * tn // HEAD_DIM
    mask = _dsa_index(proj, aux, batch=batch, seq=seq, qi_blk=AW // IW, ki_blk=(nq_t + ni_t + 2) * tn // IDX_DIM,
                      n_keep=min(TOPK_KEYS, seq // 4))
    o = _dsa_attn(proj, mask, strip, strip_amax, batch=batch, seq=seq, k_blk=k_blk, v_blk=k_blk + tn // HEAD_DIM)
    return _matmul_res(o, w_out.astype(BF16), xf, mod_all, tab, mod_row=5, seq=seq)


def _nsa_compress_kernel(xk_ref, xkn_ref, xv_ref, xvn_ref, pos_ref, w1_ref, w2_ref, gain_ref, ok_ref, ov_ref, *, groups, kw):
    half = CMP_BLOCK // 2
    rows = xk_ref.shape[0]
    row_id = lax.broadcasted_iota(I32, (rows, HEAD_DIM), 0)

    def branch(br, x_ref, xn_ref, o_ref):
        for g in range(groups):
            p1 = jnp.zeros((rows, HEAD_DIM), F32)
            p2 = jnp.zeros((rows, HEAD_DIM), F32)
            p2n = jnp.zeros((xn_ref.shape[0], HEAD_DIM), F32)
            for r in range(half):
                c0 = r * kw + g * HEAD_DIM
                xa = x_ref[:, c0:c0 + HEAD_DIM]
                w_lo = w1_ref[br, r * HEAD_DIM:(r + 1) * HEAD_DIM, :]
                w_hi = w1_ref[br, (half + r) * HEAD_DIM:(half + r + 1) * HEAD_DIM, :]
                p1 = p1 + jnp.dot((xa + pos_ref[br, r:r + 1, :]).astype(BF16), w_lo, preferred_element_type=F32)
                p2 = p2 + jnp.dot((xa + pos_ref[br, half + r:half + r + 1, :]).astype(BF16), w_hi,
                                  preferred_element_type=F32)
                xb = xn_ref[:, c0:c0 + HEAD_DIM]
                p2n = p2n + jnp.dot((xb + pos_ref[br, half + r:half + r + 1, :]).astype(BF16), w_hi,
                                    preferred_element_type=F32)
            p2s = jnp.where(row_id == rows - 1, p2n[0:1, :], pltpu.roll(p2, rows - 1, 0))
            hid = _gelu_tanh(p1 + p2s)
            out = jnp.dot(hid.astype(BF16), w2_ref[br], preferred_element_type=F32)
            if br == 0:
                ms = jnp.mean(out * out, axis=-1, keepdims=True)
                out = out * lax.rsqrt(ms + EPS) * gain_ref[...]
            o_ref[:, g * HEAD_DIM:(g + 1) * HEAD_DIM] = out.astype(o_ref.dtype)

    branch(0, xk_ref, xkn_ref, ok_ref)
    branch(1, xv_ref, xvn_ref, ov_ref)


def _nsa_compress(kc, vc, cmp_pos, cmp_w1, cmp_w2, k_gain0, *, batch, seq):
    assert CMP_BLOCK == 2 * CMP_STRIDE
    KW = N_KV_HEADS * HEAD_DIM
    ncp = seq // CMP_STRIDE
    rb = min(TQ, ncp)
    nrb = ncp // rb
    nxt = 16
    wide = CMP_STRIDE * KW
    xk = kc.reshape(batch * ncp, wide)
    xv = vc.reshape(batch * ncp, wide)
    last_nxt = batch * ncp // nxt - 1
    main = pl.BlockSpec((rb, wide), lambda b, r: (b * nrb + r, 0))
    ahead = pl.BlockSpec((nxt, wide), lambda b, r: (jnp.minimum((b * nrb + r + 1) * (rb // nxt), last_nxt), 0))
    kern = functools.partial(_nsa_compress_kernel, groups=N_KV_HEADS, kw=KW)
    return pl.pallas_call(
        kern,
        grid=(batch, nrb),
        in_specs=[main, ahead, main, ahead,
                  pl.BlockSpec((2, CMP_BLOCK, HEAD_DIM), lambda b, r: (0, 0, 0)),
                  pl.BlockSpec((2, CMP_BLOCK * HEAD_DIM, HEAD_DIM), lambda b, r: (0, 0, 0)),
                  pl.BlockSpec((2, HEAD_DIM, HEAD_DIM), lambda b, r: (0, 0, 0)),
                  pl.BlockSpec((1, HEAD_DIM), lambda b, r: (0, 0))],
        out_specs=[pl.BlockSpec((rb, KW), lambda b, r: (b * nrb + r, 0))] * 2,
        out_shape=[jax.ShapeDtypeStruct((batch * ncp, KW), BF16)] * 2,
        compiler_params=_cparams(("arbitrary", "arbitrary")),
        name="nsa_compress",
    )(xk, xk, xv, xv, cmp_pos, cmp_w1.astype(BF16), cmp_w2.astype(BF16), k_gain0.reshape(1, HEAD_DIM))


def _nsa_attn_kernel(q_ref, kc_ref, vc_ref, ks_ref, vs_ref, kw_ref, vw_ref, g_ref, gb_ref, strip_ref, amax_ref,
                     ustrip_ref, cover_ref, o_ref, qs_scr, lc_scr, sel_scr, gt_scr, m_scr, l_scr, acc_scr, ocmp_scr,
                     osel_scr, ksq_scr, *, r_heads, n_pick):
    g = pl.program_id(1)
    i = pl.program_id(2)
    R = r_heads

    @pl.when(i == 0)
    def _():
        ksq_scr[0] = _max_row_sumsq(ks_ref)
        ksq_scr[1] = _max_row_sumsq(kw_ref)

    ncp = kc_ref.shape[0]
    n_sel = cover_ref.shape[0]
    pad = NEAR_CMP - CPQ
    _stack_heads(q_ref, qs_scr, R)
    dk_iota = lax.broadcasted_iota(I32, (TK, TQ), 0) - lax.broadcasted_iota(I32, (TK, TQ), 1)

    lc_scr[0:pad, :] = jnp.zeros((pad, R * TQ), F32)
    lc_scr[pad:pad + ncp, :] = lax.dot_general(kc_ref[...], qs_scr[...], NT_DIMS, preferred_element_type=F32)
    off = pl.multiple_of(i * CPQ, SUBLANES)
    lc_scr[pl.ds(off, NEAR_CMP), :] += ustrip_ref[...]
    c_iota = lax.broadcasted_iota(I32, (ncp, TQ), 0)
    t_pos = i * TQ + lax.broadcasted_iota(I32, (ncp, TQ), 1)
    vis = jnp.where(c_iota * CMP_STRIDE + (CMP_BLOCK - 1) <= t_pos, 0.0, NEG)
    s = lc_scr[pad:pad + ncp, :] + jnp.tile(vis, (1, R))
    m = jnp.max(s, axis=0, keepdims=True)
    p = jnp.exp2(s - m)
    inv = jnp.where(m > 0.5 * NEG, 1.0 / jnp.sum(p, axis=0, keepdims=True), 0.0)
    pc = p * inv
    ocmp_scr[...] = lax.dot_general(vc_ref[...], pc.astype(BF16), TN_DIMS, preferred_element_type=F32)

    psum = pc[:, 0:TQ]
    for r in range(1, R):
        psum = psum + pc[:, r * TQ:(r + 1) * TQ]
    hi = psum.astype(BF16)
    lo = (psum - hi.astype(F32)).astype(BF16)
    imp = (jnp.dot(cover_ref[...], hi, preferred_element_type=F32)
           + jnp.dot(cover_ref[...], lo, preferred_element_type=F32))
    n_io = lax.broadcasted_iota(I32, (n_sel, TQ), 0)
    t_sel = i * TQ + lax.broadcasted_iota(I32, (n_sel, TQ), 1)
    cur = jnp.right_shift(t_sel, SEL_BLOCK.bit_length() - 1)
    forced = (n_io == 0) | (n_io == cur) | (n_io == cur - 1)
    val0 = jnp.where(forced, BIG, jnp.where(n_io * SEL_BLOCK <= t_sel, imp, NEG))
    n_f = n_io.astype(F32)

    def pick_round(_, carry):
        val, selm = carry
        cm = jnp.max(val, axis=0, keepdims=True)
        idx = jnp.min(jnp.where(val == cm, n_f, 1e9), axis=0, keepdims=True)
        pick = n_f == idx
        return jnp.where(pick, LOWEST, val), jnp.where(pick, 0.0, selm)

    _, selm = lax.fori_loop(0, n_pick, pick_round, (val0, jnp.full((n_sel, TQ), NEG, F32)))
    sel_scr[...] = selm

    bpt = TK // SEL_BLOCK
    n_far = _num_far_tiles(i)
    kt_last = i // KPT

    def sel_sweep(online):
        def sel_tile(kt, near):
            k0 = pl.multiple_of(kt * TK, TK)
            bm = jnp.concatenate([jnp.broadcast_to(sel_scr[pl.ds(kt * bpt + b, 1), :], (SEL_BLOCK, TQ))
                                  for b in range(bpt)], axis=0)
            if near:
                bm = jnp.where(dk_iota <= (i - kt * KPT) * TQ, bm, NEG)
            _flash_tile(ks_ref[pl.ds(k0, TK), :], vs_ref[pl.ds(k0, TK), :], qs_scr, bm,
                        strip_ref if near else None, _strip_row(i, kt), m_scr, l_scr, acc_scr, R, online)

        def far_body(kt, carry):
            sel_tile(kt, False)
            return carry

        def near_body(kt, carry):
            sel_tile(kt, True)
            return carry

        lax.fori_loop(0, n_far, far_body, 0)
        lax.fori_loop(n_far, kt_last + 1, near_body, 0)

    sel_bounded = _logits_bounded(qs_scr, ksq_scr[0], amax_ref[...])
    osel_scr[...] = _flash_sweep(sel_bounded, sel_sweep, m_scr, l_scr, acc_scr)

    def win_sweep(online):
        def win_body(kt, carry):
            k0 = pl.multiple_of(kt * TK, TK)
            dist = (i - kt * KPT) * TQ - dk_iota
            vis_w = jnp.where((dist >= 0) & (dist < WINDOW), 0.0, NEG)
            _flash_tile(kw_ref[pl.ds(k0, TK), :], vw_ref[pl.ds(k0, TK), :], qs_scr, vis_w,
                        strip_ref, _strip_row(i, kt), m_scr, l_scr, acc_scr, R, online)
            return carry

        lax.fori_loop(jnp.maximum(kt_last - 1, 0), kt_last + 1, win_body, 0)

    win_bounded = _logits_bounded(qs_scr, ksq_scr[1], amax_ref[...])
    o_win = _flash_sweep(win_bounded, win_sweep, m_scr, l_scr, acc_scr)

    gt_scr[...] = jax.nn.sigmoid(g_ref[:, 0:LANES] + gb_ref[...]).T

    def gate_row(br):
        rows = gt_scr[pl.ds(pl.multiple_of(br * N_HEADS + g * R, SUBLANES), R), :]
        return jnp.concatenate([rows[r:r + 1, :] for r in range(R)], axis=1)

    o_t = gate_row(0) * ocmp_scr[...] + gate_row(1) * osel_scr[...] + gate_row(2) * o_win
    _write_heads(o_ref, o_t, R)


def _nsa_attn(proj, aux_g, gate_b, k_cmp, v_cmp, strip, strip_amax, ustrip, *, batch, seq, blks):
    nq = seq // TQ
    G = N_KV_HEADS
    R = N_HEADS // G
    assert R == SUBLANES and 3 * N_HEADS <= LANES and WINDOW <= TK and WINDOW <= (NEAR_TILES - 1) * TQ
    W = R * TQ
    ncp = seq // CMP_STRIDE
    n_cmp = (seq - CMP_BLOCK) // CMP_STRIDE + 1
    n_sel = seq // SEL_BLOCK
    cs = np.arange(ncp)[None, :] * CMP_STRIDE
    ss = np.arange(n_sel)[:, None] * SEL_BLOCK
    cover = ((cs < ss + SEL_BLOCK) & (cs + CMP_BLOCK - 1 >= ss) & (np.arange(ncp)[None, :] < n_cmp))
    cover = jnp.asarray(cover.astype(np.float32), dtype=BF16)
    gb = jnp.zeros((1, LANES), F32).at[0, :3 * N_HEADS].set(gate_b)
    kv = lambda blk: pl.BlockSpec((seq, HEAD_DIM), lambda b, g, i: (b, blk + g))
    cmp_spec = pl.BlockSpec((ncp, HEAD_DIM), lambda b, g, i: (b, g))
    kern = functools.partial(_nsa_attn_kernel, r_heads=R, n_pick=min(SEL_TOPN, n_sel))
    return pl.pallas_call(
        kern,
        grid=(batch, G, nq),
        in_specs=[pl.BlockSpec((TQ, W), lambda b, g, i: (b * nq + i, g)),
                  cmp_spec, cmp_spec, kv(blks["ks"]), kv(blks["vs"]), kv(blks["kw"]), kv(blks["vw"]),
                  pl.BlockSpec((TQ, aux_g.shape[1]), lambda b, g, i: (b * nq + i, 0)),
                  pl.BlockSpec((1, LANES), lambda b, g, i: (0, 0)),
                  pl.BlockSpec((STRIP_ROWS, W), lambda b, g, i: (0, g)),
                  pl.BlockSpec((1, W), lambda b, g, i: (0, g)),
                  pl.BlockSpec((NEAR_CMP, W), lambda b, g, i: (0, g)),
                  pl.BlockSpec((n_sel, ncp), lambda b, g, i: (0, 0))],
        out_specs=pl.BlockSpec((TQ, W), lambda b, g, i: (b * nq + i, g)),
        out_shape=jax.ShapeDtypeStruct((batch * seq, N_HEADS * HEAD_DIM), BF16),
        scratch_shapes=[pltpu.VMEM((W, HEAD_DIM), BF16),
                        pltpu.VMEM((NEAR_CMP - CPQ + ncp, W), F32),
                        pltpu.VMEM((n_sel, TQ), F32),
                        pltpu.VMEM((LANES, TQ), F32),
                        pltpu.VMEM((1, W), F32), pltpu.VMEM((1, W), F32), pltpu.VMEM((HEAD_DIM, W), F32),
                        pltpu.VMEM((HEAD_DIM, W), F32), pltpu.VMEM((HEAD_DIM, W), F32),
                        pltpu.SMEM((2,), F32)],
        compiler_params=_cparams(("arbitrary", "arbitrary", "arbitrary")),
        name="nsa_attn",
    )(proj, k_cmp, v_cmp, proj, proj, proj, proj, aux_g, gb, strip, strip_amax, ustrip, cover)


def _nsa_mixer(xf, mod_all, tab, gain, w_in, gate_b, w_out, q_gain, k_gain, cmp_pos, cmp_w1, cmp_w2, strip, strip_amax,
               ustrip, *, batch, seq):
    AW, KW = N_HEADS * HEAD_DIM, N_KV_HEADS * HEAD_DIM
    tn = KW
    wp = _proj_layout(w_in, [(0, AW), (AW + 2 * KW, KW), (AW + 3 * KW, KW), (AW + 4 * KW, KW), (AW + 5 * KW, KW),
                             (AW, KW), (AW + KW, KW), (AW + 6 * KW, 3 * N_HEADS)], tn)
    nq_t = AW // tn
    gain_cols = jnp.concatenate([jnp.tile(q_gain * (HEAD_DIM ** -0.5 * LOG2E), N_HEADS), jnp.tile(k_gain[1], N_KV_HEADS),
                                 jnp.zeros((tn,), F32), jnp.tile(k_gain[2], N_KV_HEADS),
                                 jnp.zeros((4 * tn,), F32)]).reshape(1, -1)
    proj, kc, vc, aux_g = _norm_matmul(xf, mod_all, tab, gain, wp, mod_row=3, seq=seq, tn=tn, gain_cols=gain_cols,
                                       norm_tiles=tuple(range(nq_t)) + (nq_t, nq_t + 2), n_aux=3)
    k_cmp, v_cmp = _nsa_compress(kc, vc, cmp_pos, cmp_w1, cmp_w2, k_gain[0], batch=batch, seq=seq)
    per = tn // HEAD_DIM
    blks = {"ks": nq_t * per, "vs": (nq_t + 1) * per, "kw": (nq_t + 2) * per, "vw": (nq_t + 3) * per}
    o = _nsa_attn(proj, aux_g, gate_b, k_cmp, v_cmp, strip, strip_amax, ustrip, batch=batch, seq=seq, blks=blks)
    return _matmul_res(o, w_out.astype(BF16), xf, mod_all, tab, mod_row=5, seq=seq)


def _rglru_kernel(gbr_ref, xr_ref, cw_ref, cb_ref, gw_ref, gbias_ref, lam_ref, o_ref, h_scr, tail_scr, *, ts, bd):
    @pl.when(pl.program_id(2) == 0)
    def _():
        h_scr[...] = jnp.zeros(h_scr.shape, F32)
        tail_scr[...] = jnp.zeros(tail_scr.shape, F32)

    x = xr_ref[...]
    cbw = x.shape[1]
    xfull = jnp.concatenate([tail_scr[...], x], axis=0)
    xc = cw_ref[CONV_WIDTH - 1:CONV_WIDTH, :] * x + cb_ref[...]
    for w in range(1, CONV_WIDTH):
        xc = xc + cw_ref[CONV_WIDTH - 1 - w:CONV_WIDTH - w, :] * pltpu.roll(xfull, w, 0)[SUBLANES:, :]
    tail_scr[...] = x[ts - SUBLANES:ts, :]

    gr = []
    for gi in range(2):
        parts = [jnp.dot(xc[:, n * bd:(n + 1) * bd].astype(BF16), gw_ref[gi, n], preferred_element_type=F32)
                 for n in range(cbw // bd)]
        gr.append(jnp.concatenate(parts, axis=1) + gbias_ref[gi:gi + 1, :])
    r = jax.nn.sigmoid(gr[0])
    i_g = jax.nn.sigmoid(gr[1])
    nl = -lam_ref[...]
    softplus = jnp.maximum(nl, 0.0) + jnp.log1p(jnp.exp(-jnp.abs(nl)))
    log_a = -RG_C * r * softplus
    a = jnp.exp(log_a)
    u = jnp.sqrt(1.0 - jnp.exp(2.0 * log_a)) * (i_g * xc)

    row = lax.broadcasted_iota(I32, (ts, cbw), 0)
    sft = 1
    while sft < ts:
        keep = row >= sft
        a_sh = jnp.where(keep, pltpu.roll(a, sft, 0), 1.0)
        u_sh = jnp.where(keep, pltpu.roll(u, sft, 0), 0.0)
        u = u + a * u_sh
        a = a * a_sh
        sft *= 2
    hs = u + a * h_scr[...]
    h_scr[...] = hs[ts - 1:ts, :]
    o_ref[...] = (_gelu_tanh(gbr_ref[...]) * hs).astype(o_ref.dtype)


def _rglru_mixer(xf, mod_all, tab, gain, w_in, conv_w, conv_b, gate_w, gate_b, lam, w_out, *, batch, seq):
    d_rnn = conv_w.shape[1]
    bd = d_rnn // RG_BLOCKS
    tn = min(512, d_rnn)
    proj = _norm_matmul(xf, mod_all, tab, gain, w_in.astype(BF16), mod_row=3, seq=seq, tn=tn, out_dtype=F32)
    cbw = min(512, d_rnn)
    ncb = d_rnn // cbw
    ts = min(256, seq)
    nts = seq // ts
    kern = functools.partial(_rglru_kernel, ts=ts, bd=bd)
    y = pl.pallas_call(
        kern,
        grid=(batch, ncb, nts),
        in_specs=[pl.BlockSpec((ts, cbw), lambda b, c, t: (b * nts + t, c)),
                  pl.BlockSpec((ts, cbw), lambda b, c, t: (b * nts + t, ncb + c)),
                  pl.BlockSpec((CONV_WIDTH, cbw), lambda b, c, t: (0, c)),
                  pl.BlockSpec((1, cbw), lambda b, c, t: (0, c)),
                  pl.BlockSpec((2, cbw // bd, bd, bd), lambda b, c, t: (0, c, 0, 0)),
                  pl.BlockSpec((2, cbw), lambda b, c, t: (0, c)),
                  pl.BlockSpec((1, cbw), lambda b, c, t: (0, c))],
        out_specs=pl.BlockSpec((ts, cbw), lambda b, c, t: (b * nts + t, c)),
        out_shape=jax.ShapeDtypeStruct((batch * seq, d_rnn), BF16),
        scratch_shapes=[pltpu.VMEM((1, cbw), F32), pltpu.VMEM((SUBLANES, cbw), F32)],
        compiler_params=_cparams(("arbitrary", "arbitrary", "arbitrary")),
        name="rglru",
    )(proj, proj, conv_w, conv_b.reshape(1, d_rnn), gate_w.astype(BF16), gate_b, lam.reshape(1, d_rnn))
    return _matmul_res(y, w_out.astype(BF16), xf, mod_all, tab, mod_row=5, seq=seq)


def kernel(x, c, rel_bias, ada_w, ada_b, ada_table, norm_g, ffn_w_in, ffn_w_out, dsa_w_in, dsa_w_out, dsa_q_gain, dsa_k_gain, nsa_w_in, nsa_gate_b, nsa_w_out, nsa_q_gain, nsa_k_gain, nsa_cmp_pos, nsa_cmp_w1, nsa_cmp_w2, rg_w_in, rg_conv_w, rg_conv_b, rg_gate_w, rg_gate_b, rg_lambda, rg_w_out):
    B, S, D = x.shape
    depth = ada_table.shape[0]
    assert S % TK == 0 and S >= NEAR_TILES * TQ
    mod_all = _mod_all(c, ada_w, ada_b)
    strip, strip_amax, ustrip = _bias_strips(rel_bias)
    w1_all, w2_all, tf = _prep_ffn_weights(ffn_w_in, ffn_w_out)
    xf = x.reshape(B * S, D)
    for layer in range(depth):
        tab = ada_table[layer]
        xf = _ffn(xf, mod_all, tab, norm_g[layer, 0], w1_all, w2_all, tf, 2 * layer, mod_row=0, seq=S)
        kind, j = layer % 3, layer // 3
        if kind == 0:
            xf = _dsa_mixer(xf, mod_all, tab, norm_g[layer, 1], dsa_w_in[j], dsa_w_out[j], dsa_q_gain[j],
                            dsa_k_gain[j], strip, strip_amax, batch=B, seq=S)
        elif kind == 1:
            xf = _nsa_mixer(xf, mod_all, tab, norm_g[layer, 1], nsa_w_in[j], nsa_gate_b[j], nsa_w_out[j],
                            nsa_q_gain[j], nsa_k_gain[j], nsa_cmp_pos[j], nsa_cmp_w1[j], nsa_cmp_w2[j],
                            strip, strip_amax, ustrip, batch=B, seq=S)
        else:
            xf = _rglru_mixer(xf, mod_all, tab, norm_g[layer, 1], rg_w_in[j], rg_conv_w[j], rg_conv_b[j],
                              rg_gate_w[j], rg_gate_b[j], rg_lambda[j], rg_w_out[j], batch=B, seq=S)
        xf = _ffn(xf, mod_all, tab, norm_g[layer, 2], w1_all, w2_all, tf, 2 * layer + 1, mod_row=6, seq=S)
    return xf.reshape(B, S, D)
```

```python
import functools
import math

import numpy as np
import jax
import jax.numpy as jnp
from jax import lax
from jax.experimental import pallas as pl
from jax.experimental.pallas import tpu as pltpu

N_HEADS = 32
HEAD_DIM = 128
N_KV_HEADS = 4
FFN_RES = 0.5
N_MOD = 9
NUM_BUCKETS = 32
MAX_DISTANCE = 1024
EPS = 1e-6
NEG = -1e30
BIG = 1e30
IDX_HEADS = 16
IDX_DIM = 128
TOPK_KEYS = 256
CMP_BLOCK = 32
CMP_STRIDE = 16
SEL_BLOCK = 64
SEL_TOPN = 16
WINDOW = 512
RG_BLOCKS = 16
CONV_WIDTH = 4
RG_C = 8.0

LANES = 128
SUBLANES = 8
VMEM_LIMIT_BYTES = 56 * 1024 * 1024

F32 = jnp.float32
BF16 = jnp.bfloat16
I32 = jnp.int32
INT_MIN = -2 ** 31
LOWEST = -3.0e38

TQ = 128
TK = 512
KPT = TK // TQ
SAFE_LOGIT = 60.0
LOG2E = 1.4426950408889634
FAR_DIST = int(math.ceil((NUM_BUCKETS // 2) * (MAX_DISTANCE / (NUM_BUCKETS // 2)) ** ((NUM_BUCKETS // 2 - 1) / (NUM_BUCKETS // 2)))) + 8
NEAR_TILES = -(-(FAR_DIST + TQ - 1) // TQ)
CPQ = TQ // CMP_STRIDE
CMP_LAST = CMP_STRIDE * (CPQ - 1) + CMP_BLOCK - 1
NEAR_CMP = -(-(-(-(FAR_DIST + CMP_LAST) // CMP_STRIDE)) // SUBLANES) * SUBLANES

STRIP_PAD = TK - TQ
STRIP_ROWS = NEAR_TILES * TQ + 2 * STRIP_PAD
NT_DIMS = (((1,), (1,)), ((), ()))
TN_DIMS = (((0,), (0,)), ((), ()))


def _round_up(n, m):
    return (n + m - 1) // m * m


def _cparams(sem):
    return pltpu.CompilerParams(dimension_semantics=sem, vmem_limit_bytes=VMEM_LIMIT_BYTES)


def _gelu_tanh(x):
    return 0.5 * x * (1.0 + jnp.tanh(0.7978845608028654 * (x + 0.044715 * x * x * x)))


def _mod_kernel(c_ref, w_ref, b_ref, o_ref):
    c = c_ref[...]
    s = c * jax.nn.sigmoid(c)
    o_ref[...] = jnp.dot(s.astype(BF16), w_ref[...].astype(BF16), preferred_element_type=F32) + b_ref[...]


def _mod_all(c, ada_w, ada_b):
    B, D = c.shape
    N = ada_w.shape[1]
    rows = 16
    tn = next(t for t in (512, 256, 128) if N % t == 0)
    cp = jnp.zeros((rows, D), F32).at[:B].set(c)
    out = pl.pallas_call(
        _mod_kernel,
        grid=(N // tn,),
        in_specs=[pl.BlockSpec((rows, D), lambda j: (0, 0)),
                  pl.BlockSpec((D, tn), lambda j: (0, j)),
                  pl.BlockSpec((1, tn), lambda j: (0, j))],
        out_specs=pl.BlockSpec((rows, tn), lambda j: (0, j)),
        out_shape=jax.ShapeDtypeStruct((rows, N), F32),
        compiler_params=_cparams(("arbitrary",)),
        name="ada_mod",
    )(cp, ada_w, ada_b.reshape(1, N))
    return out[:B].reshape(B, N_MOD, D)


def _ada_norm_block(x, g, mod, tab, k):
    var = jnp.mean(x * x, axis=-1, keepdims=True)
    y = x * lax.rsqrt(var + EPS) * g
    shift = mod[k:k + 1, :] + tab[k:k + 1, :]
    scale = mod[k + 1:k + 2, :] + tab[k + 1:k + 2, :]
    return y * (1.0 + scale) + shift


def _norm_matmul_kernel(x_ref, mod_ref, tab_ref, g_ref, w_ref, gc_ref, *refs, mod_row, epilogue,
                        norm_tiles, n_aux, n_tiles):
    if epilogue == "swiglu":
        wu_ref, refs = refs[0], refs[1:]
    o_ref = refs[0]
    aux_refs = refs[1:1 + n_aux]
    y_scr = refs[1 + n_aux]
    j = pl.program_id(1)

    @pl.when(j == 0)
    def _():
        y = _ada_norm_block(x_ref[...], g_ref[...], mod_ref[0], tab_ref[...], mod_row)
        y_scr[...] = y.astype(BF16)

    h = jnp.dot(y_scr[...], w_ref[...], preferred_element_type=F32)
    if epilogue == "swiglu":
        u = jnp.dot(y_scr[...], wu_ref[...], preferred_element_type=F32)
        o_ref[...] = (h * jax.nn.sigmoid(h) * u).astype(o_ref.dtype)
        return
    if norm_tiles:
        is_norm = functools.reduce(lambda a, b: a | b, [j == t for t in norm_tiles])

        @pl.when(is_norm)
        def _():
            for c in range(h.shape[1] // LANES):
                hc = h[:, c * LANES:(c + 1) * LANES]
                ms = jnp.mean(hc * hc, axis=-1, keepdims=True)
                yc = hc * lax.rsqrt(ms + EPS) * gc_ref[:, c * LANES:(c + 1) * LANES]
                o_ref[:, c * LANES:(c + 1) * LANES] = yc.astype(o_ref.dtype)

        @pl.when(jnp.logical_not(is_norm))
        def _():
            o_ref[...] = h.astype(o_ref.dtype)
    else:
        o_ref[...] = h.astype(o_ref.dtype)
    for a in range(n_aux):
        @pl.when(j == n_tiles - n_aux + a)
        def _(a=a):
            aux_refs[a][...] = h


def _weight_spec(w, rows, cols, w_index, col_block):
    if w.ndim == 2:
        return pl.BlockSpec((rows, cols), lambda i, j: (0, col_block(j)))
    return pl.BlockSpec((None, rows, cols), lambda i, j: (w_index, 0, col_block(j)))


def _norm_matmul(x, mod_all, tab, gain, w, *, mod_row, seq, epilogue="plain", tn, tm=512,
                 out_dtype=BF16, gain_cols=None, norm_tiles=(), n_aux=0, w_index=0):
    T, D = x.shape
    swiglu = epilogue == "swiglu"
    n_tiles = w.shape[-1] // (2 * tn if swiglu else tn)
    n_out = n_tiles * tn
    tm = min(tm, seq)
    bpb = seq // tm
    if gain_cols is None:
        gain_cols = jnp.zeros((1, n_out), F32)
    out_shape = [jax.ShapeDtypeStruct((T, n_out), out_dtype)]
    out_specs = [pl.BlockSpec((tm, tn), lambda i, j: (i, j))]
    for _ in range(n_aux):
        out_shape.append(jax.ShapeDtypeStruct((T, tn), F32))
        out_specs.append(pl.BlockSpec((tm, tn), lambda i, j: (i, 0)))
    kern = functools.partial(_norm_matmul_kernel, mod_row=mod_row, epilogue=epilogue,
                             norm_tiles=tuple(norm_tiles), n_aux=n_aux, n_tiles=n_tiles)
    in_specs = [pl.BlockSpec((tm, D), lambda i, j: (i, 0)),
                pl.BlockSpec((1, N_MOD, D), lambda i, j: (i // bpb, 0, 0)),
                pl.BlockSpec((N_MOD, D), lambda i, j: (0, 0)),
                pl.BlockSpec((1, D), lambda i, j: (0, 0)),
                _weight_spec(w, D, tn, w_index, lambda j: j),
                pl.BlockSpec((1, tn), lambda i, j: (0, j))]
    args = [x, mod_all, tab, gain.reshape(1, D), w, gain_cols]
    if swiglu:
        in_specs.append(_weight_spec(w, D, tn, w_index, lambda j: n_tiles + j))
        args.append(w)
    res = pl.pallas_call(
        kern,
        grid=(T // tm, n_tiles),
        in_specs=in_specs,
        out_specs=out_specs,
        out_shape=out_shape,
        scratch_shapes=[pltpu.VMEM((tm, D), BF16)],
        compiler_params=_cparams(("arbitrary", "arbitrary")),
        name="norm_matmul_" + epilogue,
    )(*args)
    return res if n_aux else res[0]


def _matmul_res_kernel(a_ref, w_ref, x_ref, mod_ref, tab_ref, o_ref, *, mod_row, res_scale):
    h = jnp.dot(a_ref[...], w_ref[...], preferred_element_type=F32)
    gate = mod_ref[0, mod_row:mod_row + 1, :] + tab_ref[mod_row:mod_row + 1, :]
    if res_scale != 1.0:
        gate = res_scale * gate
    o_ref[...] = x_ref[...] + gate * h


def _matmul_res(a, w, x, mod_all, tab, *, mod_row, seq, res_scale=1.0, tm=512, tn=1024, w_index=0):
    T, K = a.shape
    D = w.shape[-1]
    tm = min(tm, seq)
    tn = min(tn, D)
    bpb = seq // tm
    kern = functools.partial(_matmul_res_kernel, mod_row=mod_row, res_scale=res_scale)
    return pl.pallas_call(
        kern,
        grid=(T // tm, D // tn),
        in_specs=[pl.BlockSpec((tm, K), lambda i, j: (i, 0)),
                  _weight_spec(w, K, tn, w_index, lambda j: j),
                  pl.BlockSpec((tm, tn), lambda i, j: (i, j)),
                  pl.BlockSpec((1, N_MOD, tn), lambda i, j: (i // bpb, 0, j)),
                  pl.BlockSpec((N_MOD, tn), lambda i, j: (0, j))],
        out_specs=pl.BlockSpec((tm, tn), lambda i, j: (i, j)),
        out_shape=jax.ShapeDtypeStruct((T, D), F32),
        compiler_params=_cparams(("arbitrary", "arbitrary")),
        name="matmul_residual",
    )(a, w, x, mod_all, tab)


def _ffn_tile(d_ff):
    return 512 if d_ff >= 512 else _round_up(d_ff, LANES)


def _pad_cols_kernel(x_ref, o_ref):
    f = x_ref.shape[1]
    o_ref[:, :f] = x_ref[...].astype(o_ref.dtype)
    if o_ref.shape[1] > f:
        o_ref[:, f:] = jnp.zeros((o_ref.shape[0], o_ref.shape[1] - f), o_ref.dtype)


def _pad_rows_kernel(x_ref, o_ref, *, rows_valid):
    tr = o_ref.shape[0]
    row = pl.program_id(1) * tr + lax.broadcasted_iota(I32, o_ref.shape, 0)
    o_ref[...] = jnp.where(row < rows_valid, x_ref[...], 0.0).astype(o_ref.dtype)


def _prep_ffn_weights(ffn_w_in, ffn_w_out):
    L, two, D, two_f = ffn_w_in.shape
    F = two_f // 2
    tf = _ffn_tile(F)
    Fp = _round_up(F, tf)
    rows = L * two * D
    tr = 256
    w1 = pl.pallas_call(
        _pad_cols_kernel,
        grid=(rows // tr, 2),
        in_specs=[pl.BlockSpec((tr, F), lambda r, h: (r, h))],
        out_specs=pl.BlockSpec((tr, Fp), lambda r, h: (r, h)),
        out_shape=jax.ShapeDtypeStruct((rows, 2 * Fp), BF16),
        compiler_params=_cparams(("arbitrary", "arbitrary")),
        name="ffn_w_in_prep",
    )(ffn_w_in.reshape(rows, two_f))
    w2 = pl.pallas_call(
        functools.partial(_pad_rows_kernel, rows_valid=F),
        grid=(L * two, Fp // tf),
        in_specs=[pl.BlockSpec((None, tf, D), lambda l, r: (l, r, 0))],
        out_specs=pl.BlockSpec((None, tf, D), lambda l, r: (l, r, 0)),
        out_shape=jax.ShapeDtypeStruct((L * two, Fp, D), BF16),
        compiler_params=_cparams(("arbitrary", "arbitrary")),
        name="ffn_w_out_prep",
    )(ffn_w_out.reshape(L * two, F, D))
    return w1.reshape(L * two, D, 2 * Fp), w2, tf


def _ffn(x, mod_all, tab, gain, w1, w2, tf, w_index, *, mod_row, seq):
    a = _norm_matmul(x, mod_all, tab, gain, w1, mod_row=mod_row, seq=seq, epilogue="swiglu", tn=tf, w_index=w_index)
    return _matmul_res(a, w2, x, mod_all, tab, mod_row=mod_row + 2, seq=seq, res_scale=FFN_RES, w_index=w_index)


def _t5_bucket(dist):
    n = jnp.maximum(dist, 0)
    max_exact = NUM_BUCKETS // 2
    nf = jnp.maximum(n, 1).astype(F32)
    large = max_exact + (jnp.log(nf / max_exact) / math.log(MAX_DISTANCE / max_exact)
                         * (NUM_BUCKETS - max_exact)).astype(I32)
    large = jnp.minimum(large, NUM_BUCKETS - 1)
    return jnp.where(n < max_exact, n, large)


def _bias_strip_kernel(rb_ref, bkt_ref, o_ref, amax_ref):
    h = pl.program_id(0)
    bk = bkt_ref[...]
    far = rb_ref[NUM_BUCKETS - 1, h]
    acc = jnp.zeros(bk.shape, F32)
    for k in range(NUM_BUCKETS - 1):
        acc = jnp.where(bk == k, LOG2E * (rb_ref[k, h] - far), acc)
    o_ref[...] = acc
    amax_ref[...] = jnp.max(jnp.abs(acc), axis=0, keepdims=True)


def _bias_strip(rel_bias, bkt):
    rows = bkt.shape[0]
    H = rel_bias.shape[1]
    return pl.pallas_call(
        _bias_strip_kernel,
        grid=(H,),
        in_specs=[pl.BlockSpec(memory_space=pltpu.SMEM),
                  pl.BlockSpec((rows, TQ), lambda h: (0, 0))],
        out_specs=[pl.BlockSpec((rows, TQ), lambda h: (0, h)), pl.BlockSpec((1, TQ), lambda h: (0, h))],
        out_shape=[jax.ShapeDtypeStruct((rows, H * TQ), F32), jax.ShapeDtypeStruct((1, H * TQ), F32)],
        compiler_params=_cparams(("arbitrary",)),
        name="bias_strip",
    )(rel_bias, bkt)


def _bias_strips(rel_bias):
    iq = jnp.arange(TQ, dtype=I32)[None, :]
    x = jnp.arange(-STRIP_PAD, NEAR_TILES * TQ + STRIP_PAD, dtype=I32)[:, None]
    bkt_tok = _t5_bucket(iq - x + (NEAR_TILES - 1) * TQ)
    y = jnp.arange(NEAR_CMP, dtype=I32)[:, None]
    bkt_cmp = _t5_bucket(iq + CMP_STRIDE * (NEAR_CMP - 1 - y) - CMP_LAST)
    strip, strip_amax = _bias_strip(rel_bias, bkt_tok)
    ustrip, _ = _bias_strip(rel_bias, bkt_cmp)
    return strip, strip_amax, ustrip


def _stack_heads(q_ref, qs_scr, n):
    for r in range(n):
        qs_scr[r * TQ:(r + 1) * TQ, :] = q_ref[:, r * LANES:(r + 1) * LANES]


def _num_far_tiles(i):
    return jnp.maximum(i - NEAR_TILES + 1, 0) // KPT


def _strip_row(i, kt):
    return pl.multiple_of((KPT + NEAR_TILES - 2 - (i - kt * KPT)) * TQ, TQ)


def _flash_init(m_scr, l_scr, acc_scr):
    m_scr[...] = jnp.full(m_scr.shape, NEG, F32)
    l_scr[...] = jnp.zeros(l_scr.shape, F32)
    acc_scr[...] = jnp.zeros(acc_scr.shape, F32)


def _max_row_sumsq(k_ref):
    rows = k_ref.shape[0]
    ch = min(rows, 1024)

    def body(c, best):
        kk = k_ref[pl.ds(pl.multiple_of(c * ch, ch), ch), :].astype(F32)
        return jnp.maximum(best, jnp.max(jnp.sum(kk * kk, axis=1, keepdims=True)))

    return lax.fori_loop(0, rows // ch, body, jnp.float32(0.0))


def _logits_bounded(qs_scr, k_sumsq, amax_row):
    qf = qs_scr[...].astype(F32)
    q_sumsq = lax.dot_general(jnp.ones((SUBLANES, HEAD_DIM), BF16), (qf * qf).astype(BF16), NT_DIMS,
                              preferred_element_type=F32)[0:1, :]
    bound = jnp.sqrt(q_sumsq * k_sumsq) * 1.02 + amax_row
    return jnp.max(bound) <= SAFE_LOGIT


def _flash_tile(k_tile, v_tile, qs_scr, madd, strip_ref, strip_row, m_scr, l_scr, acc_scr, n_heads, online):
    s = lax.dot_general(k_tile, qs_scr[...], NT_DIMS, preferred_element_type=F32) + jnp.tile(madd, (1, n_heads))
    if strip_ref is not None:
        s = s + strip_ref[pl.ds(strip_row, k_tile.shape[0]), :]
    if not online:
        p = jnp.exp2(s)
        l_scr[...] += jnp.sum(p, axis=0, keepdims=True)
        acc_scr[...] += lax.dot_general(v_tile, p.astype(BF16), TN_DIMS, preferred_element_type=F32)
        return
    m_prev = m_scr[...]
    m_new = jnp.maximum(m_prev, jnp.max(s, axis=0, keepdims=True))
    alpha = jnp.exp2(m_prev - m_new)
    p = jnp.exp2(s - m_new)
    l_scr[...] = alpha * l_scr[...] + jnp.sum(p, axis=0, keepdims=True)
    pv = lax.dot_general(v_tile, p.astype(BF16), TN_DIMS, preferred_element_type=F32)
    acc_scr[...] = alpha * acc_scr[...] + pv
    m_scr[...] = m_new


def _far_tiles(n_far, tile, online):
    if online:
        def body(kt, carry):
            tile(pl.multiple_of(kt * TK, TK), TK)
            return carry

        lax.fori_loop(0, n_far, body, 0)
        return

    def pair_body(u, carry):
        tile(pl.multiple_of(u * 2 * TK, 2 * TK), 2 * TK)
        return carry

    lax.fori_loop(0, n_far // 2, pair_body, 0)

    @pl.when(n_far % 2 == 1)
    def _():
        tile(pl.multiple_of((n_far - 1) * TK, TK), TK)


def _flash_sweep(bounded, sweep, m_scr, l_scr, acc_scr):
    _flash_init(m_scr, l_scr, acc_scr)

    @pl.when(bounded)
    def _():
        sweep(False)

    @pl.when(jnp.logical_not(bounded))
    def _():
        sweep(True)
        l_scr[...] = jnp.where(m_scr[...] > 0.5 * NEG, l_scr[...], 0.0)

    l = l_scr[...]
    return acc_scr[...] * jnp.where(l > 0.0, 1.0 / l, 0.0)


def _write_heads(o_ref, o_t, n):
    for r in range(n):
        o_ref[:, r * LANES:(r + 1) * LANES] = o_t[:, r * TQ:(r + 1) * TQ].T.astype(o_ref.dtype)


def _dsa_index_kernel(qi_ref, ki_ref, wi_ref, mask_ref, keys_scr, qis_scr, j_scr, *, n_keep, ih, seq):
    i = pl.program_id(1)
    nkt = i // KPT + 1
    _stack_heads(qi_ref, qis_scr, ih)
    w_t = wi_ref[...].T * (ih ** -0.5 * IDX_DIM ** -0.5)
    w_rows = [w_t[h:h + 1, :] for h in range(ih)]
    q_pos = i * TQ + lax.broadcasted_iota(I32, (TQ, TQ), 1)
    k_iota = lax.broadcasted_iota(I32, (TQ, TQ), 0)

    def sub_rows(kt, u):
        return pl.multiple_of(kt * TK + u * TQ, TQ)

    def score_tile(kt, carry):
        k0 = pl.multiple_of(kt * TK, TK)
        d = lax.dot_general(ki_ref[pl.ds(k0, TK), :], qis_scr[...], NT_DIMS, preferred_element_type=F32)
        for u in range(KPT):
            sc = jnp.zeros((TQ, TQ), F32)
            for h in range(ih):
                sc = sc + jnp.maximum(d[u * TQ:(u + 1) * TQ, h * TQ:(h + 1) * TQ], 0.0) * w_rows[h]
            bits = pltpu.bitcast(sc, I32)
            key = jnp.where(bits < 0, bits ^ 0x7FFFFFFF, bits)
            r0 = sub_rows(kt, u)
            keys_scr[pl.ds(r0, TQ), :] = jnp.where(r0 + k_iota <= q_pos, key, INT_MIN)
        return carry

    lax.fori_loop(0, nkt, score_tile, 0)

    def count(pred):
        def body(kt, acc):
            for u in range(KPT):
                r0 = sub_rows(kt, u)
                acc = acc + jnp.where(pred(keys_scr[pl.ds(r0, TQ), :], r0 + k_iota), 1.0, 0.0)
            return acc
        acc = lax.fori_loop(0, nkt, body, jnp.zeros((TQ, TQ), F32))
        return jnp.sum(acc, axis=0, keepdims=True)

    def bit_body(b, lo):
        cand = lo + jnp.left_shift(jnp.int32(1), 31 - b)
        cnt = count(lambda kk, idx: kk >= cand)
        return jnp.where(cnt >= n_keep, cand, lo)

    thr = lax.fori_loop(0, 32, bit_body, jnp.full((1, TQ), INT_MIN, I32))
    cnt_ge = count(lambda kk, idx: kk >= thr)
    has_tie = (cnt_ge > n_keep) & (thr > INT_MIN)
    j_scr[...] = jnp.full((1, TQ), seq, I32)

    @pl.when(jnp.max(jnp.where(has_tie, 1.0, 0.0)) > 0.5)
    def _():
        need = n_keep - count(lambda kk, idx: kk > thr)

        def jb(b, lo):
            cand = lo + jnp.left_shift(jnp.int32(1), (seq.bit_length() - 2) - b)
            c = count(lambda kk, idx: (kk == thr) & (idx < cand))
            return jnp.where(c < need, cand, lo)

        j_last = lax.fori_loop(0, seq.bit_length() - 1, jb, jnp.zeros((1, TQ), I32))
        j_scr[...] = jnp.where(has_tie, j_last, seq)

    thr_c = jnp.maximum(thr, INT_MIN + 1)
    j_last = j_scr[...]

    def write_tile(kt, carry):
        for u in range(KPT):
            r0 = sub_rows(kt, u)
            kk = keys_scr[pl.ds(r0, TQ), :]
            sel = (kk > thr_c) | ((kk == thr_c) & (r0 + k_iota <= j_last))
            mask_ref[pl.ds(r0, TQ), :] = jnp.where(sel, 0.0, NEG).astype(BF16)
        return carry

    lax.fori_loop(0, nkt, write_tile, 0)

    def fill_tile(kt, carry):
        k0 = pl.multiple_of(kt * TK, TK)
        mask_ref[pl.ds(k0, TK), :] = jnp.full((TK, TQ), NEG, BF16)
        return carry

    lax.fori_loop(nkt, seq // TK, fill_tile, 0)


def _dsa_index(proj, aux, *, batch, seq, qi_blk, ki_blk, n_keep):
    nq = seq // TQ
    ihw = IDX_HEADS * IDX_DIM
    kern = functools.partial(_dsa_index_kernel, n_keep=n_keep, ih=IDX_HEADS, seq=seq)
    return pl.pallas_call(
        kern,
        grid=(batch, nq),
        in_specs=[pl.BlockSpec((TQ, ihw), lambda b, i: (b * nq + i, qi_blk)),
                  pl.BlockSpec((seq, IDX_DIM), lambda b, i: (b, ki_blk)),
                  pl.BlockSpec((TQ, LANES), lambda b, i: (b * nq + i, 1))],
        out_specs=pl.BlockSpec((None, seq, TQ), lambda b, i: (b, 0, i)),
        out_shape=jax.ShapeDtypeStruct((batch, seq, seq), BF16),
        scratch_shapes=[pltpu.VMEM((seq, TQ), I32),
                        pltpu.VMEM((IDX_HEADS * TQ, IDX_DIM), BF16),
                        pltpu.VMEM((1, TQ), I32)],
        compiler_params=_cparams(("arbitrary", "arbitrary")),
        name="dsa_index",
    )(proj, proj, aux)


def _dsa_attn_kernel(q_ref, k_ref, v_ref, mask_ref, strip_ref, amax_ref, o_ref, qs_scr, m_scr, l_scr, acc_scr,
                     ksq_scr, *, r_heads):
    i = pl.program_id(2)

    @pl.when(i == 0)
    def _():
        ksq_scr[0] = _max_row_sumsq(k_ref)

    _stack_heads(q_ref, qs_scr, r_heads)
    n_far = _num_far_tiles(i)

    def sweep(online):
        def far_tile(k0, rows):
            _flash_tile(k_ref[pl.ds(k0, rows), :], v_ref[pl.ds(k0, rows), :], qs_scr,
                        mask_ref[pl.ds(k0, rows), :].astype(F32), None, None, m_scr, l_scr, acc_scr, r_heads, online)

        def near_body(kt, carry):
            k0 = pl.multiple_of(kt * TK, TK)
            _flash_tile(k_ref[pl.ds(k0, TK), :], v_ref[pl.ds(k0, TK), :], qs_scr,
                        mask_ref[pl.ds(k0, TK), :].astype(F32), strip_ref, _strip_row(i, kt),
                        m_scr, l_scr, acc_scr, r_heads, online)
            return carry

        _far_tiles(n_far, far_tile, online)
        lax.fori_loop(n_far, i // KPT + 1, near_body, 0)

    bounded = _logits_bounded(qs_scr, ksq_scr[0], amax_ref[...])
    _write_heads(o_ref, _flash_sweep(bounded, sweep, m_scr, l_scr, acc_scr), r_heads)


def _dsa_attn(proj, mask, strip, strip_amax, *, batch, seq, k_blk, v_blk):
    nq = seq // TQ
    G = N_KV_HEADS
    R = N_HEADS // G
    W = R * TQ
    kern = functools.partial(_dsa_attn_kernel, r_heads=R)
    return pl.pallas_call(
        kern,
        grid=(batch, G, nq),
        in_specs=[pl.BlockSpec((TQ, R * HEAD_DIM), lambda b, g, i: (b * nq + i, g)),
                  pl.BlockSpec((seq, HEAD_DIM), lambda b, g, i: (b, k_blk + g)),
                  pl.BlockSpec((seq, HEAD_DIM), lambda b, g, i: (b, v_blk + g)),
                  pl.BlockSpec((None, seq, TQ), lambda b, g, i: (b, 0, i)),
                  pl.BlockSpec((STRIP_ROWS, W), lambda b, g, i: (0, g)),
                  pl.BlockSpec((1, W), lambda b, g, i: (0, g))],
        out_specs=pl.BlockSpec((TQ, R * HEAD_DIM), lambda b, g, i: (b * nq + i, g)),
        out_shape=jax.ShapeDtypeStruct((batch * seq, N_HEADS * HEAD_DIM), BF16),
        scratch_shapes=[pltpu.VMEM((W, HEAD_DIM), BF16),
                        pltpu.VMEM((1, W), F32), pltpu.VMEM((1, W), F32), pltpu.VMEM((HEAD_DIM, W), F32),
                        pltpu.SMEM((1,), F32)],
        compiler_params=_cparams(("arbitrary", "arbitrary", "arbitrary")),
        name="dsa_attn",
    )(proj, proj, proj, mask, strip, strip_amax)


def _proj_layout(w_in, segments, tn):
    cols = []
    for start, width in segments:
        pad = _round_up(width, tn) - width
        cols.append(jnp.pad(w_in[:, start:start + width], ((0, 0), (0, pad))))
    return jnp.concatenate(cols, axis=1).astype(BF16)


def _dsa_mixer(xf, mod_all, tab, gain, w_in, w_out, q_gain, k_gain, strip, strip_amax, *, batch, seq):
    AW, KW = N_HEADS * HEAD_DIM, N_KV_HEADS * HEAD_DIM
    IW = IDX_HEADS * IDX_DIM
    tn = KW
    wp = _proj_layout(w_in, [(0, AW), (AW + 2 * KW, IW), (AW, KW), (AW + KW, KW), (AW + 2 * KW + IW, IDX_DIM + IDX_HEADS)], tn)
    nq_t, ni_t = AW // tn, IW // tn
    gain_cols = jnp.concatenate([jnp.tile(q_gain * (HEAD_DIM ** -0.5 * LOG2E), N_HEADS), jnp.zeros((IW,), F32),
                                 jnp.tile(k_gain, N_KV_HEADS), jnp.zeros((2 * tn,), F32)]).reshape(1, -1)
    proj, aux = _norm_matmul(xf, mod_all, tab, gain, wp, mod_row=3, seq=seq, tn=tn, gain_cols=gain_cols,
                             norm_tiles=tuple(range(nq_t)) + (nq_t + ni_t,), n_aux=1)
    k_blk = (nq_t + ni_t) * tn // HEAD_DIM
    mask = _dsa_index(proj, aux, batch=batch, seq=seq, qi_blk=AW // IW, ki_blk=(nq_t + ni_t + 2) * tn // IDX_DIM,
                      n_keep=min(TOPK_KEYS, seq // 4))
    o = _dsa_attn(proj, mask, strip, strip_amax, batch=batch, seq=seq, k_blk=k_blk, v_blk=k_blk + tn // HEAD_DIM)
    return _matmul_res(o, w_out.astype(BF16), xf, mod_all, tab, mod_row=5, seq=seq)


def _nsa_compress_kernel(xk_ref, xkn_ref, xv_ref, xvn_ref, pos_ref, w1_ref, w2_ref, gain_ref, ok_ref, ov_ref, *, groups, kw):
    half = CMP_BLOCK // 2
    rows = xk_ref.shape[0]
    row_id = lax.broadcasted_iota(I32, (rows, HEAD_DIM), 0)

    def branch(br, x_ref, xn_ref, o_ref):
        for g in range(groups):
            p1 = jnp.zeros((rows, HEAD_DIM), F32)
            p2 = jnp.zeros((rows, HEAD_DIM), F32)
            p2n = jnp.zeros((xn_ref.shape[0], HEAD_DIM), F32)
            for r in range(half):
                c0 = r * kw + g * HEAD_DIM
                xa = x_ref[:, c0:c0 + HEAD_DIM]
                w_lo = w1_ref[br, r * HEAD_DIM:(r + 1) * HEAD_DIM, :]
                w_hi = w1_ref[br, (half + r) * HEAD_DIM:(half + r + 1) * HEAD_DIM, :]
                p1 = p1 + jnp.dot((xa + pos_ref[br, r:r + 1, :]).astype(BF16), w_lo, preferred_element_type=F32)
                p2 = p2 + jnp.dot((xa + pos_ref[br, half + r:half + r + 1, :]).astype(BF16), w_hi,
                                  preferred_element_type=F32)
                xb = xn_ref[:, c0:c0 + HEAD_DIM]
                p2n = p2n + jnp.dot((xb + pos_ref[br, half + r:half + r + 1, :]).astype(BF16), w_hi,
                                    preferred_element_type=F32)
            p2s = jnp.where(row_id == rows - 1, p2n[0:1, :], pltpu.roll(p2, rows - 1, 0))
            hid = _gelu_tanh(p1 + p2s)
            out = jnp.dot(hid.astype(BF16), w2_ref[br], preferred_element_type=F32)
            if br == 0:
                ms = jnp.mean(out * out, axis=-1, keepdims=True)
                out = out * lax.rsqrt(ms + EPS) * gain_ref[...]
            o_ref[:, g * HEAD_DIM:(g + 1) * HEAD_DIM] = out.astype(o_ref.dtype)

    branch(0, xk_ref, xkn_ref, ok_ref)
    branch(1, xv_ref, xvn_ref, ov_ref)


def _nsa_compress(kc, vc, cmp_pos, cmp_w1, cmp_w2, k_gain0, *, batch, seq):
    assert CMP_BLOCK == 2 * CMP_STRIDE
    KW = N_KV_HEADS * HEAD_DIM
    ncp = seq // CMP_STRIDE
    rb = min(TQ, ncp)
    nrb = ncp // rb
    nxt = 16
    wide = CMP_STRIDE * KW
    xk = kc.reshape(batch * ncp, wide)
    xv = vc.reshape(batch * ncp, wide)
    last_nxt = batch * ncp // nxt - 1
    main = pl.BlockSpec((rb, wide), lambda b, r: (b * nrb + r, 0))
    ahead = pl.BlockSpec((nxt, wide), lambda b, r: (jnp.minimum((b * nrb + r + 1) * (rb // nxt), last_nxt), 0))
    kern = functools.partial(_nsa_compress_kernel, groups=N_KV_HEADS, kw=KW)
    return pl.pallas_call(
        kern,
        grid=(batch, nrb),
        in_specs=[main, ahead, main, ahead,
                  pl.BlockSpec((2, CMP_BLOCK, HEAD_DIM), lambda b, r: (0, 0, 0)),
                  pl.BlockSpec((2, CMP_BLOCK * HEAD_DIM, HEAD_DIM), lambda b, r: (0, 0, 0)),
                  pl.BlockSpec((2, HEAD_DIM, HEAD_DIM), lambda b, r: (0, 0, 0)),
                  pl.BlockSpec((1, HEAD_DIM), lambda b, r: (0, 0))],
        out_specs=[pl.BlockSpec((rb, KW), lambda b, r: (b * nrb + r, 0))] * 2,
        out_shape=[jax.ShapeDtypeStruct((batch * ncp, KW), BF16)] * 2,
        compiler_params=_cparams(("arbitrary", "arbitrary")),
        name="nsa_compress",
    )(xk, xk, xv, xv, cmp_pos, cmp_w1.astype(BF16), cmp_w2.astype(BF16), k_gain0.reshape(1, HEAD_DIM))


def _nsa_attn_kernel(q_ref, kc_ref, vc_ref, ks_ref, vs_ref, kw_ref, vw_ref, g_ref, gb_ref, strip_ref, amax_ref,
                     ustrip_ref, cover_ref, o_ref, qs_scr, lc_scr, sel_scr, gt_scr, m_scr, l_scr, acc_scr, ocmp_scr,
                     osel_scr, ksq_scr, *, r_heads, n_pick):
    g = pl.program_id(1)
    i = pl.program_id(2)
    R = r_heads

    @pl.when(i == 0)
    def _():
        ksq_scr[0] = _max_row_sumsq(ks_ref)
        ksq_scr[1] = _max_row_sumsq(kw_ref)

    ncp = kc_ref.shape[0]
    n_sel = cover_ref.shape[0]
    pad = NEAR_CMP - CPQ
    _stack_heads(q_ref, qs_scr, R)
    dk_iota = lax.broadcasted_iota(I32, (TK, TQ), 0) - lax.broadcasted_iota(I32, (TK, TQ), 1)

    lc_scr[0:pad, :] = jnp.zeros((pad, R * TQ), F32)
    lc_scr[pad:pad + ncp, :] = lax.dot_general(kc_ref[...], qs_scr[...], NT_DIMS, preferred_element_type=F32)
    off = pl.multiple_of(i * CPQ, SUBLANES)
    lc_scr[pl.ds(off, NEAR_CMP), :] += ustrip_ref[...]
    c_iota = lax.broadcasted_iota(I32, (ncp, TQ), 0)
    t_pos = i * TQ + lax.broadcasted_iota(I32, (ncp, TQ), 1)
    vis = jnp.where(c_iota * CMP_STRIDE + (CMP_BLOCK - 1) <= t_pos, 0.0, NEG)
    s = lc_scr[pad:pad + ncp, :] + jnp.tile(vis, (1, R))
    m = jnp.max(s, axis=0, keepdims=True)
    p = jnp.exp2(s - m)
    inv = jnp.where(m > 0.5 * NEG, 1.0 / jnp.sum(p, axis=0, keepdims=True), 0.0)
    pc = p * inv
    ocmp_scr[...] = lax.dot_general(vc_ref[...], pc.astype(BF16), TN_DIMS, preferred_element_type=F32)

    psum = pc[:, 0:TQ]
    for r in range(1, R):
        psum = psum + pc[:, r * TQ:(r + 1) * TQ]
    hi = psum.astype(BF16)
    lo = (psum - hi.astype(F32)).astype(BF16)
    imp = (jnp.dot(cover_ref[...], hi, preferred_element_type=F32)
           + jnp.dot(cover_ref[...], lo, preferred_element_type=F32))
    n_io = lax.broadcasted_iota(I32, (n_sel, TQ), 0)
    t_sel = i * TQ + lax.broadcasted_iota(I32, (n_sel, TQ), 1)
    cur = jnp.right_shift(t_sel, SEL_BLOCK.bit_length() - 1)
    forced = (n_io == 0) | (n_io == cur) | (n_io == cur - 1)
    val0 = jnp.where(forced, BIG, jnp.where(n_io * SEL_BLOCK <= t_sel, imp, NEG))
    n_f = n_io.astype(F32)

    def pick_round(_, carry):
        val, selm = carry
        cm = jnp.max(val, axis=0, keepdims=True)
        idx = jnp.min(jnp.where(val == cm, n_f, 1e9), axis=0, keepdims=True)
        pick = n_f == idx
        return jnp.where(pick, LOWEST, val), jnp.where(pick, 0.0, selm)

    _, selm = lax.fori_loop(0, n_pick, pick_round, (val0, jnp.full((n_sel, TQ), NEG, F32)))
    sel_scr[...] = selm

    bpt = TK // SEL_BLOCK
    n_far = _num_far_tiles(i)
    kt_last = i // KPT

    def sel_sweep(online):
        def block_mask(k0, rows):
            b0 = k0 // SEL_BLOCK
            return jnp.concatenate([jnp.broadcast_to(sel_scr[pl.ds(b0 + b, 1), :], (SEL_BLOCK, TQ))
                                    for b in range(rows // SEL_BLOCK)], axis=0)

        def far_tile(k0, rows):
            _flash_tile(ks_ref[pl.ds(k0, rows), :], vs_ref[pl.ds(k0, rows), :], qs_scr, block_mask(k0, rows),
                        None, None, m_scr, l_scr, acc_scr, R, online)

        def near_body(kt, carry):
            k0 = pl.multiple_of(kt * TK, TK)
            bm = jnp.where(dk_iota <= (i - kt * KPT) * TQ, block_mask(k0, TK), NEG)
            _flash_tile(ks_ref[pl.ds(k0, TK), :], vs_ref[pl.ds(k0, TK), :], qs_scr, bm,
                        strip_ref, _strip_row(i, kt), m_scr, l_scr, acc_scr, R, online)
            return carry

        _far_tiles(n_far, far_tile, online)
        lax.fori_loop(n_far, kt_last + 1, near_body, 0)

    sel_bounded = _logits_bounded(qs_scr, ksq_scr[0], amax_ref[...])
    osel_scr[...] = _flash_sweep(sel_bounded, sel_sweep, m_scr, l_scr, acc_scr)

    def win_sweep(online):
        def win_body(kt, carry):
            k0 = pl.multiple_of(kt * TK, TK)
            dist = (i - kt * KPT) * TQ - dk_iota
            vis_w = jnp.where((dist >= 0) & (dist < WINDOW), 0.0, NEG)
            _flash_tile(kw_ref[pl.ds(k0, TK), :], vw_ref[pl.ds(k0, TK), :], qs_scr, vis_w,
                        strip_ref, _strip_row(i, kt), m_scr, l_scr, acc_scr, R, online)
            return carry

        lax.fori_loop(jnp.maximum(kt_last - 1, 0), kt_last + 1, win_body, 0)

    win_bounded = _logits_bounded(qs_scr, ksq_scr[1], amax_ref[...])
    o_win = _flash_sweep(win_bounded, win_sweep, m_scr, l_scr, acc_scr)

    gt_scr[...] = jax.nn.sigmoid(g_ref[:, 0:LANES] + gb_ref[...]).T

    def gate_row(br):
        rows = gt_scr[pl.ds(pl.multiple_of(br * N_HEADS + g * R, SUBLANES), R), :]
        return jnp.concatenate([rows[r:r + 1, :] for r in range(R)], axis=1)

    o_t = gate_row(0) * ocmp_scr[...] + gate_row(1) * osel_scr[...] + gate_row(2) * o_win
    _write_heads(o_ref, o_t, R)


def _nsa_attn(proj, aux_g, gate_b, k_cmp, v_cmp, strip, strip_amax, ustrip, *, batch, seq, blks):
    nq = seq // TQ
    G = N_KV_HEADS
    R = N_HEADS // G
    assert R == SUBLANES and 3 * N_HEADS <= LANES and WINDOW <= TK and WINDOW <= (NEAR_TILES - 1) * TQ
    W = R * TQ
    ncp = seq // CMP_STRIDE
    n_cmp = (seq - CMP_BLOCK) // CMP_STRIDE + 1
    n_sel = seq // SEL_BLOCK
    cs = np.arange(ncp)[None, :] * CMP_STRIDE
    ss = np.arange(n_sel)[:, None] * SEL_BLOCK
    cover = ((cs < ss + SEL_BLOCK) & (cs + CMP_BLOCK - 1 >= ss) & (np.arange(ncp)[None, :] < n_cmp))
    cover = jnp.asarray(cover.astype(np.float32), dtype=BF16)
    gb = jnp.zeros((1, LANES), F32).at[0, :3 * N_HEADS].set(gate_b)
    kv = lambda blk: pl.BlockSpec((seq, HEAD_DIM), lambda b, g, i: (b, blk + g))
    cmp_spec = pl.BlockSpec((ncp, HEAD_DIM), lambda b, g, i: (b, g))
    kern = functools.partial(_nsa_attn_kernel, r_heads=R, n_pick=min(SEL_TOPN, n_sel))
    return pl.pallas_call(
        kern,
        grid=(batch, G, nq),
        in_specs=[pl.BlockSpec((TQ, W), lambda b, g, i: (b * nq + i, g)),
                  cmp_spec, cmp_spec, kv(blks["ks"]), kv(blks["vs"]), kv(blks["kw"]), kv(blks["vw"]),
                  pl.BlockSpec((TQ, aux_g.shape[1]), lambda b, g, i: (b * nq + i, 0)),
                  pl.BlockSpec((1, LANES), lambda b, g, i: (0, 0)),
                  pl.BlockSpec((STRIP_ROWS, W), lambda b, g, i: (0, g)),
                  pl.BlockSpec((1, W), lambda b, g, i: (0, g)),
                  pl.BlockSpec((NEAR_CMP, W), lambda b, g, i: (0, g)),
                  pl.BlockSpec((n_sel, ncp), lambda b, g, i: (0, 0))],
        out_specs=pl.BlockSpec((TQ, W), lambda b, g, i: (b * nq + i, g)),
        out_shape=jax.ShapeDtypeStruct((batch * seq, N_HEADS * HEAD_DIM), BF16),
        scratch_shapes=[pltpu.VMEM((W, HEAD_DIM), BF16),
                        pltpu.VMEM((NEAR_CMP - CPQ + ncp, W), F32),
                        pltpu.VMEM((n_sel, TQ), F32),
                        pltpu.VMEM((LANES, TQ), F32),
                        pltpu.VMEM((1, W), F32), pltpu.VMEM((1, W), F32), pltpu.VMEM((HEAD_DIM, W), F32),
                        pltpu.VMEM((HEAD_DIM, W), F32), pltpu.VMEM((HEAD_DIM, W), F32),
                        pltpu.SMEM((2,), F32)],
        compiler_params=_cparams(("arbitrary", "arbitrary", "arbitrary")),
        name="nsa_attn",
    )(proj, k_cmp, v_cmp, proj, proj, proj, proj, aux_g, gb, strip, strip_amax, ustrip, cover)


def _nsa_mixer(xf, mod_all, tab, gain, w_in, gate_b, w_out, q_gain, k_gain, cmp_pos, cmp_w1, cmp_w2, strip, strip_amax,
               ustrip, *, batch, seq):
    AW, KW = N_HEADS * HEAD_DIM, N_KV_HEADS * HEAD_DIM
    tn = KW
    wp = _proj_layout(w_in, [(0, AW), (AW + 2 * KW, KW), (AW + 3 * KW, KW), (AW + 4 * KW, KW), (AW + 5 * KW, KW),
                             (AW, KW), (AW + KW, KW), (AW + 6 * KW, 3 * N_HEADS)], tn)
    nq_t = AW // tn
    gain_cols = jnp.concatenate([jnp.tile(q_gain * (HEAD_DIM ** -0.5 * LOG2E), N_HEADS), jnp.tile(k_gain[1], N_KV_HEADS),
                                 jnp.zeros((tn,), F32), jnp.tile(k_gain[2], N_KV_HEADS),
                                 jnp.zeros((4 * tn,), F32)]).reshape(1, -1)
    proj, kc, vc, aux_g = _norm_matmul(xf, mod_all, tab, gain, wp, mod_row=3, seq=seq, tn=tn, gain_cols=gain_cols,
                                       norm_tiles=tuple(range(nq_t)) + (nq_t, nq_t + 2), n_aux=3)
    k_cmp, v_cmp = _nsa_compress(kc, vc, cmp_pos, cmp_w1, cmp_w2, k_gain[0], batch=batch, seq=seq)
    per = tn // HEAD_DIM
    blks = {"ks": nq_t * per, "vs": (nq_t + 1) * per, "kw": (nq_t + 2) * per, "vw": (nq_t + 3) * per}
    o = _nsa_attn(proj, aux_g, gate_b, k_cmp, v_cmp, strip, strip_amax, ustrip, batch=batch, seq=seq, blks=blks)
    return _matmul_res(o, w_out.astype(BF16), xf, mod_all, tab, mod_row=5, seq=seq)


def _rglru_kernel(gbr_ref, xr_ref, cw_ref, cb_ref, gw_ref, gbias_ref, lam_ref, o_ref, h_scr, tail_scr, *, ts, bd):
    @pl.when(pl.program_id(2) == 0)
    def _():
        h_scr[...] = jnp.zeros(h_scr.shape, F32)
        tail_scr[...] = jnp.zeros(tail_scr.shape, F32)

    x = xr_ref[...]
    cbw = x.shape[1]
    xfull = jnp.concatenate([tail_scr[...], x], axis=0)
    xc = cw_ref[CONV_WIDTH - 1:CONV_WIDTH, :] * x + cb_ref[...]
    for w in range(1, CONV_WIDTH):
        xc = xc + cw_ref[CONV_WIDTH - 1 - w:CONV_WIDTH - w, :] * pltpu.roll(xfull, w, 0)[SUBLANES:, :]
    tail_scr[...] = x[ts - SUBLANES:ts, :]

    gr = []
    for gi in range(2):
        parts = [jnp.dot(xc[:, n * bd:(n + 1) * bd].astype(BF16), gw_ref[gi, n], preferred_element_type=F32)
                 for n in range(cbw // bd)]
        gr.append(jnp.concatenate(parts, axis=1) + gbias_ref[gi:gi + 1, :])
    r = jax.nn.sigmoid(gr[0])
    i_g = jax.nn.sigmoid(gr[1])
    nl = -lam_ref[...]
    softplus = jnp.maximum(nl, 0.0) + jnp.log1p(jnp.exp(-jnp.abs(nl)))
    log_a = -RG_C * r * softplus
    a = jnp.exp(log_a)
    u = jnp.sqrt(1.0 - jnp.exp(2.0 * log_a)) * (i_g * xc)

    row = lax.broadcasted_iota(I32, (ts, cbw), 0)
    sft = 1
    while sft < ts:
        keep = row >= sft
        a_sh = jnp.where(keep, pltpu.roll(a, sft, 0), 1.0)
        u_sh = jnp.where(keep, pltpu.roll(u, sft, 0), 0.0)
        u = u + a * u_sh
        a = a * a_sh
        sft *= 2
    hs = u + a * h_scr[...]
    h_scr[...] = hs[ts - 1:ts, :]
    o_ref[...] = (_gelu_tanh(gbr_ref[...]) * hs).astype(o_ref.dtype)


def _rglru_mixer(xf, mod_all, tab, gain, w_in, conv_w, conv_b, gate_w, gate_b, lam, w_out, *, batch, seq):
    d_rnn = conv_w.shape[1]
    bd = d_rnn // RG_BLOCKS
    tn = min(512, d_rnn)
    proj = _norm_matmul(xf, mod_all, tab, gain, w_in.astype(BF16), mod_row=3, seq=seq, tn=tn, out_dtype=F32)
    cbw = min(512, d_rnn)
    ncb = d_rnn // cbw
    ts = min(256, seq)
    nts = seq // ts
    kern = functools.partial(_rglru_kernel, ts=ts, bd=bd)
    y = pl.pallas_call(
        kern,
        grid=(batch, ncb, nts),
        in_specs=[pl.BlockSpec((ts, cbw), lambda b, c, t: (b * nts + t, c)),
                  pl.BlockSpec((ts, cbw), lambda b, c, t: (b * nts + t, ncb + c)),
                  pl.BlockSpec((CONV_WIDTH, cbw), lambda b, c, t: (0, c)),
                  pl.BlockSpec((1, cbw), lambda b, c, t: (0, c)),
                  pl.BlockSpec((2, cbw // bd, bd, bd), lambda b, c, t: (0, c, 0, 0)),
                  pl.BlockSpec((2, cbw), lambda b, c, t: (0, c)),
                  pl.BlockSpec((1, cbw), lambda b, c, t: (0, c))],
        out_specs=pl.BlockSpec((ts, cbw), lambda b, c, t: (b * nts + t, c)),
        out_shape=jax.ShapeDtypeStruct((batch * seq, d_rnn), BF16),
        scratch_shapes=[pltpu.VMEM((1, cbw), F32), pltpu.VMEM((SUBLANES, cbw), F32)],
        compiler_params=_cparams(("arbitrary", "arbitrary", "arbitrary")),
        name="rglru",
    )(proj, proj, conv_w, conv_b.reshape(1, d_rnn), gate_w.astype(BF16), gate_b, lam.reshape(1, d_rnn))
    return _matmul_res(y, w_out.astype(BF16), xf, mod_all, tab, mod_row=5, seq=seq)


def kernel(x, c, rel_bias, ada_w, ada_b, ada_table, norm_g, ffn_w_in, ffn_w_out, dsa_w_in, dsa_w_out, dsa_q_gain, dsa_k_gain, nsa_w_in, nsa_gate_b, nsa_w_out, nsa_q_gain, nsa_k_gain, nsa_cmp_pos, nsa_cmp_w1, nsa_cmp_w2, rg_w_in, rg_conv_w, rg_conv_b, rg_gate_w, rg_gate_b, rg_lambda, rg_w_out):
    B, S, D = x.shape
    depth = ada_table.shape[0]
    assert S % TK == 0 and S >= NEAR_TILES * TQ
    mod_all = _mod_all(c, ada_w, ada_b)
    strip, strip_amax, ustrip = _bias_strips(rel_bias)
    w1_all, w2_all, tf = _prep_ffn_weights(ffn_w_in, ffn_w_out)
    xf = x.reshape(B * S, D)
    for layer in range(depth):
        tab = ada_table[layer]
        xf = _ffn(xf, mod_all, tab, norm_g[layer, 0], w1_all, w2_all, tf, 2 * layer, mod_row=0, seq=S)
        kind, j = layer % 3, layer // 3
        if kind == 0:
            xf = _dsa_mixer(xf, mod_all, tab, norm_g[layer, 1], dsa_w_in[j], dsa_w_out[j], dsa_q_gain[j],
                            dsa_k_gain[j], strip, strip_amax, batch=B, seq=S)
        elif kind == 1:
            xf = _nsa_mixer(xf, mod_all, tab, norm_g[layer, 1], nsa_w_in[j], nsa_gate_b[j], nsa_w_out[j],
                            nsa_q_gain[j], nsa_k_gain[j], nsa_cmp_pos[j], nsa_cmp_w1[j], nsa_cmp_w2[j],
                            strip, strip_amax, ustrip, batch=B, seq=S)
        else:
            xf = _rglru_mixer(xf, mod_all, tab, norm_g[layer, 1], rg_w_in[j], rg_conv_w[j], rg_conv_b[j],
                              rg_gate_w[j], rg_gate_b[j], rg_lambda[j], rg_w_out[j], batch=B, seq=S)
        xf = _ffn(xf, mod_all, tab, norm_g[layer, 2], w1_all, w2_all, tf, 2 * layer + 1, mod_row=6, seq=S)
    return xf.reshape(B, S, D)
```

```python
import functools
import math

import numpy as np
import jax
import jax.numpy as jnp
from jax import lax
from jax.experimental import pallas as pl
from jax.experimental.pallas import tpu as pltpu

N_HEADS = 32
HEAD_DIM = 128
N_KV_HEADS = 4
FFN_RES = 0.5
N_MOD = 9
NUM_BUCKETS = 32
MAX_DISTANCE = 1024
EPS = 1e-6
NEG = -1e30
BIG = 1e30
IDX_HEADS = 16
IDX_DIM = 128
TOPK_KEYS = 256
CMP_BLOCK = 32
CMP_STRIDE = 16
SEL_BLOCK = 64
SEL_TOPN = 16
WINDOW = 512
RG_BLOCKS = 16
CONV_WIDTH = 4
RG_C = 8.0

LANES = 128
SUBLANES = 8
VMEM_LIMIT_BYTES = 56 * 1024 * 1024

F32 = jnp.float32
BF16 = jnp.bfloat16
I32 = jnp.int32
INT_MIN = -2 ** 31
LOWEST = -3.0e38

TQ = 128
TK = 512
KPT = TK // TQ
SAFE_LOGIT = 60.0
LOG2E = 1.4426950408889634
FAR_DIST = int(math.ceil((NUM_BUCKETS // 2) * (MAX_DISTANCE / (NUM_BUCKETS // 2)) ** ((NUM_BUCKETS // 2 - 1) / (NUM_BUCKETS // 2)))) + 8
NEAR_TILES = -(-(FAR_DIST + TQ - 1) // TQ)
CPQ = TQ // CMP_STRIDE
CMP_LAST = CMP_STRIDE * (CPQ - 1) + CMP_BLOCK - 1
NEAR_CMP = -(-(-(-(FAR_DIST + CMP_LAST) // CMP_STRIDE)) // SUBLANES) * SUBLANES

STRIP_PAD = TK - TQ
STRIP_ROWS = NEAR_TILES * TQ + 2 * STRIP_PAD
NT_DIMS = (((1,), (1,)), ((), ()))
TN_DIMS = (((0,), (0,)), ((), ()))


def _round_up(n, m):
    return (n + m - 1) // m * m


def _cparams(sem):
    return pltpu.CompilerParams(dimension_semantics=sem, vmem_limit_bytes=VMEM_LIMIT_BYTES)


def _gelu_tanh(x):
    return 0.5 * x * (1.0 + jnp.tanh(0.7978845608028654 * (x + 0.044715 * x * x * x)))


def _mod_kernel(c_ref, w_ref, b_ref, o_ref):
    c = c_ref[...]
    s = c * jax.nn.sigmoid(c)
    o_ref[...] = jnp.dot(s.astype(BF16), w_ref[...].astype(BF16), preferred_element_type=F32) + b_ref[...]


def _mod_all(c, ada_w, ada_b):
    B, D = c.shape
    N = ada_w.shape[1]
    rows = 16
    tn = next(t for t in (512, 256, 128) if N % t == 0)
    cp = jnp.zeros((rows, D), F32).at[:B].set(c)
    out = pl.pallas_call(
        _mod_kernel,
        grid=(N // tn,),
        in_specs=[pl.BlockSpec((rows, D), lambda j: (0, 0)),
                  pl.BlockSpec((D, tn), lambda j: (0, j)),
                  pl.BlockSpec((1, tn), lambda j: (0, j))],
        out_specs=pl.BlockSpec((rows, tn), lambda j: (0, j)),
        out_shape=jax.ShapeDtypeStruct((rows, N), F32),
        compiler_params=_cparams(("arbitrary",)),
        name="ada_mod",
    )(cp, ada_w, ada_b.reshape(1, N))
    return out[:B].reshape(B, N_MOD, D)


def _norm_matmul_kernel(x_ref, mod_ref, tab_ref, g_ref, w_ref, gc_ref, *refs, mod_row, epilogue,
                        norm_tiles, n_aux, n_tiles):
    swiglu = epilogue == "swiglu"
    if swiglu:
        wu_ref, refs = refs[0], refs[1:]
    o_ref = refs[0]
    aux_refs = refs[1:1 + n_aux]
    y_scr = refs[1 + n_aux]
    j = pl.program_id(1)
    d_model = x_ref.shape[1]
    kc = min(512, d_model)

    def finish(h, u):
        if swiglu:
            o_ref[...] = (h * jax.nn.sigmoid(h) * u).astype(o_ref.dtype)
            return
        if norm_tiles:
            is_norm = functools.reduce(lambda a, b: a | b, [j == t for t in norm_tiles])

            @pl.when(is_norm)
            def _():
                for c in range(h.shape[1] // LANES):
                    hc = h[:, c * LANES:(c + 1) * LANES]
                    ms = jnp.mean(hc * hc, axis=-1, keepdims=True)
                    yc = hc * lax.rsqrt(ms + EPS) * gc_ref[:, c * LANES:(c + 1) * LANES]
                    o_ref[:, c * LANES:(c + 1) * LANES] = yc.astype(o_ref.dtype)

            @pl.when(jnp.logical_not(is_norm))
            def _():
                o_ref[...] = h.astype(o_ref.dtype)
        else:
            o_ref[...] = h.astype(o_ref.dtype)
        for a in range(n_aux):
            @pl.when(j == n_tiles - n_aux + a)
            def _(a=a):
                aux_refs[a][...] = h

    @pl.when(j == 0)
    def _():
        x = x_ref[...]
        inv = lax.rsqrt(jnp.sum(x * x, axis=-1, keepdims=True) * (1.0 / d_model) + EPS)
        mod = mod_ref[0]
        shift = mod[mod_row:mod_row + 1, :] + tab_ref[mod_row:mod_row + 1, :]
        gs = g_ref[...] * (1.0 + mod[mod_row + 1:mod_row + 2, :] + tab_ref[mod_row + 1:mod_row + 2, :])
        h = u = None
        for c0 in range(0, d_model, kc):
            yc = ((x[:, c0:c0 + kc] * inv) * gs[:, c0:c0 + kc] + shift[:, c0:c0 + kc]).astype(BF16)
            y_scr[:, c0:c0 + kc] = yc
            part = jnp.dot(yc, w_ref[c0:c0 + kc, :], preferred_element_type=F32)
            h = part if h is None else h + part
            if swiglu:
                part_u = jnp.dot(yc, wu_ref[c0:c0 + kc, :], preferred_element_type=F32)
                u = part_u if u is None else u + part_u
        finish(h, u)

    @pl.when(j != 0)
    def _():
        h = jnp.dot(y_scr[...], w_ref[...], preferred_element_type=F32)
        u = jnp.dot(y_scr[...], wu_ref[...], preferred_element_type=F32) if swiglu else None
        finish(h, u)


def _weight_spec(w, rows, cols, w_index, col_block):
    if w.ndim == 2:
        return pl.BlockSpec((rows, cols), lambda i, j: (0, col_block(j)))
    return pl.BlockSpec((None, rows, cols), lambda i, j: (w_index, 0, col_block(j)))


def _norm_matmul(x, mod_all, tab, gain, w, *, mod_row, seq, epilogue="plain", tn, tm=512,
                 out_dtype=BF16, gain_cols=None, norm_tiles=(), n_aux=0, w_index=0):
    T, D = x.shape
    swiglu = epilogue == "swiglu"
    n_tiles = w.shape[-1] // (2 * tn if swiglu else tn)
    n_out = n_tiles * tn
    tm = min(tm, seq)
    bpb = seq // tm
    if gain_cols is None:
        gain_cols = jnp.zeros((1, n_out), F32)
    out_shape = [jax.ShapeDtypeStruct((T, n_out), out_dtype)]
    out_specs = [pl.BlockSpec((tm, tn), lambda i, j: (i, j))]
    for _ in range(n_aux):
        out_shape.append(jax.ShapeDtypeStruct((T, tn), F32))
        out_specs.append(pl.BlockSpec((tm, tn), lambda i, j: (i, 0)))
    kern = functools.partial(_norm_matmul_kernel, mod_row=mod_row, epilogue=epilogue,
                             norm_tiles=tuple(norm_tiles), n_aux=n_aux, n_tiles=n_tiles)
    in_specs = [pl.BlockSpec((tm, D), lambda i, j: (i, 0)),
                pl.BlockSpec((1, N_MOD, D), lambda i, j: (i // bpb, 0, 0)),
                pl.BlockSpec((N_MOD, D), lambda i, j: (0, 0)),
                pl.BlockSpec((1, D), lambda i, j: (0, 0)),
                _weight_spec(w, D, tn, w_index, lambda j: j),
                pl.BlockSpec((1, tn), lambda i, j: (0, j))]
    args = [x, mod_all, tab, gain.reshape(1, D), w, gain_cols]
    if swiglu:
        in_specs.append(_weight_spec(w, D, tn, w_index, lambda j: n_tiles + j))
        args.append(w)
    res = pl.pallas_call(
        kern,
        grid=(T // tm, n_tiles),
        in_specs=in_specs,
        out_specs=out_specs,
        out_shape=out_shape,
        scratch_shapes=[pltpu.VMEM((tm, D), BF16)],
        compiler_params=_cparams(("arbitrary", "arbitrary")),
        name="norm_matmul_" + epilogue,
    )(*args)
    return res if n_aux else res[0]


def _matmul_res_kernel(a_ref, w_ref, x_ref, mod_ref, tab_ref, o_ref, *, mod_row, res_scale):
    h = jnp.dot(a_ref[...], w_ref[...], preferred_element_type=F32)
    gate = mod_ref[0, mod_row:mod_row + 1, :] + tab_ref[mod_row:mod_row + 1, :]
    if res_scale != 1.0:
        gate = res_scale * gate
    o_ref[...] = x_ref[...] + gate * h


def _matmul_res(a, w, x, mod_all, tab, *, mod_row, seq, res_scale=1.0, tm=512, tn=1024, w_index=0):
    T, K = a.shape
    D = w.shape[-1]
    tm = min(tm, seq)
    tn = min(tn, D)
    bpb = seq // tm
    kern = functools.partial(_matmul_res_kernel, mod_row=mod_row, res_scale=res_scale)
    return pl.pallas_call(
        kern,
        grid=(T // tm, D // tn),
        in_specs=[pl.BlockSpec((tm, K), lambda i, j: (i, 0)),
                  _weight_spec(w, K, tn, w_index, lambda j: j),
                  pl.BlockSpec((tm, tn), lambda i, j: (i, j)),
                  pl.BlockSpec((1, N_MOD, tn), lambda i, j: (i // bpb, 0, j)),
                  pl.BlockSpec((N_MOD, tn), lambda i, j: (0, j))],
        out_specs=pl.BlockSpec((tm, tn), lambda i, j: (i, j)),
        out_shape=jax.ShapeDtypeStruct((T, D), F32),
        compiler_params=_cparams(("arbitrary", "arbitrary")),
        name="matmul_residual",
    )(a, w, x, mod_all, tab)


def _ffn_tile(d_ff):
    return 512 if d_ff >= 512 else _round_up(d_ff, LANES)


def _pad_cols_kernel(x_ref, o_ref):
    f = x_ref.shape[1]
    o_ref[:, :f] = x_ref[...].astype(o_ref.dtype)
    if o_ref.shape[1] > f:
        o_ref[:, f:] = jnp.zeros((o_ref.shape[0], o_ref.shape[1] - f), o_ref.dtype)


def _pad_rows_kernel(x_ref, o_ref, *, rows_valid):
    tr = o_ref.shape[0]
    row = pl.program_id(1) * tr + lax.broadcasted_iota(I32, o_ref.shape, 0)
    o_ref[...] = jnp.where(row < rows_valid, x_ref[...], 0.0).astype(o_ref.dtype)


def _prep_ffn_weights(ffn_w_in, ffn_w_out):
    L, two, D, two_f = ffn_w_in.shape
    F = two_f // 2
    tf = _ffn_tile(F)
    Fp = _round_up(F, tf)
    rows = L * two * D
    tr = 256
    w1 = pl.pallas_call(
        _pad_cols_kernel,
        grid=(rows // tr, 2),
        in_specs=[pl.BlockSpec((tr, F), lambda r, h: (r, h))],
        out_specs=pl.BlockSpec((tr, Fp), lambda r, h: (r, h)),
        out_shape=jax.ShapeDtypeStruct((rows, 2 * Fp), BF16),
        compiler_params=_cparams(("arbitrary", "arbitrary")),
        name="ffn_w_in_prep",
    )(ffn_w_in.reshape(rows, two_f))
    w2 = pl.pallas_call(
        functools.partial(_pad_rows_kernel, rows_valid=F),
        grid=(L * two, Fp // tf),
        in_specs=[pl.BlockSpec((None, tf, D), lambda l, r: (l, r, 0))],
        out_specs=pl.BlockSpec((None, tf, D), lambda l, r: (l, r, 0)),
        out_shape=jax.ShapeDtypeStruct((L * two, Fp, D), BF16),
        compiler_params=_cparams(("arbitrary", "arbitrary")),
        name="ffn_w_out_prep",
    )(ffn_w_out.reshape(L * two, F, D))
    return w1.reshape(L * two, D, 2 * Fp), w2, tf


def _ffn(x, mod_all, tab, gain, w1, w2, tf, w_index, *, mod_row, seq):
    a = _norm_matmul(x, mod_all, tab, gain, w1, mod_row=mod_row, seq=seq, epilogue="swiglu", tn=tf, w_index=w_index)
    return _matmul_res(a, w2, x, mod_all, tab, mod_row=mod_row + 2, seq=seq, res_scale=FFN_RES, w_index=w_index)


def _t5_bucket(dist):
    n = jnp.maximum(dist, 0)
    max_exact = NUM_BUCKETS // 2
    nf = jnp.maximum(n, 1).astype(F32)
    large = max_exact + (jnp.log(nf / max_exact) / math.log(MAX_DISTANCE / max_exact)
                         * (NUM_BUCKETS - max_exact)).astype(I32)
    large = jnp.minimum(large, NUM_BUCKETS - 1)
    return jnp.where(n < max_exact, n, large)


def _bias_strip_kernel(rb_ref, bkt_ref, o_ref, amax_ref):
    h = pl.program_id(0)
    bk = bkt_ref[...]
    far = rb_ref[NUM_BUCKETS - 1, h]
    acc = jnp.zeros(bk.shape, F32)
    for k in range(NUM_BUCKETS - 1):
        acc = jnp.where(bk == k, LOG2E * (rb_ref[k, h] - far), acc)
    o_ref[...] = acc
    amax_ref[...] = jnp.max(jnp.abs(acc), axis=0, keepdims=True)


def _bias_strip(rel_bias, bkt):
    rows = bkt.shape[0]
    H = rel_bias.shape[1]
    return pl.pallas_call(
        _bias_strip_kernel,
        grid=(H,),
        in_specs=[pl.BlockSpec(memory_space=pltpu.SMEM),
                  pl.BlockSpec((rows, TQ), lambda h: (0, 0))],
        out_specs=[pl.BlockSpec((rows, TQ), lambda h: (0, h)), pl.BlockSpec((1, TQ), lambda h: (0, h))],
        out_shape=[jax.ShapeDtypeStruct((rows, H * TQ), F32), jax.ShapeDtypeStruct((1, H * TQ), F32)],
        compiler_params=_cparams(("arbitrary",)),
        name="bias_strip",
    )(rel_bias, bkt)


def _bias_strips(rel_bias):
    iq = jnp.arange(TQ, dtype=I32)[None, :]
    x = jnp.arange(-STRIP_PAD, NEAR_TILES * TQ + STRIP_PAD, dtype=I32)[:, None]
    bkt_tok = _t5_bucket(iq - x + (NEAR_TILES - 1) * TQ)
    y = jnp.arange(NEAR_CMP, dtype=I32)[:, None]
    bkt_cmp = _t5_bucket(iq + CMP_STRIDE * (NEAR_CMP - 1 - y) - CMP_LAST)
    strip, strip_amax = _bias_strip(rel_bias, bkt_tok)
    ustrip, _ = _bias_strip(rel_bias, bkt_cmp)
    return strip, strip_amax, ustrip


def _stack_heads(q_ref, qs_scr, n):
    for r in range(n):
        qs_scr[r * TQ:(r + 1) * TQ, :] = q_ref[:, r * LANES:(r + 1) * LANES]


def _num_far_tiles(i):
    return jnp.maximum(i - NEAR_TILES + 1, 0) // KPT


def _strip_row(i, kt):
    return pl.multiple_of((KPT + NEAR_TILES - 2 - (i - kt * KPT)) * TQ, TQ)


def _flash_init(m_scr, l_scr, acc_scr):
    m_scr[...] = jnp.full(m_scr.shape, NEG, F32)
    l_scr[...] = jnp.zeros(l_scr.shape, F32)
    acc_scr[...] = jnp.zeros(acc_scr.shape, F32)


def _max_row_sumsq(k_ref):
    rows = k_ref.shape[0]
    ch = min(rows, 1024)

    def body(c, best):
        kk = k_ref[pl.ds(pl.multiple_of(c * ch, ch), ch), :].astype(F32)
        return jnp.maximum(best, jnp.max(jnp.sum(kk * kk, axis=1, keepdims=True)))

    return lax.fori_loop(0, rows // ch, body, jnp.float32(0.0))


def _logits_bounded(qs_scr, k_sumsq, amax_row):
    qf = qs_scr[...].astype(F32)
    q_sumsq = lax.dot_general(jnp.ones((SUBLANES, HEAD_DIM), BF16), (qf * qf).astype(BF16), NT_DIMS,
                              preferred_element_type=F32)[0:1, :]
    bound = jnp.sqrt(q_sumsq * k_sumsq) * 1.02 + amax_row
    return jnp.max(bound) <= SAFE_LOGIT


def _flash_tile(k_tile, v_tile, qs_scr, madd, strip_ref, strip_row, m_scr, l_scr, acc_scr, n_heads, online):
    s = lax.dot_general(k_tile, qs_scr[...], NT_DIMS, preferred_element_type=F32) + jnp.tile(madd, (1, n_heads))
    if strip_ref is not None:
        s = s + strip_ref[pl.ds(strip_row, k_tile.shape[0]), :]
    if not online:
        p = jnp.exp2(s)
        l_scr[...] += jnp.sum(p, axis=0, keepdims=True)
        acc_scr[...] += lax.dot_general(v_tile, p.astype(BF16), TN_DIMS, preferred_element_type=F32)
        return
    m_prev = m_scr[...]
    m_new = jnp.maximum(m_prev, jnp.max(s, axis=0, keepdims=True))
    alpha = jnp.exp2(m_prev - m_new)
    p = jnp.exp2(s - m_new)
    l_scr[...] = alpha * l_scr[...] + jnp.sum(p, axis=0, keepdims=True)
    pv = lax.dot_general(v_tile, p.astype(BF16), TN_DIMS, preferred_element_type=F32)
    acc_scr[...] = alpha * acc_scr[...] + pv
    m_scr[...] = m_new


def _far_tiles(n_far, tile, online):
    if online:
        def body(kt, carry):
            tile(pl.multiple_of(kt * TK, TK), TK)
            return carry

        lax.fori_loop(0, n_far, body, 0)
        return

    def pair_body(u, carry):
        tile(pl.multiple_of(u * 2 * TK, 2 * TK), 2 * TK)
        return carry

    lax.fori_loop(0, n_far // 2, pair_body, 0)

    @pl.when(n_far % 2 == 1)
    def _():
        tile(pl.multiple_of((n_far - 1) * TK, TK), TK)


def _flash_sweep(bounded, sweep, m_scr, l_scr, acc_scr):
    _flash_init(m_scr, l_scr, acc_scr)

    @pl.when(bounded)
    def _():
        sweep(False)

    @pl.when(jnp.logical_not(bounded))
    def _():
        sweep(True)
        l_scr[...] = jnp.where(m_scr[...] > 0.5 * NEG, l_scr[...], 0.0)

    l = l_scr[...]
    return acc_scr[...] * jnp.where(l > 0.0, 1.0 / l, 0.0)


def _write_heads(o_ref, o_t, n):
    for r in range(n):
        o_ref[:, r * LANES:(r + 1) * LANES] = o_t[:, r * TQ:(r + 1) * TQ].T.astype(o_ref.dtype)


def _dsa_index_kernel(qi_ref, ki_ref, wi_ref, mask_ref, keys_scr, qis_scr, j_scr, *, n_keep, ih, seq):
    i = pl.program_id(1)
    nkt = i // KPT + 1
    _stack_heads(qi_ref, qis_scr, ih)
    w_t = wi_ref[...].T * (ih ** -0.5 * IDX_DIM ** -0.5)
    w_rows = [w_t[h:h + 1, :] for h in range(ih)]
    q_pos = i * TQ + lax.broadcasted_iota(I32, (TQ, TQ), 1)
    k_iota = lax.broadcasted_iota(I32, (TQ, TQ), 0)

    def sub_rows(kt, u):
        return pl.multiple_of(kt * TK + u * TQ, TQ)

    def score_tile(kt, carry):
        k0 = pl.multiple_of(kt * TK, TK)
        d = lax.dot_general(ki_ref[pl.ds(k0, TK), :], qis_scr[...], NT_DIMS, preferred_element_type=F32)
        for u in range(KPT):
            sc = jnp.zeros((TQ, TQ), F32)
            for h in range(ih):
                sc = sc + jnp.maximum(d[u * TQ:(u + 1) * TQ, h * TQ:(h + 1) * TQ], 0.0) * w_rows[h]
            bits = pltpu.bitcast(sc, I32)
            key = jnp.where(bits < 0, bits ^ 0x7FFFFFFF, bits)
            r0 = sub_rows(kt, u)
            keys_scr[pl.ds(r0, TQ), :] = jnp.where(r0 + k_iota <= q_pos, key, INT_MIN)
        return carry

    lax.fori_loop(0, nkt, score_tile, 0)

    def count(pred):
        def body(kt, acc):
            for u in range(KPT):
                r0 = sub_rows(kt, u)
                acc = acc + jnp.where(pred(keys_scr[pl.ds(r0, TQ), :], r0 + k_iota), 1.0, 0.0)
            return acc
        acc = lax.fori_loop(0, nkt, body, jnp.zeros((TQ, TQ), F32))
        return jnp.sum(acc, axis=0, keepdims=True)

    def bit_body(b, lo):
        cand = lo + jnp.left_shift(jnp.int32(1), 31 - b)
        cnt = count(lambda kk, idx: kk >= cand)
        return jnp.where(cnt >= n_keep, cand, lo)

    thr = lax.fori_loop(0, 32, bit_body, jnp.full((1, TQ), INT_MIN, I32))
    cnt_ge = count(lambda kk, idx: kk >= thr)
    has_tie = (cnt_ge > n_keep) & (thr > INT_MIN)
    j_scr[...] = jnp.full((1, TQ), seq, I32)

    @pl.when(jnp.max(jnp.where(has_tie, 1.0, 0.0)) > 0.5)
    def _():
        need = n_keep - count(lambda kk, idx: kk > thr)

        def jb(b, lo):
            cand = lo + jnp.left_shift(jnp.int32(1), (seq.bit_length() - 2) - b)
            c = count(lambda kk, idx: (kk == thr) & (idx < cand))
            return jnp.where(c < need, cand, lo)

        j_last = lax.fori_loop(0, seq.bit_length() - 1, jb, jnp.zeros((1, TQ), I32))
        j_scr[...] = jnp.where(has_tie, j_last, seq)

    thr_c = jnp.maximum(thr, INT_MIN + 1)
    j_last = j_scr[...]

    def write_tile(kt, carry):
        for u in range(KPT):
            r0 = sub_rows(kt, u)
            kk = keys_scr[pl.ds(r0, TQ), :]
            sel = (kk > thr_c) | ((kk == thr_c) & (r0 + k_iota <= j_last))
            mask_ref[pl.ds(r0, TQ), :] = jnp.where(sel, 0.0, NEG).astype(BF16)
        return carry

    lax.fori_loop(0, nkt, write_tile, 0)

    def fill_tile(kt, carry):
        k0 = pl.multiple_of(kt * TK, TK)
        mask_ref[pl.ds(k0, TK), :] = jnp.full((TK, TQ), NEG, BF16)
        return carry

    lax.fori_loop(nkt, seq // TK, fill_tile, 0)


def _dsa_index(proj, aux, *, batch, seq, qi_blk, ki_blk, n_keep):
    nq = seq // TQ
    ihw = IDX_HEADS * IDX_DIM
    kern = functools.partial(_dsa_index_kernel, n_keep=n_keep, ih=IDX_HEADS, seq=seq)
    return pl.pallas_call(
        kern,
        grid=(batch, nq),
        in_specs=[pl.BlockSpec((TQ, ihw), lambda b, i: (b * nq + i, qi_blk)),
                  pl.BlockSpec((seq, IDX_DIM), lambda b, i: (b, ki_blk)),
                  pl.BlockSpec((TQ, LANES), lambda b, i: (b * nq + i, 1))],
        out_specs=pl.BlockSpec((None, seq, TQ), lambda b, i: (b, 0, i)),
        out_shape=jax.ShapeDtypeStruct((batch, seq, seq), BF16),
        scratch_shapes=[pltpu.VMEM((seq, TQ), I32),
                        pltpu.VMEM((IDX_HEADS * TQ, IDX_DIM), BF16),
                        pltpu.VMEM((1, TQ), I32)],
        compiler_params=_cparams(("arbitrary", "arbitrary")),
        name="dsa_index",
    )(proj, proj, aux)


def _dsa_attn_kernel(q_ref, k_ref, v_ref, mask_ref, strip_ref, amax_ref, o_ref, qs_scr, m_scr, l_scr, acc_scr,
                     ksq_scr, *, r_heads):
    i = pl.program_id(2)

    @pl.when(i == 0)
    def _():
        ksq_scr[0] = _max_row_sumsq(k_ref)

    _stack_heads(q_ref, qs_scr, r_heads)
    n_far = _num_far_tiles(i)

    def sweep(online):
        def far_tile(k0, rows):
            _flash_tile(k_ref[pl.ds(k0, rows), :], v_ref[pl.ds(k0, rows), :], qs_scr,
                        mask_ref[pl.ds(k0, rows), :].astype(F32), None, None, m_scr, l_scr, acc_scr, r_heads, online)

        def near_body(kt, carry):
            k0 = pl.multiple_of(kt * TK, TK)
            _flash_tile(k_ref[pl.ds(k0, TK), :], v_ref[pl.ds(k0, TK), :], qs_scr,
                        mask_ref[pl.ds(k0, TK), :].astype(F32), strip_ref, _strip_row(i, kt),
                        m_scr, l_scr, acc_scr, r_heads, online)
            return carry

        _far_tiles(n_far, far_tile, online)
        lax.fori_loop(n_far, i // KPT + 1, near_body, 0)

    bounded = _logits_bounded(qs_scr, ksq_scr[0], amax_ref[...])
    _write_heads(o_ref, _flash_sweep(bounded, sweep, m_scr, l_scr, acc_scr), r_heads)


def _dsa_attn(proj, mask, strip, strip_amax, *, batch, seq, k_blk, v_blk):
    nq = seq // TQ
    G = N_KV_HEADS
    R = N_HEADS // G
    W = R * TQ
    kern = functools.partial(_dsa_attn_kernel, r_heads=R)
    return pl.pallas_call(
        kern,
        grid=(batch, G, nq),
        in_specs=[pl.BlockSpec((TQ, R * HEAD_DIM), lambda b, g, i: (b * nq + i, g)),
                  pl.BlockSpec((seq, HEAD_DIM), lambda b, g, i: (b, k_blk + g)),
                  pl.BlockSpec((seq, HEAD_DIM), lambda b, g, i: (b, v_blk + g)),
                  pl.BlockSpec((None, seq, TQ), lambda b, g, i: (b, 0, i)),
                  pl.BlockSpec((STRIP_ROWS, W), lambda b, g, i: (0, g)),
                  pl.BlockSpec((1, W), lambda b, g, i: (0, g))],
        out_specs=pl.BlockSpec((TQ, R * HEAD_DIM), lambda b, g, i: (b * nq + i, g)),
        out_shape=jax.ShapeDtypeStruct((batch * seq, N_HEADS * HEAD_DIM), BF16),
        scratch_shapes=[pltpu.VMEM((W, HEAD_DIM), BF16),
                        pltpu.VMEM((1, W), F32), pltpu.VMEM((1, W), F32), pltpu.VMEM((HEAD_DIM, W), F32),
                        pltpu.SMEM((1,), F32)],
        compiler_params=_cparams(("arbitrary", "arbitrary", "arbitrary")),
        name="dsa_attn",
    )(proj, proj, proj, mask, strip, strip_amax)


def _proj_layout(w_in, segments, tn):
    cols = []
    for start, width in segments:
        pad = _round_up(width, tn) - width
        cols.append(jnp.pad(w_in[:, start:start + width], ((0, 0), (0, pad))))
    return jnp.concatenate(cols, axis=1).astype(BF16)


def _dsa_mixer(xf, mod_all, tab, gain, w_in, w_out, q_gain, k_gain, strip, strip_amax, *, batch, seq):
    AW, KW = N_HEADS * HEAD_DIM, N_KV_HEADS * HEAD_DIM
    IW = IDX_HEADS * IDX_DIM
    tn = KW
    wp = _proj_layout(w_in, [(0, AW), (AW + 2 * KW, IW), (AW, KW), (AW + KW, KW), (AW + 2 * KW + IW, IDX_DIM + IDX_HEADS)], tn)
    nq_t, ni_t = AW // tn, IW // tn
    gain_cols = jnp.concatenate([jnp.tile(q_gain * (HEAD_DIM ** -0.5 * LOG2E), N_HEADS), jnp.zeros((IW,), F32),
                                 jnp.tile(k_gain, N_KV_HEADS), jnp.zeros((2 * tn,), F32)]).reshape(1, -1)
    proj, aux = _norm_matmul(xf, mod_all, tab, gain, wp, mod_row=3, seq=seq, tn=tn, gain_cols=gain_cols,
                             norm_tiles=tuple(range(nq_t)) + (nq_t + ni_t,), n_aux=1)
    k_blk = (nq_t + ni_t) * tn // HEAD_DIM
    mask = _dsa_index(proj, aux, batch=batch, seq=seq, qi_blk=AW // IW, ki_blk=(nq_t + ni_t + 2) * tn // IDX_DIM,
                      n_keep=min(TOPK_KEYS, seq // 4))
    o = _dsa_attn(proj, mask, strip, strip_amax, batch=batch, seq=seq, k_blk=k_blk, v_blk=k_blk + tn // HEAD_DIM)
    return _matmul_res(o, w_out.astype(BF16), xf, mod_all, tab, mod_row=5, seq=seq)


def _nsa_compress_kernel(xk_ref, xkn_ref, xv_ref, xvn_ref, pos_ref, w1_ref, w2_ref, gain_ref, ok_ref, ov_ref, *, groups, kw):
    half = CMP_BLOCK // 2
    rows = xk_ref.shape[0]
    row_id = lax.broadcasted_iota(I32, (rows, HEAD_DIM), 0)

    def branch(br, x_ref, xn_ref, o_ref):
        for g in range(groups):
            p1 = jnp.zeros((rows, HEAD_DIM), F32)
            p2 = jnp.zeros((rows, HEAD_DIM), F32)
            p2n = jnp.zeros((xn_ref.shape[0], HEAD_DIM), F32)
            for r in range(half):
                c0 = r * kw + g * HEAD_DIM
                xa = x_ref[:, c0:c0 + HEAD_DIM]
                w_lo = w1_ref[br, r * HEAD_DIM:(r + 1) * HEAD_DIM, :]
                w_hi = w1_ref[br, (half + r) * HEAD_DIM:(half + r + 1) * HEAD_DIM, :]
                p1 = p1 + jnp.dot((xa + pos_ref[br, r:r + 1, :]).astype(BF16), w_lo, preferred_element_type=F32)
                p2 = p2 + jnp.dot((xa + pos_ref[br, half + r:half + r + 1, :]).astype(BF16), w_hi,
                                  preferred_element_type=F32)
                xb = xn_ref[:, c0:c0 + HEAD_DIM]
                p2n = p2n + jnp.dot((xb + pos_ref[br, half + r:half + r + 1, :]).astype(BF16), w_hi,
                                    preferred_element_type=F32)
            p2s = jnp.where(row_id == rows - 1, p2n[0:1, :], pltpu.roll(p2, rows - 1, 0))
            hid = _gelu_tanh(p1 + p2s)
            out = jnp.dot(hid.astype(BF16), w2_ref[br], preferred_element_type=F32)
            if br == 0:
                ms = jnp.mean(out * out, axis=-1, keepdims=True)
                out = out * lax.rsqrt(ms + EPS) * gain_ref[...]
            o_ref[:, g * HEAD_DIM:(g + 1) * HEAD_DIM] = out.astype(o_ref.dtype)

    branch(0, xk_ref, xkn_ref, ok_ref)
    branch(1, xv_ref, xvn_ref, ov_ref)


def _nsa_compress(kc, vc, cmp_pos, cmp_w1, cmp_w2, k_gain0, *, batch, seq):
    assert CMP_BLOCK == 2 * CMP_STRIDE
    KW = N_KV_HEADS * HEAD_DIM
    ncp = seq // CMP_STRIDE
    rb = min(TQ, ncp)
    nrb = ncp // rb
    nxt = 16
    wide = CMP_STRIDE * KW
    xk = kc.reshape(batch * ncp, wide)
    xv = vc.reshape(batch * ncp, wide)
    last_nxt = batch * ncp // nxt - 1
    main = pl.BlockSpec((rb, wide), lambda b, r: (b * nrb + r, 0))
    ahead = pl.BlockSpec((nxt, wide), lambda b, r: (jnp.minimum((b * nrb + r + 1) * (rb // nxt), last_nxt), 0))
    kern = functools.partial(_nsa_compress_kernel, groups=N_KV_HEADS, kw=KW)
    return pl.pallas_call(
        kern,
        grid=(batch, nrb),
        in_specs=[main, ahead, main, ahead,
                  pl.BlockSpec((2, CMP_BLOCK, HEAD_DIM), lambda b, r: (0, 0, 0)),
                  pl.BlockSpec((2, CMP_BLOCK * HEAD_DIM, HEAD_DIM), lambda b, r: (0, 0, 0)),
                  pl.BlockSpec((2, HEAD_DIM, HEAD_DIM), lambda b, r: (0, 0, 0)),
                  pl.BlockSpec((1, HEAD_DIM), lambda b, r: (0, 0))],
        out_specs=[pl.BlockSpec((rb, KW), lambda b, r: (b * nrb + r, 0))] * 2,
        out_shape=[jax.ShapeDtypeStruct((batch * ncp, KW), BF16)] * 2,
        compiler_params=_cparams(("arbitrary", "arbitrary")),
        name="nsa_compress",
    )(xk, xk, xv, xv, cmp_pos, cmp_w1.astype(BF16), cmp_w2.astype(BF16), k_gain0.reshape(1, HEAD_DIM))


def _nsa_attn_kernel(q_ref, kc_ref, vc_ref, ks_ref, vs_ref, kw_ref, vw_ref, g_ref, gb_ref, strip_ref, amax_ref,
                     ustrip_ref, cover_ref, o_ref, qs_scr, lc_scr, sel_scr, gt_scr, m_scr, l_scr, acc_scr, ocmp_scr,
                     osel_scr, ksq_scr, *, r_heads, n_pick):
    g = pl.program_id(1)
    i = pl.program_id(2)
    R = r_heads

    @pl.when(i == 0)
    def _():
        ksq_scr[0] = _max_row_sumsq(ks_ref)
        ksq_scr[1] = _max_row_sumsq(kw_ref)

    ncp = kc_ref.shape[0]
    n_sel = cover_ref.shape[0]
    pad = NEAR_CMP - CPQ
    _stack_heads(q_ref, qs_scr, R)
    dk_iota = lax.broadcasted_iota(I32, (TK, TQ), 0) - lax.broadcasted_iota(I32, (TK, TQ), 1)

    lc_scr[0:pad, :] = jnp.zeros((pad, R * TQ), F32)
    lc_scr[pad:pad + ncp, :] = lax.dot_general(kc_ref[...], qs_scr[...], NT_DIMS, preferred_element_type=F32)
    off = pl.multiple_of(i * CPQ, SUBLANES)
    lc_scr[pl.ds(off, NEAR_CMP), :] += ustrip_ref[...]
    c_iota = lax.broadcasted_iota(I32, (ncp, TQ), 0)
    t_pos = i * TQ + lax.broadcasted_iota(I32, (ncp, TQ), 1)
    vis = jnp.where(c_iota * CMP_STRIDE + (CMP_BLOCK - 1) <= t_pos, 0.0, NEG)
    s = lc_scr[pad:pad + ncp, :] + jnp.tile(vis, (1, R))
    m = jnp.max(s, axis=0, keepdims=True)
    p = jnp.exp2(s - m)
    inv = jnp.where(m > 0.5 * NEG, 1.0 / jnp.sum(p, axis=0, keepdims=True), 0.0)
    pc = p * inv
    ocmp_scr[...] = lax.dot_general(vc_ref[...], pc.astype(BF16), TN_DIMS, preferred_element_type=F32)

    psum = pc[:, 0:TQ]
    for r in range(1, R):
        psum = psum + pc[:, r * TQ:(r + 1) * TQ]
    hi = psum.astype(BF16)
    lo = (psum - hi.astype(F32)).astype(BF16)
    imp = (jnp.dot(cover_ref[...], hi, preferred_element_type=F32)
           + jnp.dot(cover_ref[...], lo, preferred_element_type=F32))
    n_io = lax.broadcasted_iota(I32, (n_sel, TQ), 0)
    t_sel = i * TQ + lax.broadcasted_iota(I32, (n_sel, TQ), 1)
    cur = jnp.right_shift(t_sel, SEL_BLOCK.bit_length() - 1)
    forced = (n_io == 0) | (n_io == cur) | (n_io == cur - 1)
    val0 = jnp.where(forced, BIG, jnp.where(n_io * SEL_BLOCK <= t_sel, imp, NEG))
    n_f = n_io.astype(F32)

    def pick_round(_, carry):
        val, selm = carry
        cm = jnp.max(val, axis=0, keepdims=True)
        idx = jnp.min(jnp.where(val == cm, n_f, 1e9), axis=0, keepdims=True)
        pick = n_f == idx
        return jnp.where(pick, LOWEST, val), jnp.where(pick, 0.0, selm)

    _, selm = lax.fori_loop(0, n_pick, pick_round, (val0, jnp.full((n_sel, TQ), NEG, F32)))
    sel_scr[...] = selm

    n_far = _num_far_tiles(i)
    kt_last = i // KPT

    def sel_sweep(online):
        def block_mask(k0, rows):
            b0 = k0 // SEL_BLOCK
            return jnp.concatenate([jnp.broadcast_to(sel_scr[pl.ds(b0 + b, 1), :], (SEL_BLOCK, TQ))
                                    for b in range(rows // SEL_BLOCK)], axis=0)

        def far_tile(k0, rows):
            _flash_tile(ks_ref[pl.ds(k0, rows), :], vs_ref[pl.ds(k0, rows), :], qs_scr, block_mask(k0, rows),
                        None, None, m_scr, l_scr, acc_scr, R, online)

        def near_body(kt, carry):
            k0 = pl.multiple_of(kt * TK, TK)
            bm = jnp.where(dk_iota <= (i - kt * KPT) * TQ, block_mask(k0, TK), NEG)
            _flash_tile(ks_ref[pl.ds(k0, TK), :], vs_ref[pl.ds(k0, TK), :], qs_scr, bm,
                        strip_ref, _strip_row(i, kt), m_scr, l_scr, acc_scr, R, online)
            return carry

        _far_tiles(n_far, far_tile, online)
        lax.fori_loop(n_far, kt_last + 1, near_body, 0)

    sel_bounded = _logits_bounded(qs_scr, ksq_scr[0], amax_ref[...])
    osel_scr[...] = _flash_sweep(sel_bounded, sel_sweep, m_scr, l_scr, acc_scr)

    def win_sweep(online):
        def win_tile(kt, rows):
            k0 = pl.multiple_of(kt * TK, TK)
            dk = lax.broadcasted_iota(I32, (rows, TQ), 0) - lax.broadcasted_iota(I32, (rows, TQ), 1)
            dist = (i - kt * KPT) * TQ - dk
            vis_w = jnp.where((dist >= 0) & (dist < WINDOW), 0.0, NEG)
            _flash_tile(kw_ref[pl.ds(k0, rows), :], vw_ref[pl.ds(k0, rows), :], qs_scr, vis_w,
                        strip_ref, _strip_row(i, kt), m_scr, l_scr, acc_scr, R, online)

        if online:
            def win_body(kt, carry):
                win_tile(kt, TK)
                return carry

            lax.fori_loop(jnp.maximum(kt_last - 1, 0), kt_last + 1, win_body, 0)
        else:
            @pl.when(kt_last >= 1)
            def _():
                win_tile(kt_last - 1, 2 * TK)

            @pl.when(kt_last == 0)
            def _():
                win_tile(kt_last, TK)

    win_bounded = _logits_bounded(qs_scr, ksq_scr[1], amax_ref[...])
    o_win = _flash_sweep(win_bounded, win_sweep, m_scr, l_scr, acc_scr)

    gt_scr[...] = jax.nn.sigmoid(g_ref[:, 0:LANES] + gb_ref[...]).T

    def gate_row(br):
        rows = gt_scr[pl.ds(pl.multiple_of(br * N_HEADS + g * R, SUBLANES), R), :]
        return jnp.concatenate([rows[r:r + 1, :] for r in range(R)], axis=1)

    o_t = gate_row(0) * ocmp_scr[...] + gate_row(1) * osel_scr[...] + gate_row(2) * o_win
    _write_heads(o_ref, o_t, R)


def _nsa_attn(proj, aux_g, gate_b, k_cmp, v_cmp, strip, strip_amax, ustrip, *, batch, seq, blks):
    nq = seq // TQ
    G = N_KV_HEADS
    R = N_HEADS // G
    assert R == SUBLANES and 3 * N_HEADS <= LANES and WINDOW <= TK and WINDOW <= (NEAR_TILES - 1) * TQ
    W = R * TQ
    ncp = seq // CMP_STRIDE
    n_cmp = (seq - CMP_BLOCK) // CMP_STRIDE + 1
    n_sel = seq // SEL_BLOCK
    cs = np.arange(ncp)[None, :] * CMP_STRIDE
    ss = np.arange(n_sel)[:, None] * SEL_BLOCK
    cover = ((cs < ss + SEL_BLOCK) & (cs + CMP_BLOCK - 1 >= ss) & (np.arange(ncp)[None, :] < n_cmp))
    cover = jnp.asarray(cover.astype(np.float32), dtype=BF16)
    gb = jnp.zeros((1, LANES), F32).at[0, :3 * N_HEADS].set(gate_b)
    kv = lambda blk: pl.BlockSpec((seq, HEAD_DIM), lambda b, g, i: (b, blk + g))
    cmp_spec = pl.BlockSpec((ncp, HEAD_DIM), lambda b, g, i: (b, g))
    kern = functools.partial(_nsa_attn_kernel, r_heads=R, n_pick=min(SEL_TOPN, n_sel))
    return pl.pallas_call(
        kern,
        grid=(batch, G, nq),
        in_specs=[pl.BlockSpec((TQ, W), lambda b, g, i: (b * nq + i, g)),
                  cmp_spec, cmp_spec, kv(blks["ks"]), kv(blks["vs"]), kv(blks["kw"]), kv(blks["vw"]),
                  pl.BlockSpec((TQ, aux_g.shape[1]), lambda b, g, i: (b * nq + i, 0)),
                  pl.BlockSpec((1, LANES), lambda b, g, i: (0, 0)),
                  pl.BlockSpec((STRIP_ROWS, W), lambda b, g, i: (0, g)),
                  pl.BlockSpec((1, W), lambda b, g, i: (0, g)),
                  pl.BlockSpec((NEAR_CMP, W), lambda b, g, i: (0, g)),
                  pl.BlockSpec((n_sel, ncp), lambda b, g, i: (0, 0))],
        out_specs=pl.BlockSpec((TQ, W), lambda b, g, i: (b * nq + i, g)),
        out_shape=jax.ShapeDtypeStruct((batch * seq, N_HEADS * HEAD_DIM), BF16),
        scratch_shapes=[pltpu.VMEM((W, HEAD_DIM), BF16),
                        pltpu.VMEM((NEAR_CMP - CPQ + ncp, W), F32),
                        pltpu.VMEM((n_sel, TQ), F32),
                        pltpu.VMEM((LANES, TQ), F32),
                        pltpu.VMEM((1, W), F32), pltpu.VMEM((1, W), F32), pltpu.VMEM((HEAD_DIM, W), F32),
                        pltpu.VMEM((HEAD_DIM, W), F32), pltpu.VMEM((HEAD_DIM, W), F32),
                        pltpu.SMEM((2,), F32)],
        compiler_params=_cparams(("arbitrary", "arbitrary", "arbitrary")),
        name="nsa_attn",
    )(proj, k_cmp, v_cmp, proj, proj, proj, proj, aux_g, gb, strip, strip_amax, ustrip, cover)


def _nsa_mixer(xf, mod_all, tab, gain, w_in, gate_b, w_out, q_gain, k_gain, cmp_pos, cmp_w1, cmp_w2, strip, strip_amax,
               ustrip, *, batch, seq):
    AW, KW = N_HEADS * HEAD_DIM, N_KV_HEADS * HEAD_DIM
    tn = KW
    wp = _proj_layout(w_in, [(0, AW), (AW + 2 * KW, KW), (AW + 3 * KW, KW), (AW + 4 * KW, KW), (AW + 5 * KW, KW),
                             (AW, KW), (AW + KW, KW), (AW + 6 * KW, 3 * N_HEADS)], tn)
    nq_t = AW // tn
    gain_cols = jnp.concatenate([jnp.tile(q_gain * (HEAD_DIM ** -0.5 * LOG2E), N_HEADS), jnp.tile(k_gain[1], N_KV_HEADS),
                                 jnp.zeros((tn,), F32), jnp.tile(k_gain[2], N_KV_HEADS),
                                 jnp.zeros((4 * tn,), F32)]).reshape(1, -1)
    proj, kc, vc, aux_g = _norm_matmul(xf, mod_all, tab, gain, wp, mod_row=3, seq=seq, tn=tn, gain_cols=gain_cols,
                                       norm_tiles=tuple(range(nq_t)) + (nq_t, nq_t + 2), n_aux=3)
    k_cmp, v_cmp = _nsa_compress(kc, vc, cmp_pos, cmp_w1, cmp_w2, k_gain[0], batch=batch, seq=seq)
    per = tn // HEAD_DIM
    blks = {"ks": nq_t * per, "vs": (nq_t + 1) * per, "kw": (nq_t + 2) * per, "vw": (nq_t + 3) * per}
    o = _nsa_attn(proj, aux_g, gate_b, k_cmp, v_cmp, strip, strip_amax, ustrip, batch=batch, seq=seq, blks=blks)
    return _matmul_res(o, w_out.astype(BF16), xf, mod_all, tab, mod_row=5, seq=seq)


def _rglru_kernel(gbr_ref, xr_ref, cw_ref, cb_ref, gw_ref, gbias_ref, lam_ref, o_ref, h_scr, tail_scr, *, ts, bd):
    @pl.when(pl.program_id(2) == 0)
    def _():
        h_scr[...] = jnp.zeros(h_scr.shape, F32)
        tail_scr[...] = jnp.zeros(tail_scr.shape, F32)

    x = xr_ref[...]
    cbw = x.shape[1]
    xfull = jnp.concatenate([tail_scr[...], x], axis=0)
    xc = cw_ref[CONV_WIDTH - 1:CONV_WIDTH, :] * x + cb_ref[...]
    for w in range(1, CONV_WIDTH):
        xc = xc + cw_ref[CONV_WIDTH - 1 - w:CONV_WIDTH - w, :] * pltpu.roll(xfull, w, 0)[SUBLANES:, :]
    tail_scr[...] = x[ts - SUBLANES:ts, :]

    gr = []
    for gi in range(2):
        parts = [jnp.dot(xc[:, n * bd:(n + 1) * bd].astype(BF16), gw_ref[gi, n], preferred_element_type=F32)
                 for n in range(cbw // bd)]
        gr.append(jnp.concatenate(parts, axis=1) + gbias_ref[gi:gi + 1, :])
    r = jax.nn.sigmoid(gr[0])
    i_g = jax.nn.sigmoid(gr[1])
    nl = -lam_ref[...]
    softplus = jnp.maximum(nl, 0.0) + jnp.log1p(jnp.exp(-jnp.abs(nl)))
    log_a = -RG_C * r * softplus
    a = jnp.exp(log_a)
    u = jnp.sqrt(1.0 - jnp.exp(2.0 * log_a)) * (i_g * xc)

    row = lax.broadcasted_iota(I32, (ts, cbw), 0)
    sft = 1
    while sft < ts:
        keep = row >= sft
        a_sh = jnp.where(keep, pltpu.roll(a, sft, 0), 1.0)
        u_sh = jnp.where(keep, pltpu.roll(u, sft, 0), 0.0)
        u = u + a * u_sh
        a = a * a_sh
        sft *= 2
    hs = u + a * h_scr[...]
    h_scr[...] = hs[ts - 1:ts, :]
    o_ref[...] = (_gelu_tanh(gbr_ref[...]) * hs).astype(o_ref.dtype)


def _rglru_mixer(xf, mod_all, tab, gain, w_in, conv_w, conv_b, gate_w, gate_b, lam, w_out, *, batch, seq):
    d_rnn = conv_w.shape[1]
    bd = d_rnn // RG_BLOCKS
    tn = min(512, d_rnn)
    proj = _norm_matmul(xf, mod_all, tab, gain, w_in.astype(BF16), mod_row=3, seq=seq, tn=tn, out_dtype=F32)
    cbw = min(512, d_rnn)
    ncb = d_rnn // cbw
    ts = min(256, seq)
    nts = seq // ts
    kern = functools.partial(_rglru_kernel, ts=ts, bd=bd)
    y = pl.pallas_call(
        kern,
        grid=(batch, ncb, nts),
        in_specs=[pl.BlockSpec((ts, cbw), lambda b, c, t: (b * nts + t, c)),
                  pl.BlockSpec((ts, cbw), lambda b, c, t: (b * nts + t, ncb + c)),
                  pl.BlockSpec((CONV_WIDTH, cbw), lambda b, c, t: (0, c)),
                  pl.BlockSpec((1, cbw), lambda b, c, t: (0, c)),
                  pl.BlockSpec((2, cbw // bd, bd, bd), lambda b, c, t: (0, c, 0, 0)),
                  pl.BlockSpec((2, cbw), lambda b, c, t: (0, c)),
                  pl.BlockSpec((1, cbw), lambda b, c, t: (0, c))],
        out_specs=pl.BlockSpec((ts, cbw), lambda b, c, t: (b * nts + t, c)),
        out_shape=jax.ShapeDtypeStruct((batch * seq, d_rnn), BF16),
        scratch_shapes=[pltpu.VMEM((1, cbw), F32), pltpu.VMEM((SUBLANES, cbw), F32)],
        compiler_params=_cparams(("arbitrary", "arbitrary", "arbitrary")),
        name="rglru",
    )(proj, proj, conv_w, conv_b.reshape(1, d_rnn), gate_w.astype(BF16), gate_b, lam.reshape(1, d_rnn))
    return _matmul_res(y, w_out.astype(BF16), xf, mod_all, tab, mod_row=5, seq=seq)


def kernel(x, c, rel_bias, ada_w, ada_b, ada_table, norm_g, ffn_w_in, ffn_w_out, dsa_w_in, dsa_w_out, dsa_q_gain, dsa_k_gain, nsa_w_in, nsa_gate_b, nsa_w_out, nsa_q_gain, nsa_k_gain, nsa_cmp_pos, nsa_cmp_w1, nsa_cmp_w2, rg_w_in, rg_conv_w, rg_conv_b, rg_gate_w, rg_gate_b, rg_lambda, rg_w_out):
    B, S, D = x.shape
    depth = ada_table.shape[0]
    assert S % TK == 0 and S >= NEAR_TILES * TQ
    mod_all = _mod_all(c, ada_w, ada_b)
    strip, strip_amax, ustrip = _bias_strips(rel_bias)
    w1_all, w2_all, tf = _prep_ffn_weights(ffn_w_in, ffn_w_out)
    xf = x.reshape(B * S, D)
    for layer in range(depth):
        tab = ada_table[layer]
        xf = _ffn(xf, mod_all, tab, norm_g[layer, 0], w1_all, w2_all, tf, 2 * layer, mod_row=0, seq=S)
        kind, j = layer % 3, layer // 3
        if kind == 0:
            xf = _dsa_mixer(xf, mod_all, tab, norm_g[layer, 1], dsa_w_in[j], dsa_w_out[j], dsa_q_gain[j],
                            dsa_k_gain[j], strip, strip_amax, batch=B, seq=S)
        elif kind == 1:
            xf = _nsa_mixer(xf, mod_all, tab, norm_g[layer, 1], nsa_w_in[j], nsa_gate_b[j], nsa_w_out[j],
                            nsa_q_gain[j], nsa_k_gain[j], nsa_cmp_pos[j], nsa_cmp_w1[j], nsa_cmp_w2[j],
                            strip, strip_amax, ustrip, batch=B, seq=S)
        else:
            xf = _rglru_mixer(xf, mod_all, tab, norm_g[layer, 1], rg_w_in[j], rg_conv_w[j], rg_conv_b[j],
                              rg_gate_w[j], rg_gate_b[j], rg_lambda[j], rg_w_out[j], batch=B, seq=S)
        xf = _ffn(xf, mod_all, tab, norm_g[layer, 2], w1_all, w2_all, tf, 2 * layer + 1, mod_row=6, seq=S)
    return xf.reshape(B, S, D)
```

```python
import functools
import math

import numpy as np
import jax
import jax.numpy as jnp
from jax import lax
from jax.experimental import pallas as pl
from jax.experimental.pallas import tpu as pltpu

N_HEADS = 32
HEAD_DIM = 128
N_KV_HEADS = 4
FFN_RES = 0.5
N_MOD = 9
NUM_BUCKETS = 32
MAX_DISTANCE = 1024
EPS = 1e-6
NEG = -1e30
BIG = 1e30
IDX_HEADS = 16
IDX_DIM = 128
TOPK_KEYS = 256
CMP_BLOCK = 32
CMP_STRIDE = 16
SEL_BLOCK = 64
SEL_TOPN = 16
WINDOW = 512
RG_BLOCKS = 16
CONV_WIDTH = 4
RG_C = 8.0

LANES = 128
SUBLANES = 8
BF16_SUBLANES = 16
VMEM_LIMIT_BYTES = 56 * 1024 * 1024
BOUND_SLACK = 1.02

F32 = jnp.float32
BF16 = jnp.bfloat16
I32 = jnp.int32
INT_MIN = -2 ** 31
LOWEST = -3.0e38

TQ = 128
TK = 512
KPT = TK // TQ
SAFE_LOGIT = 60.0
LOG2E = 1.4426950408889634
FAR_DIST = int(math.ceil((NUM_BUCKETS // 2) * (MAX_DISTANCE / (NUM_BUCKETS // 2)) ** ((NUM_BUCKETS // 2 - 1) / (NUM_BUCKETS // 2)))) + 8
NEAR_TILES = -(-(FAR_DIST + TQ - 1) // TQ)
CPQ = TQ // CMP_STRIDE
CMP_LAST = CMP_STRIDE * (CPQ - 1) + CMP_BLOCK - 1
NEAR_CMP = -(-(-(-(FAR_DIST + CMP_LAST) // CMP_STRIDE)) // SUBLANES) * SUBLANES

STRIP_PAD = TK - TQ
STRIP_ROWS = NEAR_TILES * TQ + 2 * STRIP_PAD
NT_DIMS = (((1,), (1,)), ((), ()))
TN_DIMS = (((0,), (0,)), ((), ()))


def _round_up(n, m):
    return (n + m - 1) // m * m


def _cparams(sem):
    return pltpu.CompilerParams(dimension_semantics=sem, vmem_limit_bytes=VMEM_LIMIT_BYTES)


def _gelu_tanh(x):
    return 0.5 * x * (1.0 + jnp.tanh(0.7978845608028654 * (x + 0.044715 * x * x * x)))


def _mod_kernel(c_ref, w_ref, b_ref, o_ref):
    c = c_ref[...]
    s = c * jax.nn.sigmoid(c)
    o_ref[...] = jnp.dot(s.astype(BF16), w_ref[...].astype(BF16), preferred_element_type=F32) + b_ref[...]


def _mod_all(c, ada_w, ada_b):
    B, D = c.shape
    N = ada_w.shape[1]
    rows = BF16_SUBLANES
    tn = next(t for t in (512, 256, 128) if N % t == 0)
    cp = jnp.zeros((rows, D), F32).at[:B].set(c)
    out = pl.pallas_call(
        _mod_kernel,
        grid=(N // tn,),
        in_specs=[pl.BlockSpec((rows, D), lambda j: (0, 0)),
                  pl.BlockSpec((D, tn), lambda j: (0, j)),
                  pl.BlockSpec((1, tn), lambda j: (0, j))],
        out_specs=pl.BlockSpec((rows, tn), lambda j: (0, j)),
        out_shape=jax.ShapeDtypeStruct((rows, N), F32),
        compiler_params=_cparams(("arbitrary",)),
        name="ada_mod",
    )(cp, ada_w, ada_b.reshape(1, N))
    return out[:B].reshape(B, N_MOD, D)


def _norm_matmul_kernel(x_ref, mod_ref, tab_ref, g_ref, w_ref, gc_ref, *refs, mod_row, epilogue,
                        norm_tiles, n_aux, n_tiles):
    swiglu = epilogue == "swiglu"
    if swiglu:
        wu_ref, refs = refs[0], refs[1:]
    o_ref = refs[0]
    aux_refs = refs[1:1 + n_aux]
    y_scr = refs[1 + n_aux]
    j = pl.program_id(1)
    d_model = x_ref.shape[1]
    kc = min(512, d_model)

    def finish(h, u):
        if swiglu:
            o_ref[...] = (h * jax.nn.sigmoid(h) * u).astype(o_ref.dtype)
            return
        if norm_tiles:
            is_norm = functools.reduce(lambda a, b: a | b, [j == t for t in norm_tiles])

            @pl.when(is_norm)
            def _():
                for c in range(h.shape[1] // LANES):
                    hc = h[:, c * LANES:(c + 1) * LANES]
                    ms = jnp.mean(hc * hc, axis=-1, keepdims=True)
                    yc = hc * lax.rsqrt(ms + EPS) * gc_ref[:, c * LANES:(c + 1) * LANES]
                    o_ref[:, c * LANES:(c + 1) * LANES] = yc.astype(o_ref.dtype)

            @pl.when(jnp.logical_not(is_norm))
            def _():
                o_ref[...] = h.astype(o_ref.dtype)
        else:
            o_ref[...] = h.astype(o_ref.dtype)
        for a in range(n_aux):
            @pl.when(j == n_tiles - n_aux + a)
            def _(a=a):
                aux_refs[a][...] = h

    @pl.when(j == 0)
    def _():
        x = x_ref[...]
        inv = lax.rsqrt(jnp.sum(x * x, axis=-1, keepdims=True) * (1.0 / d_model) + EPS)
        mod = mod_ref[0]
        shift = mod[mod_row:mod_row + 1, :] + tab_ref[mod_row:mod_row + 1, :]
        gs = g_ref[...] * (1.0 + mod[mod_row + 1:mod_row + 2, :] + tab_ref[mod_row + 1:mod_row + 2, :])
        h = u = None
        for c0 in range(0, d_model, kc):
            yc = ((x[:, c0:c0 + kc] * inv) * gs[:, c0:c0 + kc] + shift[:, c0:c0 + kc]).astype(BF16)
            y_scr[:, c0:c0 + kc] = yc
            part = jnp.dot(yc, w_ref[c0:c0 + kc, :], preferred_element_type=F32)
            h = part if h is None else h + part
            if swiglu:
                part_u = jnp.dot(yc, wu_ref[c0:c0 + kc, :], preferred_element_type=F32)
                u = part_u if u is None else u + part_u
        finish(h, u)

    @pl.when(j != 0)
    def _():
        h = jnp.dot(y_scr[...], w_ref[...], preferred_element_type=F32)
        u = jnp.dot(y_scr[...], wu_ref[...], preferred_element_type=F32) if swiglu else None
        finish(h, u)


def _weight_spec(w, rows, cols, w_index, col_block):
    if w.ndim == 2:
        return pl.BlockSpec((rows, cols), lambda i, j: (0, col_block(j)))
    return pl.BlockSpec((None, rows, cols), lambda i, j: (w_index, 0, col_block(j)))


def _norm_matmul(x, mod_all, tab, gain, w, *, mod_row, seq, epilogue="plain", tn, tm=512,
                 out_dtype=BF16, gain_cols=None, norm_tiles=(), n_aux=0, w_index=0):
    T, D = x.shape
    swiglu = epilogue == "swiglu"
    n_tiles = w.shape[-1] // (2 * tn if swiglu else tn)
    n_out = n_tiles * tn
    tm = min(tm, seq)
    bpb = seq // tm
    if gain_cols is None:
        gain_cols = jnp.zeros((1, n_out), F32)
    out_shape = [jax.ShapeDtypeStruct((T, n_out), out_dtype)]
    out_specs = [pl.BlockSpec((tm, tn), lambda i, j: (i, j))]
    for _ in range(n_aux):
        out_shape.append(jax.ShapeDtypeStruct((T, tn), F32))
        out_specs.append(pl.BlockSpec((tm, tn), lambda i, j: (i, 0)))
    kern = functools.partial(_norm_matmul_kernel, mod_row=mod_row, epilogue=epilogue,
                             norm_tiles=tuple(norm_tiles), n_aux=n_aux, n_tiles=n_tiles)
    in_specs = [pl.BlockSpec((tm, D), lambda i, j: (i, 0)),
                pl.BlockSpec((1, N_MOD, D), lambda i, j: (i // bpb, 0, 0)),
                pl.BlockSpec((N_MOD, D), lambda i, j: (0, 0)),
                pl.BlockSpec((1, D), lambda i, j: (0, 0)),
                _weight_spec(w, D, tn, w_index, lambda j: j),
                pl.BlockSpec((1, tn), lambda i, j: (0, j))]
    args = [x, mod_all, tab, gain.reshape(1, D), w, gain_cols]
    if swiglu:
        in_specs.append(_weight_spec(w, D, tn, w_index, lambda j: n_tiles + j))
        args.append(w)
    res = pl.pallas_call(
        kern,
        grid=(T // tm, n_tiles),
        in_specs=in_specs,
        out_specs=out_specs,
        out_shape=out_shape,
        scratch_shapes=[pltpu.VMEM((tm, D), BF16)],
        compiler_params=_cparams(("arbitrary", "arbitrary")),
        name="norm_matmul_" + epilogue,
    )(*args)
    return res if n_aux else res[0]


def _matmul_res_kernel(a_ref, w_ref, x_ref, mod_ref, tab_ref, o_ref, *, mod_row, res_scale):
    h = jnp.dot(a_ref[...], w_ref[...], preferred_element_type=F32)
    gate = mod_ref[0, mod_row:mod_row + 1, :] + tab_ref[mod_row:mod_row + 1, :]
    if res_scale != 1.0:
        gate = res_scale * gate
    o_ref[...] = x_ref[...] + gate * h


def _matmul_res(a, w, x, mod_all, tab, *, mod_row, seq, res_scale=1.0, tm=512, tn=1024, w_index=0):
    T, K = a.shape
    D = w.shape[-1]
    tm = min(tm, seq)
    tn = min(tn, D)
    bpb = seq // tm
    kern = functools.partial(_matmul_res_kernel, mod_row=mod_row, res_scale=res_scale)
    return pl.pallas_call(
        kern,
        grid=(T // tm, D // tn),
        in_specs=[pl.BlockSpec((tm, K), lambda i, j: (i, 0)),
                  _weight_spec(w, K, tn, w_index, lambda j: j),
                  pl.BlockSpec((tm, tn), lambda i, j: (i, j)),
                  pl.BlockSpec((1, N_MOD, tn), lambda i, j: (i // bpb, 0, j)),
                  pl.BlockSpec((N_MOD, tn), lambda i, j: (0, j))],
        out_specs=pl.BlockSpec((tm, tn), lambda i, j: (i, j)),
        out_shape=jax.ShapeDtypeStruct((T, D), F32),
        compiler_params=_cparams(("arbitrary", "arbitrary")),
        name="matmul_residual",
    )(a, w, x, mod_all, tab)


def _ffn_tile(d_ff):
    return 512 if d_ff >= 512 else _round_up(d_ff, LANES)


def _pad_cols_kernel(x_ref, o_ref):
    f = x_ref.shape[1]
    o_ref[:, :f] = x_ref[...].astype(o_ref.dtype)
    if o_ref.shape[1] > f:
        o_ref[:, f:] = jnp.zeros((o_ref.shape[0], o_ref.shape[1] - f), o_ref.dtype)


def _pad_rows_kernel(x_ref, o_ref, *, rows_valid):
    tr = o_ref.shape[0]
    row = pl.program_id(1) * tr + lax.broadcasted_iota(I32, o_ref.shape, 0)
    o_ref[...] = jnp.where(row < rows_valid, x_ref[...], 0.0).astype(o_ref.dtype)


def _prep_ffn_weights(ffn_w_in, ffn_w_out):
    L, two, D, two_f = ffn_w_in.shape
    F = two_f // 2
    tf = _ffn_tile(F)
    Fp = _round_up(F, tf)
    rows = L * two * D
    tr = 256
    w1 = pl.pallas_call(
        _pad_cols_kernel,
        grid=(rows // tr, 2),
        in_specs=[pl.BlockSpec((tr, F), lambda r, h: (r, h))],
        out_specs=pl.BlockSpec((tr, Fp), lambda r, h: (r, h)),
        out_shape=jax.ShapeDtypeStruct((rows, 2 * Fp), BF16),
        compiler_params=_cparams(("arbitrary", "arbitrary")),
        name="ffn_w_in_prep",
    )(ffn_w_in.reshape(rows, two_f))
    w2 = pl.pallas_call(
        functools.partial(_pad_rows_kernel, rows_valid=F),
        grid=(L * two, Fp // tf),
        in_specs=[pl.BlockSpec((None, tf, D), lambda l, r: (l, r, 0))],
        out_specs=pl.BlockSpec((None, tf, D), lambda l, r: (l, r, 0)),
        out_shape=jax.ShapeDtypeStruct((L * two, Fp, D), BF16),
        compiler_params=_cparams(("arbitrary", "arbitrary")),
        name="ffn_w_out_prep",
    )(ffn_w_out.reshape(L * two, F, D))
    return w1.reshape(L * two, D, 2 * Fp), w2, tf


def _ffn(x, mod_all, tab, gain, w1, w2, tf, w_index, *, mod_row, seq):
    a = _norm_matmul(x, mod_all, tab, gain, w1, mod_row=mod_row, seq=seq, epilogue="swiglu", tn=tf, w_index=w_index)
    return _matmul_res(a, w2, x, mod_all, tab, mod_row=mod_row + 2, seq=seq, res_scale=FFN_RES, w_index=w_index)


def _t5_bucket(dist):
    n = jnp.maximum(dist, 0)
    max_exact = NUM_BUCKETS // 2
    nf = jnp.maximum(n, 1).astype(F32)
    large = max_exact + (jnp.log(nf / max_exact) / math.log(MAX_DISTANCE / max_exact)
                         * (NUM_BUCKETS - max_exact)).astype(I32)
    large = jnp.minimum(large, NUM_BUCKETS - 1)
    return jnp.where(n < max_exact, n, large)


def _bias_strip_kernel(rb_ref, bkt_ref, o_ref, amax_ref):
    h = pl.program_id(0)
    bk = bkt_ref[...]
    far = rb_ref[NUM_BUCKETS - 1, h]
    acc = jnp.zeros(bk.shape, F32)
    for k in range(NUM_BUCKETS - 1):
        acc = jnp.where(bk == k, LOG2E * (rb_ref[k, h] - far), acc)
    o_ref[...] = acc
    amax_ref[...] = jnp.max(jnp.abs(acc), axis=0, keepdims=True)


def _bias_strip(rel_bias, bkt):
    rows = bkt.shape[0]
    H = rel_bias.shape[1]
    return pl.pallas_call(
        _bias_strip_kernel,
        grid=(H,),
        in_specs=[pl.BlockSpec(memory_space=pltpu.SMEM),
                  pl.BlockSpec((rows, TQ), lambda h: (0, 0))],
        out_specs=[pl.BlockSpec((rows, TQ), lambda h: (0, h)), pl.BlockSpec((1, TQ), lambda h: (0, h))],
        out_shape=[jax.ShapeDtypeStruct((rows, H * TQ), F32), jax.ShapeDtypeStruct((1, H * TQ), F32)],
        compiler_params=_cparams(("arbitrary",)),
        name="bias_strip",
    )(rel_bias, bkt)


def _bias_strips(rel_bias):
    iq = jnp.arange(TQ, dtype=I32)[None, :]
    x = jnp.arange(-STRIP_PAD, NEAR_TILES * TQ + STRIP_PAD, dtype=I32)[:, None]
    bkt_tok = _t5_bucket(iq - x + (NEAR_TILES - 1) * TQ)
    y = jnp.arange(NEAR_CMP, dtype=I32)[:, None]
    bkt_cmp = _t5_bucket(iq + CMP_STRIDE * (NEAR_CMP - 1 - y) - CMP_LAST)
    strip, strip_amax = _bias_strip(rel_bias, bkt_tok)
    ustrip, _ = _bias_strip(rel_bias, bkt_cmp)
    return strip, strip_amax, ustrip


def _stack_heads(q_ref, qs_scr, n):
    for r in range(n):
        qs_scr[r * TQ:(r + 1) * TQ, :] = q_ref[:, r * LANES:(r + 1) * LANES]


def _key_minus_query(rows):
    return lax.broadcasted_iota(I32, (rows, TQ), 0) - lax.broadcasted_iota(I32, (rows, TQ), 1)


def _num_far_tiles(i):
    return jnp.maximum(i - NEAR_TILES + 1, 0) // KPT


def _strip_row(i, kt):
    return pl.multiple_of((KPT + NEAR_TILES - 2 - (i - kt * KPT)) * TQ, TQ)


def _flash_init(m_scr, l_scr, acc_scr):
    m_scr[...] = jnp.full(m_scr.shape, NEG, F32)
    l_scr[...] = jnp.zeros(l_scr.shape, F32)
    acc_scr[...] = jnp.zeros(acc_scr.shape, F32)


def _max_row_sumsq(k_ref):
    rows = k_ref.shape[0]
    ch = min(rows, 1024)

    def body(c, best):
        kk = k_ref[pl.ds(pl.multiple_of(c * ch, ch), ch), :].astype(F32)
        return jnp.maximum(best, jnp.max(jnp.sum(kk * kk, axis=1, keepdims=True)))

    return lax.fori_loop(0, rows // ch, body, jnp.float32(0.0))


def _logits_bounded(qs_scr, k_sumsq, amax_row):
    qf = qs_scr[...].astype(F32)
    q_sumsq = lax.dot_general(jnp.ones((SUBLANES, HEAD_DIM), BF16), (qf * qf).astype(BF16), NT_DIMS,
                              preferred_element_type=F32)[0:1, :]
    bound = jnp.sqrt(q_sumsq * k_sumsq) * BOUND_SLACK + amax_row
    return jnp.max(bound) <= SAFE_LOGIT


def _flash_tile(k_tile, v_tile, qs_scr, madd, strip_ref, strip_row, m_scr, l_scr, acc_scr, n_heads, online):
    s = lax.dot_general(k_tile, qs_scr[...], NT_DIMS, preferred_element_type=F32) + jnp.tile(madd, (1, n_heads))
    if strip_ref is not None:
        s = s + strip_ref[pl.ds(strip_row, k_tile.shape[0]), :]
    if not online:
        p = jnp.exp2(s)
        l_scr[...] += jnp.sum(p, axis=0, keepdims=True)
        acc_scr[...] += lax.dot_general(v_tile, p.astype(BF16), TN_DIMS, preferred_element_type=F32)
        return
    m_prev = m_scr[...]
    m_new = jnp.maximum(m_prev, jnp.max(s, axis=0, keepdims=True))
    alpha = jnp.exp2(m_prev - m_new)
    p = jnp.exp2(s - m_new)
    l_scr[...] = alpha * l_scr[...] + jnp.sum(p, axis=0, keepdims=True)
    pv = lax.dot_general(v_tile, p.astype(BF16), TN_DIMS, preferred_element_type=F32)
    acc_scr[...] = alpha * acc_scr[...] + pv
    m_scr[...] = m_new


def _far_tiles(n_far, tile, online):
    if online:
        def body(kt, carry):
            tile(pl.multiple_of(kt * TK, TK), TK)
            return carry

        lax.fori_loop(0, n_far, body, 0)
        return

    def pair_body(u, carry):
        tile(pl.multiple_of(u * 2 * TK, 2 * TK), 2 * TK)
        return carry

    lax.fori_loop(0, n_far // 2, pair_body, 0)

    @pl.when(n_far % 2 == 1)
    def _():
        tile(pl.multiple_of((n_far - 1) * TK, TK), TK)


def _near_tiles(n_far, kt_last, tile, online):
    assert (NEAR_TILES + KPT - 2) // KPT + 1 <= 3
    if online:
        def body(kt, carry):
            tile(kt, TK)
            return carry

        lax.fori_loop(n_far, kt_last + 1, body, 0)
        return
    n_near = kt_last + 1 - n_far

    @pl.when(n_near == 3)
    def _():
        tile(kt_last - 2, TK)

    @pl.when(n_near >= 2)
    def _():
        tile(kt_last - 1, 2 * TK)

    @pl.when(n_near == 1)
    def _():
        tile(kt_last, TK)


def _flash_sweep(bounded, sweep, m_scr, l_scr, acc_scr):
    _flash_init(m_scr, l_scr, acc_scr)

    @pl.when(bounded)
    def _():
        sweep(False)

    @pl.when(jnp.logical_not(bounded))
    def _():
        sweep(True)
        l_scr[...] = jnp.where(m_scr[...] > 0.5 * NEG, l_scr[...], 0.0)

    l = l_scr[...]
    return acc_scr[...] * jnp.where(l > 0.0, 1.0 / l, 0.0)


def _write_heads(o_ref, o_t, n):
    for r in range(n):
        o_ref[:, r * LANES:(r + 1) * LANES] = o_t[:, r * TQ:(r + 1) * TQ].T.astype(o_ref.dtype)


def _dsa_index_kernel(qi_ref, ki_ref, wi_ref, mask_ref, keys_scr, qis_scr, j_scr, *, n_keep, ih, seq):
    i = pl.program_id(1)
    nkt = i // KPT + 1
    _stack_heads(qi_ref, qis_scr, ih)
    w_t = wi_ref[...].T * (ih ** -0.5 * IDX_DIM ** -0.5)
    w_rows = [w_t[h:h + 1, :] for h in range(ih)]
    q_pos = i * TQ + lax.broadcasted_iota(I32, (TQ, TQ), 1)
    k_iota = lax.broadcasted_iota(I32, (TQ, TQ), 0)

    def sub_rows(kt, u):
        return pl.multiple_of(kt * TK + u * TQ, TQ)

    def score_tile(kt, carry):
        k0 = pl.multiple_of(kt * TK, TK)
        d = lax.dot_general(ki_ref[pl.ds(k0, TK), :], qis_scr[...], NT_DIMS, preferred_element_type=F32)
        for u in range(KPT):
            sc = jnp.zeros((TQ, TQ), F32)
            for h in range(ih):
                sc = sc + jnp.maximum(d[u * TQ:(u + 1) * TQ, h * TQ:(h + 1) * TQ], 0.0) * w_rows[h]
            bits = pltpu.bitcast(sc, I32)
            key = jnp.where(bits < 0, bits ^ 0x7FFFFFFF, bits)
            r0 = sub_rows(kt, u)
            keys_scr[pl.ds(r0, TQ), :] = jnp.where(r0 + k_iota <= q_pos, key, INT_MIN)
        return carry

    lax.fori_loop(0, nkt, score_tile, 0)

    def count(pred):
        def body(kt, acc):
            for u in range(KPT):
                r0 = sub_rows(kt, u)
                acc = acc + jnp.where(pred(keys_scr[pl.ds(r0, TQ), :], r0 + k_iota), 1.0, 0.0)
            return acc
        acc = lax.fori_loop(0, nkt, body, jnp.zeros((TQ, TQ), F32))
        return jnp.sum(acc, axis=0, keepdims=True)

    def bit_body(b, lo):
        cand = lo + jnp.left_shift(jnp.int32(1), 31 - b)
        cnt = count(lambda kk, idx: kk >= cand)
        return jnp.where(cnt >= n_keep, cand, lo)

    thr = lax.fori_loop(0, 32, bit_body, jnp.full((1, TQ), INT_MIN, I32))
    cnt_ge = count(lambda kk, idx: kk >= thr)
    has_tie = (cnt_ge > n_keep) & (thr > INT_MIN)
    j_scr[...] = jnp.full((1, TQ), seq, I32)

    @pl.when(jnp.max(jnp.where(has_tie, 1.0, 0.0)) > 0.5)
    def _():
        need = n_keep - count(lambda kk, idx: kk > thr)

        def jb(b, lo):
            cand = lo + jnp.left_shift(jnp.int32(1), (seq.bit_length() - 2) - b)
            c = count(lambda kk, idx: (kk == thr) & (idx < cand))
            return jnp.where(c < need, cand, lo)

        j_last = lax.fori_loop(0, seq.bit_length() - 1, jb, jnp.zeros((1, TQ), I32))
        j_scr[...] = jnp.where(has_tie, j_last, seq)

    thr_c = jnp.maximum(thr, INT_MIN + 1)
    j_last = j_scr[...]

    def write_tile(kt, carry):
        for u in range(KPT):
            r0 = sub_rows(kt, u)
            kk = keys_scr[pl.ds(r0, TQ), :]
            sel = (kk > thr_c) | ((kk == thr_c) & (r0 + k_iota <= j_last))
            mask_ref[pl.ds(r0, TQ), :] = jnp.where(sel, 0.0, NEG).astype(BF16)
        return carry

    lax.fori_loop(0, nkt, write_tile, 0)

    def fill_tile(kt, carry):
        k0 = pl.multiple_of(kt * TK, TK)
        mask_ref[pl.ds(k0, TK), :] = jnp.full((TK, TQ), NEG, BF16)
        return carry

    lax.fori_loop(nkt, seq // TK, fill_tile, 0)


def _dsa_index(proj, aux, *, batch, seq, qi_blk, ki_blk, n_keep):
    nq = seq // TQ
    ihw = IDX_HEADS * IDX_DIM
    kern = functools.partial(_dsa_index_kernel, n_keep=n_keep, ih=IDX_HEADS, seq=seq)
    return pl.pallas_call(
        kern,
        grid=(batch, nq),
        in_specs=[pl.BlockSpec((TQ, ihw), lambda b, i: (b * nq + i, qi_blk)),
                  pl.BlockSpec((seq, IDX_DIM), lambda b, i: (b, ki_blk)),
                  pl.BlockSpec((TQ, LANES), lambda b, i: (b * nq + i, 1))],
        out_specs=pl.BlockSpec((None, seq, TQ), lambda b, i: (b, 0, i)),
        out_shape=jax.ShapeDtypeStruct((batch, seq, seq), BF16),
        scratch_shapes=[pltpu.VMEM((seq, TQ), I32),
                        pltpu.VMEM((IDX_HEADS * TQ, IDX_DIM), BF16),
                        pltpu.VMEM((1, TQ), I32)],
        compiler_params=_cparams(("arbitrary", "arbitrary")),
        name="dsa_index",
    )(proj, proj, aux)


def _dsa_attn_kernel(q_ref, k_ref, v_ref, mask_ref, strip_ref, amax_ref, o_ref, qs_scr, m_scr, l_scr, acc_scr,
                     ksq_scr, *, r_heads):
    i = pl.program_id(2)

    @pl.when(i == 0)
    def _():
        ksq_scr[0] = _max_row_sumsq(k_ref)

    _stack_heads(q_ref, qs_scr, r_heads)
    n_far = _num_far_tiles(i)

    def sweep(online):
        def far_tile(k0, rows):
            _flash_tile(k_ref[pl.ds(k0, rows), :], v_ref[pl.ds(k0, rows), :], qs_scr,
                        mask_ref[pl.ds(k0, rows), :].astype(F32), None, None, m_scr, l_scr, acc_scr, r_heads, online)

        def near_tile(kt, rows):
            k0 = pl.multiple_of(kt * TK, TK)
            _flash_tile(k_ref[pl.ds(k0, rows), :], v_ref[pl.ds(k0, rows), :], qs_scr,
                        mask_ref[pl.ds(k0, rows), :].astype(F32), strip_ref, _strip_row(i, kt),
                        m_scr, l_scr, acc_scr, r_heads, online)

        _far_tiles(n_far, far_tile, online)
        _near_tiles(n_far, i // KPT, near_tile, online)

    bounded = _logits_bounded(qs_scr, ksq_scr[0], amax_ref[...])
    _write_heads(o_ref, _flash_sweep(bounded, sweep, m_scr, l_scr, acc_scr), r_heads)


def _dsa_attn(proj, mask, strip, strip_amax, *, batch, seq, k_blk, v_blk):
    nq = seq // TQ
    G = N_KV_HEADS
    R = N_HEADS // G
    W = R * TQ
    kern = functools.partial(_dsa_attn_kernel, r_heads=R)
    return pl.pallas_call(
        kern,
        grid=(batch, G, nq),
        in_specs=[pl.BlockSpec((TQ, R * HEAD_DIM), lambda b, g, i: (b * nq + i, g)),
                  pl.BlockSpec((seq, HEAD_DIM), lambda b, g, i: (b, k_blk + g)),
                  pl.BlockSpec((seq, HEAD_DIM), lambda b, g, i: (b, v_blk + g)),
                  pl.BlockSpec((None, seq, TQ), lambda b, g, i: (b, 0, i)),
                  pl.BlockSpec((STRIP_ROWS, W), lambda b, g, i: (0, g)),
                  pl.BlockSpec((1, W), lambda b, g, i: (0, g))],
        out_specs=pl.BlockSpec((TQ, R * HEAD_DIM), lambda b, g, i: (b * nq + i, g)),
        out_shape=jax.ShapeDtypeStruct((batch * seq, N_HEADS * HEAD_DIM), BF16),
        scratch_shapes=[pltpu.VMEM((W, HEAD_DIM), BF16),
                        pltpu.VMEM((1, W), F32), pltpu.VMEM((1, W), F32), pltpu.VMEM((HEAD_DIM, W), F32),
                        pltpu.SMEM((1,), F32)],
        compiler_params=_cparams(("arbitrary", "arbitrary", "arbitrary")),
        name="dsa_attn",
    )(proj, proj, proj, mask, strip, strip_amax)


def _proj_layout(w_in, segments, tn):
    cols = []
    for start, width in segments:
        pad = _round_up(width, tn) - width
        cols.append(jnp.pad(w_in[:, start:start + width], ((0, 0), (0, pad))))
    return jnp.concatenate(cols, axis=1).astype(BF16)


def _dsa_mixer(xf, mod_all, tab, gain, w_in, w_out, q_gain, k_gain, strip, strip_amax, *, batch, seq):
    AW, KW = N_HEADS * HEAD_DIM, N_KV_HEADS * HEAD_DIM
    IW = IDX_HEADS * IDX_DIM
    tn = KW
    wp = _proj_layout(w_in, [(0, AW), (AW + 2 * KW, IW), (AW, KW), (AW + KW, KW), (AW + 2 * KW + IW, IDX_DIM + IDX_HEADS)], tn)
    nq_t, ni_t = AW // tn, IW // tn
    gain_cols = jnp.concatenate([jnp.tile(q_gain * (HEAD_DIM ** -0.5 * LOG2E), N_HEADS), jnp.zeros((IW,), F32),
                                 jnp.tile(k_gain, N_KV_HEADS), jnp.zeros((2 * tn,), F32)]).reshape(1, -1)
    proj, aux = _norm_matmul(xf, mod_all, tab, gain, wp, mod_row=3, seq=seq, tn=tn, gain_cols=gain_cols,
                             norm_tiles=tuple(range(nq_t)) + (nq_t + ni_t,), n_aux=1)
    k_blk = (nq_t + ni_t) * tn // HEAD_DIM
    mask = _dsa_index(proj, aux, batch=batch, seq=seq, qi_blk=AW // IW, ki_blk=(nq_t + ni_t + 2) * tn // IDX_DIM,
                      n_keep=min(TOPK_KEYS, seq // 4))
    o = _dsa_attn(proj, mask, strip, strip_amax, batch=batch, seq=seq, k_blk=k_blk, v_blk=k_blk + tn // HEAD_DIM)
    return _matmul_res(o, w_out.astype(BF16), xf, mod_all, tab, mod_row=5, seq=seq)


def _nsa_compress_kernel(xk_ref, xkn_ref, xv_ref, xvn_ref, pos_ref, w1_ref, w2_ref, gain_ref, ok_ref, ov_ref, *, groups, kw):
    half = CMP_BLOCK // 2
    rows = xk_ref.shape[0]
    row_id = lax.broadcasted_iota(I32, (rows, HEAD_DIM), 0)

    def branch(br, x_ref, xn_ref, o_ref):
        for g in range(groups):
            p1 = jnp.zeros((rows, HEAD_DIM), F32)
            p2 = jnp.zeros((rows, HEAD_DIM), F32)
            p2n = jnp.zeros((xn_ref.shape[0], HEAD_DIM), F32)
            for r in range(half):
                c0 = r * kw + g * HEAD_DIM
                xa = x_ref[:, c0:c0 + HEAD_DIM]
                w_lo = w1_ref[br, r * HEAD_DIM:(r + 1) * HEAD_DIM, :]
                w_hi = w1_ref[br, (half + r) * HEAD_DIM:(half + r + 1) * HEAD_DIM, :]
                p1 = p1 + jnp.dot((xa + pos_ref[br, r:r + 1, :]).astype(BF16), w_lo, preferred_element_type=F32)
                p2 = p2 + jnp.dot((xa + pos_ref[br, half + r:half + r + 1, :]).astype(BF16), w_hi,
                                  preferred_element_type=F32)
                xb = xn_ref[:, c0:c0 + HEAD_DIM]
                p2n = p2n + jnp.dot((xb + pos_ref[br, half + r:half + r + 1, :]).astype(BF16), w_hi,
                                    preferred_element_type=F32)
            p2s = jnp.where(row_id == rows - 1, p2n[0:1, :], pltpu.roll(p2, rows - 1, 0))
            hid = _gelu_tanh(p1 + p2s)
            out = jnp.dot(hid.astype(BF16), w2_ref[br], preferred_element_type=F32)
            if br == 0:
                ms = jnp.mean(out * out, axis=-1, keepdims=True)
                out = out * lax.rsqrt(ms + EPS) * gain_ref[...]
            o_ref[:, g * HEAD_DIM:(g + 1) * HEAD_DIM] = out.astype(o_ref.dtype)

    branch(0, xk_ref, xkn_ref, ok_ref)
    branch(1, xv_ref, xvn_ref, ov_ref)


def _nsa_compress(kc, vc, cmp_pos, cmp_w1, cmp_w2, k_gain0, *, batch, seq):
    assert CMP_BLOCK == 2 * CMP_STRIDE
    KW = N_KV_HEADS * HEAD_DIM
    ncp = seq // CMP_STRIDE
    rb = min(TQ, ncp)
    nrb = ncp // rb
    nxt = BF16_SUBLANES
    wide = CMP_STRIDE * KW
    xk = kc.reshape(batch * ncp, wide)
    xv = vc.reshape(batch * ncp, wide)
    last_nxt = batch * ncp // nxt - 1
    main = pl.BlockSpec((rb, wide), lambda b, r: (b * nrb + r, 0))
    ahead = pl.BlockSpec((nxt, wide), lambda b, r: (jnp.minimum((b * nrb + r + 1) * (rb // nxt), last_nxt), 0))
    kern = functools.partial(_nsa_compress_kernel, groups=N_KV_HEADS, kw=KW)
    return pl.pallas_call(
        kern,
        grid=(batch, nrb),
        in_specs=[main, ahead, main, ahead,
                  pl.BlockSpec((2, CMP_BLOCK, HEAD_DIM), lambda b, r: (0, 0, 0)),
                  pl.BlockSpec((2, CMP_BLOCK * HEAD_DIM, HEAD_DIM), lambda b, r: (0, 0, 0)),
                  pl.BlockSpec((2, HEAD_DIM, HEAD_DIM), lambda b, r: (0, 0, 0)),
                  pl.BlockSpec((1, HEAD_DIM), lambda b, r: (0, 0))],
        out_specs=[pl.BlockSpec((rb, KW), lambda b, r: (b * nrb + r, 0))] * 2,
        out_shape=[jax.ShapeDtypeStruct((batch * ncp, KW), BF16)] * 2,
        compiler_params=_cparams(("arbitrary", "arbitrary")),
        name="nsa_compress",
    )(xk, xk, xv, xv, cmp_pos, cmp_w1.astype(BF16), cmp_w2.astype(BF16), k_gain0.reshape(1, HEAD_DIM))


def _nsa_attn_kernel(q_ref, kc_ref, vc_ref, ks_ref, vs_ref, kw_ref, vw_ref, g_ref, gb_ref, strip_ref, amax_ref,
                     ustrip_ref, cover_ref, o_ref, qs_scr, lc_scr, sel_scr, gt_scr, m_scr, l_scr, acc_scr, ocmp_scr,
                     osel_scr, ksq_scr, *, r_heads, n_pick):
    g = pl.program_id(1)
    i = pl.program_id(2)
    R = r_heads

    @pl.when(i == 0)
    def _():
        ksq_scr[0] = _max_row_sumsq(ks_ref)
        ksq_scr[1] = _max_row_sumsq(kw_ref)

    ncp = kc_ref.shape[0]
    n_sel = cover_ref.shape[0]
    pad = NEAR_CMP - CPQ
    _stack_heads(q_ref, qs_scr, R)

    lc_scr[0:pad, :] = jnp.zeros((pad, R * TQ), F32)
    lc_scr[pad:pad + ncp, :] = lax.dot_general(kc_ref[...], qs_scr[...], NT_DIMS, preferred_element_type=F32)
    off = pl.multiple_of(i * CPQ, SUBLANES)
    lc_scr[pl.ds(off, NEAR_CMP), :] += ustrip_ref[...]
    c_iota = lax.broadcasted_iota(I32, (ncp, TQ), 0)
    t_pos = i * TQ + lax.broadcasted_iota(I32, (ncp, TQ), 1)
    vis = jnp.where(c_iota * CMP_STRIDE + (CMP_BLOCK - 1) <= t_pos, 0.0, NEG)
    s = lc_scr[pad:pad + ncp, :] + jnp.tile(vis, (1, R))
    m = jnp.max(s, axis=0, keepdims=True)
    p = jnp.exp2(s - m)
    inv = jnp.where(m > 0.5 * NEG, 1.0 / jnp.sum(p, axis=0, keepdims=True), 0.0)
    pc = p * inv
    ocmp_scr[...] = lax.dot_general(vc_ref[...], pc.astype(BF16), TN_DIMS, preferred_element_type=F32)

    psum = pc[:, 0:TQ]
    for r in range(1, R):
        psum = psum + pc[:, r * TQ:(r + 1) * TQ]
    hi = psum.astype(BF16)
    lo = (psum - hi.astype(F32)).astype(BF16)
    imp = (jnp.dot(cover_ref[...], hi, preferred_element_type=F32)
           + jnp.dot(cover_ref[...], lo, preferred_element_type=F32))
    n_io = lax.broadcasted_iota(I32, (n_sel, TQ), 0)
    t_sel = i * TQ + lax.broadcasted_iota(I32, (n_sel, TQ), 1)
    cur = jnp.right_shift(t_sel, SEL_BLOCK.bit_length() - 1)
    forced = (n_io == 0) | (n_io == cur) | (n_io == cur - 1)
    val0 = jnp.where(forced, BIG, jnp.where(n_io * SEL_BLOCK <= t_sel, imp, NEG))
    n_f = n_io.astype(F32)

    def pick_round(_, carry):
        val, selm = carry
        cm = jnp.max(val, axis=0, keepdims=True)
        idx = jnp.min(jnp.where(val == cm, n_f, float(n_sel)), axis=0, keepdims=True)
        pick = n_f == idx
        return jnp.where(pick, LOWEST, val), jnp.where(pick, 0.0, selm)

    _, selm = lax.fori_loop(0, n_pick, pick_round, (val0, jnp.full((n_sel, TQ), NEG, F32)))
    sel_scr[...] = selm

    n_far = _num_far_tiles(i)
    kt_last = i // KPT

    def sel_sweep(online):
        def block_mask(k0, rows):
            b0 = k0 // SEL_BLOCK
            return jnp.concatenate([jnp.broadcast_to(sel_scr[pl.ds(b0 + b, 1), :], (SEL_BLOCK, TQ))
                                    for b in range(rows // SEL_BLOCK)], axis=0)

        def far_tile(k0, rows):
            _flash_tile(ks_ref[pl.ds(k0, rows), :], vs_ref[pl.ds(k0, rows), :], qs_scr, block_mask(k0, rows),
                        None, None, m_scr, l_scr, acc_scr, R, online)

        def near_tile(kt, rows):
            k0 = pl.multiple_of(kt * TK, TK)
            bm = jnp.where(_key_minus_query(rows) <= (i - kt * KPT) * TQ, block_mask(k0, rows), NEG)
            _flash_tile(ks_ref[pl.ds(k0, rows), :], vs_ref[pl.ds(k0, rows), :], qs_scr, bm,
                        strip_ref, _strip_row(i, kt), m_scr, l_scr, acc_scr, R, online)

        _far_tiles(n_far, far_tile, online)
        _near_tiles(n_far, kt_last, near_tile, online)

    sel_bounded = _logits_bounded(qs_scr, ksq_scr[0], amax_ref[...])
    osel_scr[...] = _flash_sweep(sel_bounded, sel_sweep, m_scr, l_scr, acc_scr)

    def win_sweep(online):
        def win_tile(kt, rows):
            k0 = pl.multiple_of(kt * TK, TK)
            dist = (i - kt * KPT) * TQ - _key_minus_query(rows)
            vis_w = jnp.where((dist >= 0) & (dist < WINDOW), 0.0, NEG)
            _flash_tile(kw_ref[pl.ds(k0, rows), :], vw_ref[pl.ds(k0, rows), :], qs_scr, vis_w,
                        strip_ref, _strip_row(i, kt), m_scr, l_scr, acc_scr, R, online)

        if online:
            def win_body(kt, carry):
                win_tile(kt, TK)
                return carry

            lax.fori_loop(jnp.maximum(kt_last - 1, 0), kt_last + 1, win_body, 0)
        else:
            @pl.when(kt_last >= 1)
            def _():
                win_tile(kt_last - 1, 2 * TK)

            @pl.when(kt_last == 0)
            def _():
                win_tile(kt_last, TK)

    win_bounded = _logits_bounded(qs_scr, ksq_scr[1], amax_ref[...])
    o_win = _flash_sweep(win_bounded, win_sweep, m_scr, l_scr, acc_scr)

    gt_scr[...] = jax.nn.sigmoid(g_ref[:, 0:LANES] + gb_ref[...]).T

    def gate_row(br):
        rows = gt_scr[pl.ds(pl.multiple_of(br * N_HEADS + g * R, SUBLANES), R), :]
        return jnp.concatenate([rows[r:r + 1, :] for r in range(R)], axis=1)

    o_t = gate_row(0) * ocmp_scr[...] + gate_row(1) * osel_scr[...] + gate_row(2) * o_win
    _write_heads(o_ref, o_t, R)


def _nsa_attn(proj, aux_g, gate_b, k_cmp, v_cmp, strip, strip_amax, ustrip, *, batch, seq, blks):
    nq = seq // TQ
    G = N_KV_HEADS
    R = N_HEADS // G
    assert R == SUBLANES and 3 * N_HEADS <= LANES and WINDOW <= TK and WINDOW <= (NEAR_TILES - 1) * TQ
    W = R * TQ
    ncp = seq // CMP_STRIDE
    n_cmp = (seq - CMP_BLOCK) // CMP_STRIDE + 1
    n_sel = seq // SEL_BLOCK
    cs = np.arange(ncp)[None, :] * CMP_STRIDE
    ss = np.arange(n_sel)[:, None] * SEL_BLOCK
    cover = ((cs < ss + SEL_BLOCK) & (cs + CMP_BLOCK - 1 >= ss) & (np.arange(ncp)[None, :] < n_cmp))
    cover = jnp.asarray(cover.astype(np.float32), dtype=BF16)
    gb = jnp.zeros((1, LANES), F32).at[0, :3 * N_HEADS].set(gate_b)
    kv = lambda blk: pl.BlockSpec((seq, HEAD_DIM), lambda b, g, i: (b, blk + g))
    cmp_spec = pl.BlockSpec((ncp, HEAD_DIM), lambda b, g, i: (b, g))
    kern = functools.partial(_nsa_attn_kernel, r_heads=R, n_pick=min(SEL_TOPN, n_sel))
    return pl.pallas_call(
        kern,
        grid=(batch, G, nq),
        in_specs=[pl.BlockSpec((TQ, W), lambda b, g, i: (b * nq + i, g)),
                  cmp_spec, cmp_spec, kv(blks["ks"]), kv(blks["vs"]), kv(blks["kw"]), kv(blks["vw"]),
                  pl.BlockSpec((TQ, aux_g.shape[1]), lambda b, g, i: (b * nq + i, 0)),
                  pl.BlockSpec((1, LANES), lambda b, g, i: (0, 0)),
                  pl.BlockSpec((STRIP_ROWS, W), lambda b, g, i: (0, g)),
                  pl.BlockSpec((1, W), lambda b, g, i: (0, g)),
                  pl.BlockSpec((NEAR_CMP, W), lambda b, g, i: (0, g)),
                  pl.BlockSpec((n_sel, ncp), lambda b, g, i: (0, 0))],
        out_specs=pl.BlockSpec((TQ, W), lambda b, g, i: (b * nq + i, g)),
        out_shape=jax.ShapeDtypeStruct((batch * seq, N_HEADS * HEAD_DIM), BF16),
        scratch_shapes=[pltpu.VMEM((W, HEAD_DIM), BF16),
                        pltpu.VMEM((NEAR_CMP - CPQ + ncp, W), F32),
                        pltpu.VMEM((n_sel, TQ), F32),
                        pltpu.VMEM((LANES, TQ), F32),
                        pltpu.VMEM((1, W), F32), pltpu.VMEM((1, W), F32), pltpu.VMEM((HEAD_DIM, W), F32),
                        pltpu.VMEM((HEAD_DIM, W), F32), pltpu.VMEM((HEAD_DIM, W), F32),
                        pltpu.SMEM((2,), F32)],
        compiler_params=_cparams(("arbitrary", "arbitrary", "arbitrary")),
        name="nsa_attn",
    )(proj, k_cmp, v_cmp, proj, proj, proj, proj, aux_g, gb, strip, strip_amax, ustrip, cover)


def _nsa_mixer(xf, mod_all, tab, gain, w_in, gate_b, w_out, q_gain, k_gain, cmp_pos, cmp_w1, cmp_w2, strip, strip_amax,
               ustrip, *, batch, seq):
    AW, KW = N_HEADS * HEAD_DIM, N_KV_HEADS * HEAD_DIM
    tn = KW
    wp = _proj_layout(w_in, [(0, AW), (AW + 2 * KW, KW), (AW + 3 * KW, KW), (AW + 4 * KW, KW), (AW + 5 * KW, KW),
                             (AW, KW), (AW + KW, KW), (AW + 6 * KW, 3 * N_HEADS)], tn)
    nq_t = AW // tn
    gain_cols = jnp.concatenate([jnp.tile(q_gain * (HEAD_DIM ** -0.5 * LOG2E), N_HEADS), jnp.tile(k_gain[1], N_KV_HEADS),
                                 jnp.zeros((tn,), F32), jnp.tile(k_gain[2], N_KV_HEADS),
                                 jnp.zeros((4 * tn,), F32)]).reshape(1, -1)
    proj, kc, vc, aux_g = _norm_matmul(xf, mod_all, tab, gain, wp, mod_row=3, seq=seq, tn=tn, gain_cols=gain_cols,
                                       norm_tiles=tuple(range(nq_t)) + (nq_t, nq_t + 2), n_aux=3)
    k_cmp, v_cmp = _nsa_compress(kc, vc, cmp_pos, cmp_w1, cmp_w2, k_gain[0], batch=batch, seq=seq)
    per = tn // HEAD_DIM
    blks = {"ks": nq_t * per, "vs": (nq_t + 1) * per, "kw": (nq_t + 2) * per, "vw": (nq_t + 3) * per}
    o = _nsa_attn(proj, aux_g, gate_b, k_cmp, v_cmp, strip, strip_amax, ustrip, batch=batch, seq=seq, blks=blks)
    return _matmul_res(o, w_out.astype(BF16), xf, mod_all, tab, mod_row=5, seq=seq)


def _rglru_kernel(gbr_ref, xr_ref, cw_ref, cb_ref, gw_ref, gbias_ref, lam_ref, o_ref, h_scr, tail_scr, *, ts, bd):
    @pl.when(pl.program_id(2) == 0)
    def _():
        h_scr[...] = jnp.zeros(h_scr.shape, F32)
        tail_scr[...] = jnp.zeros(tail_scr.shape, F32)

    x = xr_ref[...]
    cbw = x.shape[1]
    xfull = jnp.concatenate([tail_scr[...], x], axis=0)
    xc = cw_ref[CONV_WIDTH - 1:CONV_WIDTH, :] * x + cb_ref[...]
    for w in range(1, CONV_WIDTH):
        xc = xc + cw_ref[CONV_WIDTH - 1 - w:CONV_WIDTH - w, :] * pltpu.roll(xfull, w, 0)[SUBLANES:, :]
    tail_scr[...] = x[ts - SUBLANES:ts, :]

    gr = []
    for gi in range(2):
        parts = [jnp.dot(xc[:, n * bd:(n + 1) * bd].astype(BF16), gw_ref[gi, n], preferred_element_type=F32)
                 for n in range(cbw // bd)]
        gr.append(jnp.concatenate(parts, axis=1) + gbias_ref[gi:gi + 1, :])
    r = jax.nn.sigmoid(gr[0])
    i_g = jax.nn.sigmoid(gr[1])
    nl = -lam_ref[...]
    softplus = jnp.maximum(nl, 0.0) + jnp.log1p(jnp.exp(-jnp.abs(nl)))
    log_a = -RG_C * r * softplus
    a = jnp.exp(log_a)
    u = jnp.sqrt(1.0 - jnp.exp(2.0 * log_a)) * (i_g * xc)

    row = lax.broadcasted_iota(I32, (ts, cbw), 0)
    sft = 1
    while sft < ts:
        keep = row >= sft
        a_sh = jnp.where(keep, pltpu.roll(a, sft, 0), 1.0)
        u_sh = jnp.where(keep, pltpu.roll(u, sft, 0), 0.0)
        u = u + a * u_sh
        a = a * a_sh
        sft *= 2
    hs = u + a * h_scr[...]
    h_scr[...] = hs[ts - 1:ts, :]
    o_ref[...] = (_gelu_tanh(gbr_ref[...]) * hs).astype(o_ref.dtype)


def _rglru_mixer(xf, mod_all, tab, gain, w_in, conv_w, conv_b, gate_w, gate_b, lam, w_out, *, batch, seq):
    d_rnn = conv_w.shape[1]
    bd = d_rnn // RG_BLOCKS
    tn = min(512, d_rnn)
    proj = _norm_matmul(xf, mod_all, tab, gain, w_in.astype(BF16), mod_row=3, seq=seq, tn=tn, out_dtype=F32)
    cbw = min(512, d_rnn)
    ncb = d_rnn // cbw
    ts = min(256, seq)
    nts = seq // ts
    kern = functools.partial(_rglru_kernel, ts=ts, bd=bd)
    y = pl.pallas_call(
        kern,
        grid=(batch, ncb, nts),
        in_specs=[pl.BlockSpec((ts, cbw), lambda b, c, t: (b * nts + t, c)),
                  pl.BlockSpec((ts, cbw), lambda b, c, t: (b * nts + t, ncb + c)),
                  pl.BlockSpec((CONV_WIDTH, cbw), lambda b, c, t: (0, c)),
                  pl.BlockSpec((1, cbw), lambda b, c, t: (0, c)),
                  pl.BlockSpec((2, cbw // bd, bd, bd), lambda b, c, t: (0, c, 0, 0)),
                  pl.BlockSpec((2, cbw), lambda b, c, t: (0, c)),
                  pl.BlockSpec((1, cbw), lambda b, c, t: (0, c))],
        out_specs=pl.BlockSpec((ts, cbw), lambda b, c, t: (b * nts + t, c)),
        out_shape=jax.ShapeDtypeStruct((batch * seq, d_rnn), BF16),
        scratch_shapes=[pltpu.VMEM((1, cbw), F32), pltpu.VMEM((SUBLANES, cbw), F32)],
        compiler_params=_cparams(("arbitrary", "arbitrary", "arbitrary")),
        name="rglru",
    )(proj, proj, conv_w, conv_b.reshape(1, d_rnn), gate_w.astype(BF16), gate_b, lam.reshape(1, d_rnn))
    return _matmul_res(y, w_out.astype(BF16), xf, mod_all, tab, mod_row=5, seq=seq)


def kernel(x, c, rel_bias, ada_w, ada_b, ada_table, norm_g, ffn_w_in, ffn_w_out, dsa_w_in, dsa_w_out, dsa_q_gain, dsa_k_gain, nsa_w_in, nsa_gate_b, nsa_w_out, nsa_q_gain, nsa_k_gain, nsa_cmp_pos, nsa_cmp_w1, nsa_cmp_w2, rg_w_in, rg_conv_w, rg_conv_b, rg_gate_w, rg_gate_b, rg_lambda, rg_w_out):
    B, S, D = x.shape
    depth = ada_table.shape[0]
    assert S % TK == 0 and S >= NEAR_TILES * TQ
    mod_all = _mod_all(c, ada_w, ada_b)
    strip, strip_amax, ustrip = _bias_strips(rel_bias)
    w1_all, w2_all, tf = _prep_ffn_weights(ffn_w_in, ffn_w_out)
    xf = x.reshape(B * S, D)
    for layer in range(depth):
        tab = ada_table[layer]
        xf = _ffn(xf, mod_all, tab, norm_g[layer, 0], w1_all, w2_all, tf, 2 * layer, mod_row=0, seq=S)
        kind, j = layer % 3, layer // 3
        if kind == 0:
            xf = _dsa_mixer(xf, mod_all, tab, norm_g[layer, 1], dsa_w_in[j], dsa_w_out[j], dsa_q_gain[j],
                            dsa_k_gain[j], strip, strip_amax, batch=B, seq=S)
        elif kind == 1:
            xf = _nsa_mixer(xf, mod_all, tab, norm_g[layer, 1], nsa_w_in[j], nsa_gate_b[j], nsa_w_out[j],
                            nsa_q_gain[j], nsa_k_gain[j], nsa_cmp_pos[j], nsa_cmp_w1[j], nsa_cmp_w2[j],
                            strip, strip_amax, ustrip, batch=B, seq=S)
        else:
            xf = _rglru_mixer(xf, mod_all, tab, norm_g[layer, 1], rg_w_in[j], rg_conv_w[j], rg_conv_b[j],
                              rg_gate_w[j], rg_gate_b[j], rg_lambda[j], rg_w_out[j], batch=B, seq=S)
        xf = _ffn(xf, mod_all, tab, norm_g[layer, 2], w1_all, w2_all, tf, 2 * layer + 1, mod_row=6, seq=S)
    return xf.reshape(B, S, D)
```

```python
import functools
import math

import numpy as np
import jax
import jax.numpy as jnp
from jax import lax
from jax.experimental import pallas as pl
from jax.experimental.pallas import tpu as pltpu

N_HEADS = 32
HEAD_DIM = 128
N_KV_HEADS = 4
FFN_RES = 0.5
N_MOD = 9
NUM_BUCKETS = 32
MAX_DISTANCE = 1024
EPS = 1e-6
NEG = -1e30
BIG = 1e30
IDX_HEADS = 16
IDX_DIM = 128
TOPK_KEYS = 256
CMP_BLOCK = 32
CMP_STRIDE = 16
SEL_BLOCK = 64
SEL_TOPN = 16
WINDOW = 512
RG_BLOCKS = 16
CONV_WIDTH = 4
RG_C = 8.0

LANES = 128
SUBLANES = 8
BF16_SUBLANES = 16
VMEM_LIMIT_BYTES = 56 * 1024 * 1024
BOUND_SLACK = 1.02

F32 = jnp.float32
BF16 = jnp.bfloat16
I32 = jnp.int32
INT_MIN = -2 ** 31
LOWEST = -3.0e38

TQ = 128
TK = 512
KPT = TK // TQ
SAFE_LOGIT = 60.0
LOG2E = 1.4426950408889634
FAR_DIST = int(math.ceil((NUM_BUCKETS // 2) * (MAX_DISTANCE / (NUM_BUCKETS // 2)) ** ((NUM_BUCKETS // 2 - 1) / (NUM_BUCKETS // 2)))) + 8
NEAR_TILES = -(-(FAR_DIST + TQ - 1) // TQ)
CPQ = TQ // CMP_STRIDE
CMP_LAST = CMP_STRIDE * (CPQ - 1) + CMP_BLOCK - 1
NEAR_CMP = -(-(-(-(FAR_DIST + CMP_LAST) // CMP_STRIDE)) // SUBLANES) * SUBLANES

STRIP_PAD = TK - TQ
STRIP_ROWS = NEAR_TILES * TQ + 2 * STRIP_PAD
NT_DIMS = (((1,), (1,)), ((), ()))
TN_DIMS = (((0,), (0,)), ((), ()))


def _round_up(n, m):
    return (n + m - 1) // m * m


def _cparams(sem):
    return pltpu.CompilerParams(dimension_semantics=sem, vmem_limit_bytes=VMEM_LIMIT_BYTES)


def _gelu_tanh(x):
    return 0.5 * x * (1.0 + jnp.tanh(0.7978845608028654 * (x + 0.044715 * x * x * x)))


def _mod_kernel(c_ref, w_ref, b_ref, o_ref):
    c = c_ref[...]
    s = c * jax.nn.sigmoid(c)
    o_ref[...] = jnp.dot(s.astype(BF16), w_ref[...].astype(BF16), preferred_element_type=F32) + b_ref[...]


def _mod_all(c, ada_w, ada_b):
    B, D = c.shape
    N = ada_w.shape[1]
    rows = BF16_SUBLANES
    tn = next(t for t in (512, 256, 128) if N % t == 0)
    cp = jnp.zeros((rows, D), F32).at[:B].set(c)
    out = pl.pallas_call(
        _mod_kernel,
        grid=(N // tn,),
        in_specs=[pl.BlockSpec((rows, D), lambda j: (0, 0)),
                  pl.BlockSpec((D, tn), lambda j: (0, j)),
                  pl.BlockSpec((1, tn), lambda j: (0, j))],
        out_specs=pl.BlockSpec((rows, tn), lambda j: (0, j)),
        out_shape=jax.ShapeDtypeStruct((rows, N), F32),
        compiler_params=_cparams(("arbitrary",)),
        name="ada_mod",
    )(cp, ada_w, ada_b.reshape(1, N))
    return out[:B].reshape(B, N_MOD, D)


def _norm_matmul_kernel(x_ref, mod_ref, tab_ref, g_ref, w_ref, gc_ref, *refs, mod_row, epilogue,
                        norm_tiles, n_aux, n_tiles):
    swiglu = epilogue == "swiglu"
    if swiglu:
        wu_ref, refs = refs[0], refs[1:]
    o_ref = refs[0]
    aux_refs = refs[1:1 + n_aux]
    y_scr = refs[1 + n_aux]
    j = pl.program_id(1)
    d_model = x_ref.shape[1]
    kc = min(512, d_model)

    def finish(h, u):
        if swiglu:
            o_ref[...] = (h * jax.nn.sigmoid(h) * u).astype(o_ref.dtype)
            return
        if norm_tiles:
            is_norm = functools.reduce(lambda a, b: a | b, [j == t for t in norm_tiles])

            @pl.when(is_norm)
            def _():
                for c in range(h.shape[1] // LANES):
                    hc = h[:, c * LANES:(c + 1) * LANES]
                    ms = jnp.mean(hc * hc, axis=-1, keepdims=True)
                    yc = hc * lax.rsqrt(ms + EPS) * gc_ref[:, c * LANES:(c + 1) * LANES]
                    o_ref[:, c * LANES:(c + 1) * LANES] = yc.astype(o_ref.dtype)

            @pl.when(jnp.logical_not(is_norm))
            def _():
                o_ref[...] = h.astype(o_ref.dtype)
        else:
            o_ref[...] = h.astype(o_ref.dtype)
        for a in range(n_aux):
            @pl.when(j == n_tiles - n_aux + a)
            def _(a=a):
                aux_refs[a][...] = h

    @pl.when(j == 0)
    def _():
        x = x_ref[...]
        inv = lax.rsqrt(jnp.sum(x * x, axis=-1, keepdims=True) * (1.0 / d_model) + EPS)
        mod = mod_ref[0]
        shift = mod[mod_row:mod_row + 1, :] + tab_ref[mod_row:mod_row + 1, :]
        gs = g_ref[...] * (1.0 + mod[mod_row + 1:mod_row + 2, :] + tab_ref[mod_row + 1:mod_row + 2, :])
        h = u = None
        for c0 in range(0, d_model, kc):
            yc = ((x[:, c0:c0 + kc] * inv) * gs[:, c0:c0 + kc] + shift[:, c0:c0 + kc]).astype(BF16)
            y_scr[:, c0:c0 + kc] = yc
            part = jnp.dot(yc, w_ref[c0:c0 + kc, :], preferred_element_type=F32)
            h = part if h is None else h + part
            if swiglu:
                part_u = jnp.dot(yc, wu_ref[c0:c0 + kc, :], preferred_element_type=F32)
                u = part_u if u is None else u + part_u
        finish(h, u)

    @pl.when(j != 0)
    def _():
        h = jnp.dot(y_scr[...], w_ref[...], preferred_element_type=F32)
        u = jnp.dot(y_scr[...], wu_ref[...], preferred_element_type=F32) if swiglu else None
        finish(h, u)


def _weight_spec(w, rows, cols, w_index, col_block):
    if w.ndim == 2:
        return pl.BlockSpec((rows, cols), lambda i, j: (0, col_block(j)))
    return pl.BlockSpec((None, rows, cols), lambda i, j: (w_index, 0, col_block(j)))


def _norm_matmul(x, mod_all, tab, gain, w, *, mod_row, seq, epilogue="plain", tn, tm=512,
                 out_dtype=BF16, gain_cols=None, norm_tiles=(), n_aux=0, w_index=0):
    T, D = x.shape
    swiglu = epilogue == "swiglu"
    n_tiles = w.shape[-1] // (2 * tn if swiglu else tn)
    n_out = n_tiles * tn
    tm = min(tm, seq)
    bpb = seq // tm
    if gain_cols is None:
        gain_cols = jnp.zeros((1, n_out), F32)
    out_shape = [jax.ShapeDtypeStruct((T, n_out), out_dtype)]
    out_specs = [pl.BlockSpec((tm, tn), lambda i, j: (i, j))]
    for _ in range(n_aux):
        out_shape.append(jax.ShapeDtypeStruct((T, tn), F32))
        out_specs.append(pl.BlockSpec((tm, tn), lambda i, j: (i, 0)))
    kern = functools.partial(_norm_matmul_kernel, mod_row=mod_row, epilogue=epilogue,
                             norm_tiles=tuple(norm_tiles), n_aux=n_aux, n_tiles=n_tiles)
    in_specs = [pl.BlockSpec((tm, D), lambda i, j: (i, 0)),
                pl.BlockSpec((1, N_MOD, D), lambda i, j: (i // bpb, 0, 0)),
                pl.BlockSpec((N_MOD, D), lambda i, j: (0, 0)),
                pl.BlockSpec((1, D), lambda i, j: (0, 0)),
                _weight_spec(w, D, tn, w_index, lambda j: j),
                pl.BlockSpec((1, tn), lambda i, j: (0, j))]
    args = [x, mod_all, tab, gain.reshape(1, D), w, gain_cols]
    if swiglu:
        in_specs.append(_weight_spec(w, D, tn, w_index, lambda j: n_tiles + j))
        args.append(w)
    res = pl.pallas_call(
        kern,
        grid=(T // tm, n_tiles),
        in_specs=in_specs,
        out_specs=out_specs,
        out_shape=out_shape,
        scratch_shapes=[pltpu.VMEM((tm, D), BF16)],
        compiler_params=_cparams(("arbitrary", "arbitrary")),
        name="norm_matmul_" + epilogue,
    )(*args)
    return res if n_aux else res[0]


def _matmul_res_kernel(a_ref, w_ref, x_ref, mod_ref, tab_ref, o_ref, *, mod_row, res_scale):
    h = jnp.dot(a_ref[...], w_ref[...], preferred_element_type=F32)
    gate = mod_ref[0, mod_row:mod_row + 1, :] + tab_ref[mod_row:mod_row + 1, :]
    if res_scale != 1.0:
        gate = res_scale * gate
    o_ref[...] = x_ref[...] + gate * h


def _matmul_res(a, w, x, mod_all, tab, *, mod_row, seq, res_scale=1.0, tm=512, tn=1024, w_index=0):
    T, K = a.shape
    D = w.shape[-1]
    tm = min(tm, seq)
    tn = min(tn, D)
    bpb = seq // tm
    kern = functools.partial(_matmul_res_kernel, mod_row=mod_row, res_scale=res_scale)
    return pl.pallas_call(
        kern,
        grid=(T // tm, D // tn),
        in_specs=[pl.BlockSpec((tm, K), lambda i, j: (i, 0)),
                  _weight_spec(w, K, tn, w_index, lambda j: j),
                  pl.BlockSpec((tm, tn), lambda i, j: (i, j)),
                  pl.BlockSpec((1, N_MOD, tn), lambda i, j: (i // bpb, 0, j)),
                  pl.BlockSpec((N_MOD, tn), lambda i, j: (0, j))],
        out_specs=pl.BlockSpec((tm, tn), lambda i, j: (i, j)),
        out_shape=jax.ShapeDtypeStruct((T, D), F32),
        compiler_params=_cparams(("arbitrary", "arbitrary")),
        name="matmul_residual",
    )(a, w, x, mod_all, tab)


def _ffn_tile(d_ff):
    return 512 if d_ff >= 512 else _round_up(d_ff, LANES)


def _pad_cols_kernel(x_ref, o_ref):
    f = x_ref.shape[1]
    o_ref[:, :f] = x_ref[...].astype(o_ref.dtype)
    if o_ref.shape[1] > f:
        o_ref[:, f:] = jnp.zeros((o_ref.shape[0], o_ref.shape[1] - f), o_ref.dtype)


def _pad_rows_kernel(x_ref, o_ref, *, rows_valid):
    tr = o_ref.shape[0]
    row = pl.program_id(1) * tr + lax.broadcasted_iota(I32, o_ref.shape, 0)
    o_ref[...] = jnp.where(row < rows_valid, x_ref[...], 0.0).astype(o_ref.dtype)


def _prep_ffn_weights(ffn_w_in, ffn_w_out):
    L, two, D, two_f = ffn_w_in.shape
    F = two_f // 2
    tf = _ffn_tile(F)
    Fp = _round_up(F, tf)
    rows = L * two * D
    tr = 256
    w1 = pl.pallas_call(
        _pad_cols_kernel,
        grid=(rows // tr, 2),
        in_specs=[pl.BlockSpec((tr, F), lambda r, h: (r, h))],
        out_specs=pl.BlockSpec((tr, Fp), lambda r, h: (r, h)),
        out_shape=jax.ShapeDtypeStruct((rows, 2 * Fp), BF16),
        compiler_params=_cparams(("arbitrary", "arbitrary")),
        name="ffn_w_in_prep",
    )(ffn_w_in.reshape(rows, two_f))
    w2 = pl.pallas_call(
        functools.partial(_pad_rows_kernel, rows_valid=F),
        grid=(L * two, Fp // tf),
        in_specs=[pl.BlockSpec((None, tf, D), lambda l, r: (l, r, 0))],
        out_specs=pl.BlockSpec((None, tf, D), lambda l, r: (l, r, 0)),
        out_shape=jax.ShapeDtypeStruct((L * two, Fp, D), BF16),
        compiler_params=_cparams(("arbitrary", "arbitrary")),
        name="ffn_w_out_prep",
    )(ffn_w_out.reshape(L * two, F, D))
    return w1.reshape(L * two, D, 2 * Fp), w2, tf


def _ffn(x, mod_all, tab, gain, w1, w2, tf, w_index, *, mod_row, seq):
    a = _norm_matmul(x, mod_all, tab, gain, w1, mod_row=mod_row, seq=seq, epilogue="swiglu", tn=tf, w_index=w_index)
    return _matmul_res(a, w2, x, mod_all, tab, mod_row=mod_row + 2, seq=seq, res_scale=FFN_RES, w_index=w_index)


def _t5_bucket(dist):
    n = jnp.maximum(dist, 0)
    max_exact = NUM_BUCKETS // 2
    nf = jnp.maximum(n, 1).astype(F32)
    large = max_exact + (jnp.log(nf / max_exact) / math.log(MAX_DISTANCE / max_exact)
                         * (NUM_BUCKETS - max_exact)).astype(I32)
    large = jnp.minimum(large, NUM_BUCKETS - 1)
    return jnp.where(n < max_exact, n, large)


def _bias_strip_kernel(rb_ref, bkt_ref, o_ref, amax_ref):
    h = pl.program_id(0)
    bk = bkt_ref[...]
    far = rb_ref[NUM_BUCKETS - 1, h]
    acc = jnp.zeros(bk.shape, F32)
    for k in range(NUM_BUCKETS - 1):
        acc = jnp.where(bk == k, LOG2E * (rb_ref[k, h] - far), acc)
    o_ref[...] = acc
    amax_ref[...] = jnp.max(jnp.abs(acc), axis=0, keepdims=True)


def _bias_strip(rel_bias, bkt):
    rows = bkt.shape[0]
    H = rel_bias.shape[1]
    return pl.pallas_call(
        _bias_strip_kernel,
        grid=(H,),
        in_specs=[pl.BlockSpec(memory_space=pltpu.SMEM),
                  pl.BlockSpec((rows, TQ), lambda h: (0, 0))],
        out_specs=[pl.BlockSpec((rows, TQ), lambda h: (0, h)), pl.BlockSpec((1, TQ), lambda h: (0, h))],
        out_shape=[jax.ShapeDtypeStruct((rows, H * TQ), F32), jax.ShapeDtypeStruct((1, H * TQ), F32)],
        compiler_params=_cparams(("arbitrary",)),
        name="bias_strip",
    )(rel_bias, bkt)


def _bias_strips(rel_bias):
    iq = jnp.arange(TQ, dtype=I32)[None, :]
    x = jnp.arange(-STRIP_PAD, NEAR_TILES * TQ + STRIP_PAD, dtype=I32)[:, None]
    bkt_tok = _t5_bucket(iq - x + (NEAR_TILES - 1) * TQ)
    y = jnp.arange(NEAR_CMP, dtype=I32)[:, None]
    bkt_cmp = _t5_bucket(iq + CMP_STRIDE * (NEAR_CMP - 1 - y) - CMP_LAST)
    strip, strip_amax = _bias_strip(rel_bias, bkt_tok)
    ustrip, _ = _bias_strip(rel_bias, bkt_cmp)
    return strip, strip_amax, ustrip


def _stack_heads(q_ref, qs_scr, n):
    for r in range(n):
        qs_scr[r * TQ:(r + 1) * TQ, :] = q_ref[:, r * LANES:(r + 1) * LANES]


def _key_minus_query(rows):
    return lax.broadcasted_iota(I32, (rows, TQ), 0) - lax.broadcasted_iota(I32, (rows, TQ), 1)


def _num_far_tiles(i):
    return jnp.maximum(i - NEAR_TILES + 1, 0) // KPT


def _strip_row(i, kt):
    return pl.multiple_of((KPT + NEAR_TILES - 2 - (i - kt * KPT)) * TQ, TQ)


def _flash_init(m_scr, l_scr, acc_scr):
    m_scr[...] = jnp.full(m_scr.shape, NEG, F32)
    l_scr[...] = jnp.zeros(l_scr.shape, F32)
    acc_scr[...] = jnp.zeros(acc_scr.shape, F32)


def _max_row_sumsq(k_ref):
    rows = k_ref.shape[0]
    ch = min(rows, 1024)

    def body(c, best):
        kk = k_ref[pl.ds(pl.multiple_of(c * ch, ch), ch), :].astype(F32)
        return jnp.maximum(best, jnp.max(jnp.sum(kk * kk, axis=1, keepdims=True)))

    return lax.fori_loop(0, rows // ch, body, jnp.float32(0.0))


def _logits_bounded(qs_scr, k_sumsq, amax_row):
    qf = qs_scr[...].astype(F32)
    q_sumsq = lax.dot_general(jnp.ones((SUBLANES, HEAD_DIM), BF16), (qf * qf).astype(BF16), NT_DIMS,
                              preferred_element_type=F32)[0:1, :]
    bound = jnp.sqrt(q_sumsq * k_sumsq) * BOUND_SLACK + amax_row
    return jnp.max(bound) <= SAFE_LOGIT


def _flash_tile(k_tile, v_tile, qs_scr, madd, strip_ref, strip_row, m_scr, l_scr, acc_scr, n_heads, online):
    s = lax.dot_general(k_tile, qs_scr[...], NT_DIMS, preferred_element_type=F32) + jnp.tile(madd, (1, n_heads))
    if strip_ref is not None:
        s = s + strip_ref[pl.ds(strip_row, k_tile.shape[0]), :]
    if not online:
        p = jnp.exp2(s)
        l_scr[...] += jnp.sum(p, axis=0, keepdims=True)
        acc_scr[...] += lax.dot_general(v_tile, p.astype(BF16), TN_DIMS, preferred_element_type=F32)
        return
    m_prev = m_scr[...]
    m_new = jnp.maximum(m_prev, jnp.max(s, axis=0, keepdims=True))
    alpha = jnp.exp2(m_prev - m_new)
    p = jnp.exp2(s - m_new)
    l_scr[...] = alpha * l_scr[...] + jnp.sum(p, axis=0, keepdims=True)
    pv = lax.dot_general(v_tile, p.astype(BF16), TN_DIMS, preferred_element_type=F32)
    acc_scr[...] = alpha * acc_scr[...] + pv
    m_scr[...] = m_new


def _far_tiles(n_far, tile, online):
    if online:
        def body(kt, carry):
            tile(pl.multiple_of(kt * TK, TK), TK)
            return carry

        lax.fori_loop(0, n_far, body, 0)
        return

    def pair_body(u, carry):
        tile(pl.multiple_of(u * 2 * TK, 2 * TK), 2 * TK)
        return carry

    lax.fori_loop(0, n_far // 2, pair_body, 0)

    @pl.when(n_far % 2 == 1)
    def _():
        tile(pl.multiple_of((n_far - 1) * TK, TK), TK)


def _near_tiles(n_far, kt_last, tile, online):
    assert (NEAR_TILES + KPT - 2) // KPT + 1 <= 3
    if online:
        def body(kt, carry):
            tile(kt, TK)
            return carry

        lax.fori_loop(n_far, kt_last + 1, body, 0)
        return
    n_near = kt_last + 1 - n_far

    @pl.when(n_near == 3)
    def _():
        tile(kt_last - 2, TK)

    @pl.when(n_near >= 2)
    def _():
        tile(kt_last - 1, 2 * TK)

    @pl.when(n_near == 1)
    def _():
        tile(kt_last, TK)


def _flash_sweep(bounded, sweep, m_scr, l_scr, acc_scr):
    _flash_init(m_scr, l_scr, acc_scr)

    @pl.when(bounded)
    def _():
        sweep(False)

    @pl.when(jnp.logical_not(bounded))
    def _():
        sweep(True)
        l_scr[...] = jnp.where(m_scr[...] > 0.5 * NEG, l_scr[...], 0.0)

    l = l_scr[...]
    return acc_scr[...] * jnp.where(l > 0.0, 1.0 / l, 0.0)


def _write_heads(o_ref, o_t, n):
    for r in range(n):
        o_ref[:, r * LANES:(r + 1) * LANES] = o_t[:, r * TQ:(r + 1) * TQ].T.astype(o_ref.dtype)


def _dsa_index_kernel(qi_ref, ki_ref, wi_ref, mask_ref, keys_scr, qis_scr, j_scr, *, n_keep, ih, seq):
    i = pl.program_id(1)
    nkt = i // KPT + 1
    _stack_heads(qi_ref, qis_scr, ih)
    w_t = wi_ref[...].T * (ih ** -0.5 * IDX_DIM ** -0.5)
    w_rows = [w_t[h:h + 1, :] for h in range(ih)]
    q_pos = i * TQ + lax.broadcasted_iota(I32, (TQ, TQ), 1)
    k_iota = lax.broadcasted_iota(I32, (TQ, TQ), 0)

    def sub_rows(kt, u):
        return pl.multiple_of(kt * TK + u * TQ, TQ)

    def score_tile(kt, carry):
        k0 = pl.multiple_of(kt * TK, TK)
        d = lax.dot_general(ki_ref[pl.ds(k0, TK), :], qis_scr[...], NT_DIMS, preferred_element_type=F32)
        for u in range(KPT):
            sc = jnp.zeros((TQ, TQ), F32)
            for h in range(ih):
                sc = sc + jnp.maximum(d[u * TQ:(u + 1) * TQ, h * TQ:(h + 1) * TQ], 0.0) * w_rows[h]
            bits = pltpu.bitcast(sc, I32)
            key = jnp.where(bits < 0, bits ^ 0x7FFFFFFF, bits)
            r0 = sub_rows(kt, u)
            keys_scr[pl.ds(r0, TQ), :] = jnp.where(r0 + k_iota <= q_pos, key, INT_MIN)
        return carry

    lax.fori_loop(0, nkt, score_tile, 0)

    def count(pred):
        def body(kt, acc):
            for u in range(KPT):
                r0 = sub_rows(kt, u)
                acc = acc + jnp.where(pred(keys_scr[pl.ds(r0, TQ), :], r0 + k_iota), 1.0, 0.0)
            return acc
        acc = lax.fori_loop(0, nkt, body, jnp.zeros((TQ, TQ), F32))
        return jnp.sum(acc, axis=0, keepdims=True)

    def bit_body(b, lo):
        cand = lo + jnp.left_shift(jnp.int32(1), 31 - b)
        cnt = count(lambda kk, idx: kk >= cand)
        return jnp.where(cnt >= n_keep, cand, lo)

    thr = lax.fori_loop(0, 32, bit_body, jnp.full((1, TQ), INT_MIN, I32))
    cnt_ge = count(lambda kk, idx: kk >= thr)
    has_tie = (cnt_ge > n_keep) & (thr > INT_MIN)
    j_scr[...] = jnp.full((1, TQ), seq, I32)

    @pl.when(jnp.max(jnp.where(has_tie, 1.0, 0.0)) > 0.5)
    def _():
        need = n_keep - count(lambda kk, idx: kk > thr)

        def jb(b, lo):
            cand = lo + jnp.left_shift(jnp.int32(1), (seq.bit_length() - 2) - b)
            c = count(lambda kk, idx: (kk == thr) & (idx < cand))
            return jnp.where(c < need, cand, lo)

        j_last = lax.fori_loop(0, seq.bit_length() - 1, jb, jnp.zeros((1, TQ), I32))
        j_scr[...] = jnp.where(has_tie, j_last, seq)

    thr_c = jnp.maximum(thr, INT_MIN + 1)
    j_last = j_scr[...]

    def write_tile(kt, carry):
        for u in range(KPT):
            r0 = sub_rows(kt, u)
            kk = keys_scr[pl.ds(r0, TQ), :]
            sel = (kk > thr_c) | ((kk == thr_c) & (r0 + k_iota <= j_last))
            mask_ref[pl.ds(r0, TQ), :] = jnp.where(sel, 0.0, NEG).astype(BF16)
        return carry

    lax.fori_loop(0, nkt, write_tile, 0)

    def fill_tile(kt, carry):
        k0 = pl.multiple_of(kt * TK, TK)
        mask_ref[pl.ds(k0, TK), :] = jnp.full((TK, TQ), NEG, BF16)
        return carry

    lax.fori_loop(nkt, seq // TK, fill_tile, 0)


def _dsa_index(proj, aux, *, batch, seq, qi_blk, ki_blk, n_keep):
    nq = seq // TQ
    ihw = IDX_HEADS * IDX_DIM
    kern = functools.partial(_dsa_index_kernel, n_keep=n_keep, ih=IDX_HEADS, seq=seq)
    return pl.pallas_call(
        kern,
        grid=(batch, nq),
        in_specs=[pl.BlockSpec((TQ, ihw), lambda b, i: (b * nq + i, qi_blk)),
                  pl.BlockSpec((seq, IDX_DIM), lambda b, i: (b, ki_blk)),
                  pl.BlockSpec((TQ, LANES), lambda b, i: (b * nq + i, 1))],
        out_specs=pl.BlockSpec((None, seq, TQ), lambda b, i: (b, 0, i)),
        out_shape=jax.ShapeDtypeStruct((batch, seq, seq), BF16),
        scratch_shapes=[pltpu.VMEM((seq, TQ), I32),
                        pltpu.VMEM((IDX_HEADS * TQ, IDX_DIM), BF16),
                        pltpu.VMEM((1, TQ), I32)],
        compiler_params=_cparams(("arbitrary", "arbitrary")),
        name="dsa_index",
    )(proj, proj, aux)


def _dsa_attn_kernel(q_ref, k_ref, v_ref, mask_ref, strip_ref, amax_ref, o_ref, qs_scr, m_scr, l_scr, acc_scr,
                     ksq_scr, *, r_heads):
    i = pl.program_id(2)

    @pl.when(i == 0)
    def _():
        ksq_scr[0] = _max_row_sumsq(k_ref)

    _stack_heads(q_ref, qs_scr, r_heads)
    n_far = _num_far_tiles(i)

    def sweep(online):
        def far_tile(k0, rows):
            _flash_tile(k_ref[pl.ds(k0, rows), :], v_ref[pl.ds(k0, rows), :], qs_scr,
                        mask_ref[pl.ds(k0, rows), :].astype(F32), None, None, m_scr, l_scr, acc_scr, r_heads, online)

        def near_tile(kt, rows):
            k0 = pl.multiple_of(kt * TK, TK)
            _flash_tile(k_ref[pl.ds(k0, rows), :], v_ref[pl.ds(k0, rows), :], qs_scr,
                        mask_ref[pl.ds(k0, rows), :].astype(F32), strip_ref, _strip_row(i, kt),
                        m_scr, l_scr, acc_scr, r_heads, online)

        _far_tiles(n_far, far_tile, online)
        _near_tiles(n_far, i // KPT, near_tile, online)

    bounded = _logits_bounded(qs_scr, ksq_scr[0], amax_ref[...])
    _write_heads(o_ref, _flash_sweep(bounded, sweep, m_scr, l_scr, acc_scr), r_heads)


def _dsa_attn(proj, mask, strip, strip_amax, *, batch, seq, k_blk, v_blk):
    nq = seq // TQ
    G = N_KV_HEADS
    R = N_HEADS // G
    W = R * TQ
    kern = functools.partial(_dsa_attn_kernel, r_heads=R)
    return pl.pallas_call(
        kern,
        grid=(batch, G, nq),
        in_specs=[pl.BlockSpec((TQ, R * HEAD_DIM), lambda b, g, i: (b * nq + i, g)),
                  pl.BlockSpec((seq, HEAD_DIM), lambda b, g, i: (b, k_blk + g)),
                  pl.BlockSpec((seq, HEAD_DIM), lambda b, g, i: (b, v_blk + g)),
                  pl.BlockSpec((None, seq, TQ), lambda b, g, i: (b, 0, i)),
                  pl.BlockSpec((STRIP_ROWS, W), lambda b, g, i: (0, g)),
                  pl.BlockSpec((1, W), lambda b, g, i: (0, g))],
        out_specs=pl.BlockSpec((TQ, R * HEAD_DIM), lambda b, g, i: (b * nq + i, g)),
        out_shape=jax.ShapeDtypeStruct((batch * seq, N_HEADS * HEAD_DIM), BF16),
        scratch_shapes=[pltpu.VMEM((W, HEAD_DIM), BF16),
                        pltpu.VMEM((1, W), F32), pltpu.VMEM((1, W), F32), pltpu.VMEM((HEAD_DIM, W), F32),
                        pltpu.SMEM((1,), F32)],
        compiler_params=_cparams(("arbitrary", "arbitrary", "arbitrary")),
        name="dsa_attn",
    )(proj, proj, proj, mask, strip, strip_amax)


def _proj_layout(w_in, segments, tn):
    cols = []
    for start, width in segments:
        pad = _round_up(width, tn) - width
        cols.append(jnp.pad(w_in[:, start:start + width], ((0, 0), (0, pad))))
    return jnp.concatenate(cols, axis=1).astype(BF16)


def _dsa_mixer(xf, mod_all, tab, gain, w_in, w_out, q_gain, k_gain, strip, strip_amax, *, batch, seq):
    AW, KW = N_HEADS * HEAD_DIM, N_KV_HEADS * HEAD_DIM
    IW = IDX_HEADS * IDX_DIM
    tn = KW
    wp = _proj_layout(w_in, [(0, AW), (AW + 2 * KW, IW), (AW, KW), (AW + KW, KW), (AW + 2 * KW + IW, IDX_DIM + IDX_HEADS)], tn)
    nq_t, ni_t = AW // tn, IW // tn
    gain_cols = jnp.concatenate([jnp.tile(q_gain * (HEAD_DIM ** -0.5 * LOG2E), N_HEADS), jnp.zeros((IW,), F32),
                                 jnp.tile(k_gain, N_KV_HEADS), jnp.zeros((2 * tn,), F32)]).reshape(1, -1)
    proj, aux = _norm_matmul(xf, mod_all, tab, gain, wp, mod_row=3, seq=seq, tn=tn, gain_cols=gain_cols,
                             norm_tiles=tuple(range(nq_t)) + (nq_t + ni_t,), n_aux=1)
    k_blk = (nq_t + ni_t) * tn // HEAD_DIM
    mask = _dsa_index(proj, aux, batch=batch, seq=seq, qi_blk=AW // IW, ki_blk=(nq_t + ni_t + 2) * tn // IDX_DIM,
                      n_keep=min(TOPK_KEYS, seq // 4))
    o = _dsa_attn(proj, mask, strip, strip_amax, batch=batch, seq=seq, k_blk=k_blk, v_blk=k_blk + tn // HEAD_DIM)
    return _matmul_res(o, w_out.astype(BF16), xf, mod_all, tab, mod_row=5, seq=seq)


def _nsa_compress_kernel(xk_ref, xkn_ref, xv_ref, xvn_ref, pos_ref, w1_ref, w2_ref, gain_ref, ok_ref, ov_ref, *, groups, kw):
    half = CMP_BLOCK // 2
    rows = xk_ref.shape[0]
    row_id = lax.broadcasted_iota(I32, (rows, HEAD_DIM), 0)

    def branch(br, x_ref, xn_ref, o_ref):
        for g in range(groups):
            p1 = jnp.zeros((rows, HEAD_DIM), F32)
            p2 = jnp.zeros((rows, HEAD_DIM), F32)
            p2n = jnp.zeros((xn_ref.shape[0], HEAD_DIM), F32)
            for r in range(half):
                c0 = r * kw + g * HEAD_DIM
                xa = x_ref[:, c0:c0 + HEAD_DIM]
                w_lo = w1_ref[br, r * HEAD_DIM:(r + 1) * HEAD_DIM, :]
                w_hi = w1_ref[br, (half + r) * HEAD_DIM:(half + r + 1) * HEAD_DIM, :]
                p1 = p1 + jnp.dot((xa + pos_ref[br, r:r + 1, :]).astype(BF16), w_lo, preferred_element_type=F32)
                p2 = p2 + jnp.dot((xa + pos_ref[br, half + r:half + r + 1, :]).astype(BF16), w_hi,
                                  preferred_element_type=F32)
                xb = xn_ref[:, c0:c0 + HEAD_DIM]
                p2n = p2n + jnp.dot((xb + pos_ref[br, half + r:half + r + 1, :]).astype(BF16), w_hi,
                                    preferred_element_type=F32)
            p2s = jnp.where(row_id == rows - 1, p2n[0:1, :], pltpu.roll(p2, rows - 1, 0))
            hid = _gelu_tanh(p1 + p2s)
            out = jnp.dot(hid.astype(BF16), w2_ref[br], preferred_element_type=F32)
            if br == 0:
                ms = jnp.mean(out * out, axis=-1, keepdims=True)
                out = out * lax.rsqrt(ms + EPS) * gain_ref[...]
            o_ref[:, g * HEAD_DIM:(g + 1) * HEAD_DIM] = out.astype(o_ref.dtype)

    branch(0, xk_ref, xkn_ref, ok_ref)
    branch(1, xv_ref, xvn_ref, ov_ref)


def _nsa_compress(kc, vc, cmp_pos, cmp_w1, cmp_w2, k_gain0, *, batch, seq):
    assert CMP_BLOCK == 2 * CMP_STRIDE
    KW = N_KV_HEADS * HEAD_DIM
    ncp = seq // CMP_STRIDE
    rb = min(TQ, ncp)
    nrb = ncp // rb
    nxt = BF16_SUBLANES
    wide = CMP_STRIDE * KW
    xk = kc.reshape(batch * ncp, wide)
    xv = vc.reshape(batch * ncp, wide)
    last_nxt = batch * ncp // nxt - 1
    main = pl.BlockSpec((rb, wide), lambda b, r: (b * nrb + r, 0))
    ahead = pl.BlockSpec((nxt, wide), lambda b, r: (jnp.minimum((b * nrb + r + 1) * (rb // nxt), last_nxt), 0))
    kern = functools.partial(_nsa_compress_kernel, groups=N_KV_HEADS, kw=KW)
    return pl.pallas_call(
        kern,
        grid=(batch, nrb),
        in_specs=[main, ahead, main, ahead,
                  pl.BlockSpec((2, CMP_BLOCK, HEAD_DIM), lambda b, r: (0, 0, 0)),
                  pl.BlockSpec((2, CMP_BLOCK * HEAD_DIM, HEAD_DIM), lambda b, r: (0, 0, 0)),
                  pl.BlockSpec((2, HEAD_DIM, HEAD_DIM), lambda b, r: (0, 0, 0)),
                  pl.BlockSpec((1, HEAD_DIM), lambda b, r: (0, 0))],
        out_specs=[pl.BlockSpec((rb, KW), lambda b, r: (b * nrb + r, 0))] * 2,
        out_shape=[jax.ShapeDtypeStruct((batch * ncp, KW), BF16)] * 2,
        compiler_params=_cparams(("arbitrary", "arbitrary")),
        name="nsa_compress",
    )(xk, xk, xv, xv, cmp_pos, cmp_w1.astype(BF16), cmp_w2.astype(BF16), k_gain0.reshape(1, HEAD_DIM))


def _nsa_attn_kernel(q_ref, kc_ref, vc_ref, ks_ref, vs_ref, kw_ref, vw_ref, g_ref, gb_ref, strip_ref, amax_ref,
                     ustrip_ref, cover_ref, o_ref, qs_scr, lc_scr, sel_scr, gt_scr, m_scr, l_scr, acc_scr, ocmp_scr,
                     osel_scr, imp_scr, ksq_scr, *, r_heads, n_pick):
    g = pl.program_id(1)
    i = pl.program_id(2)
    R = r_heads

    @pl.when(i == 0)
    def _():
        ksq_scr[0] = _max_row_sumsq(ks_ref)
        ksq_scr[1] = _max_row_sumsq(kw_ref)

    ncp = kc_ref.shape[0]
    n_sel = cover_ref.shape[0]
    pad = NEAR_CMP - CPQ
    _stack_heads(q_ref, qs_scr, R)

    def compressed(rows):
        lc_scr[0:pad, :] = jnp.zeros((pad, R * TQ), F32)
        lc_scr[pad:pad + rows, :] = lax.dot_general(kc_ref[0:rows, :], qs_scr[...], NT_DIMS,
                                                    preferred_element_type=F32)
        off = pl.multiple_of(i * CPQ, SUBLANES)
        lc_scr[pl.ds(off, NEAR_CMP), :] += ustrip_ref[...]
        c_iota = lax.broadcasted_iota(I32, (rows, TQ), 0)
        t_pos = i * TQ + lax.broadcasted_iota(I32, (rows, TQ), 1)
        vis = jnp.where(c_iota * CMP_STRIDE + (CMP_BLOCK - 1) <= t_pos, 0.0, NEG)
        s = lc_scr[pad:pad + rows, :] + jnp.tile(vis, (1, R))
        m = jnp.max(s, axis=0, keepdims=True)
        p = jnp.exp2(s - m)
        inv = jnp.where(m > 0.5 * NEG, 1.0 / jnp.sum(p, axis=0, keepdims=True), 0.0)
        pc = p * inv
        ocmp_scr[...] = lax.dot_general(vc_ref[0:rows, :], pc.astype(BF16), TN_DIMS, preferred_element_type=F32)
        psum = pc[:, 0:TQ]
        for r in range(1, R):
            psum = psum + pc[:, r * TQ:(r + 1) * TQ]
        hi = psum.astype(BF16)
        lo = (psum - hi.astype(F32)).astype(BF16)
        imp_scr[...] = (jnp.dot(cover_ref[:, 0:rows], hi, preferred_element_type=F32)
                        + jnp.dot(cover_ref[:, 0:rows], lo, preferred_element_type=F32))

    n_chunks = (CPQ * (i + 1) - 1) // TQ + 1
    for n in range(1, ncp // TQ + 1):
        @pl.when(n_chunks == n)
        def _(n=n):
            compressed(n * TQ)

    imp = imp_scr[...]
    n_io = lax.broadcasted_iota(I32, (n_sel, TQ), 0)
    t_sel = i * TQ + lax.broadcasted_iota(I32, (n_sel, TQ), 1)
    cur = jnp.right_shift(t_sel, SEL_BLOCK.bit_length() - 1)
    forced = (n_io == 0) | (n_io == cur) | (n_io == cur - 1)
    val0 = jnp.where(forced, BIG, jnp.where(n_io * SEL_BLOCK <= t_sel, imp, NEG))
    n_f = n_io.astype(F32)

    def pick_round(_, carry):
        val, selm = carry
        cm = jnp.max(val, axis=0, keepdims=True)
        idx = jnp.min(jnp.where(val == cm, n_f, float(n_sel)), axis=0, keepdims=True)
        pick = n_f == idx
        return jnp.where(pick, LOWEST, val), jnp.where(pick, 0.0, selm)

    _, selm = lax.fori_loop(0, n_pick, pick_round, (val0, jnp.full((n_sel, TQ), NEG, F32)))
    sel_scr[...] = selm

    n_far = _num_far_tiles(i)
    kt_last = i // KPT

    def sel_sweep(online):
        def block_mask(k0, rows):
            b0 = k0 // SEL_BLOCK
            return jnp.concatenate([jnp.broadcast_to(sel_scr[pl.ds(b0 + b, 1), :], (SEL_BLOCK, TQ))
                                    for b in range(rows // SEL_BLOCK)], axis=0)

        def far_tile(k0, rows):
            _flash_tile(ks_ref[pl.ds(k0, rows), :], vs_ref[pl.ds(k0, rows), :], qs_scr, block_mask(k0, rows),
                        None, None, m_scr, l_scr, acc_scr, R, online)

        def near_tile(kt, rows):
            k0 = pl.multiple_of(kt * TK, TK)
            bm = jnp.where(_key_minus_query(rows) <= (i - kt * KPT) * TQ, block_mask(k0, rows), NEG)
            _flash_tile(ks_ref[pl.ds(k0, rows), :], vs_ref[pl.ds(k0, rows), :], qs_scr, bm,
                        strip_ref, _strip_row(i, kt), m_scr, l_scr, acc_scr, R, online)

        _far_tiles(n_far, far_tile, online)
        _near_tiles(n_far, kt_last, near_tile, online)

    sel_bounded = _logits_bounded(qs_scr, ksq_scr[0], amax_ref[...])
    osel_scr[...] = _flash_sweep(sel_bounded, sel_sweep, m_scr, l_scr, acc_scr)

    def win_sweep(online):
        def win_tile(kt, rows):
            k0 = pl.multiple_of(kt * TK, TK)
            dist = (i - kt * KPT) * TQ - _key_minus_query(rows)
            vis_w = jnp.where((dist >= 0) & (dist < WINDOW), 0.0, NEG)
            _flash_tile(kw_ref[pl.ds(k0, rows), :], vw_ref[pl.ds(k0, rows), :], qs_scr, vis_w,
                        strip_ref, _strip_row(i, kt), m_scr, l_scr, acc_scr, R, online)

        if online:
            def win_body(kt, carry):
                win_tile(kt, TK)
                return carry

            lax.fori_loop(jnp.maximum(kt_last - 1, 0), kt_last + 1, win_body, 0)
        else:
            @pl.when(kt_last >= 1)
            def _():
                win_tile(kt_last - 1, 2 * TK)

            @pl.when(kt_last == 0)
            def _():
                win_tile(kt_last, TK)

    win_bounded = _logits_bounded(qs_scr, ksq_scr[1], amax_ref[...])
    o_win = _flash_sweep(win_bounded, win_sweep, m_scr, l_scr, acc_scr)

    gt_scr[...] = jax.nn.sigmoid(g_ref[:, 0:LANES] + gb_ref[...]).T

    def gate_row(br):
        rows = gt_scr[pl.ds(pl.multiple_of(br * N_HEADS + g * R, SUBLANES), R), :]
        return jnp.concatenate([rows[r:r + 1, :] for r in range(R)], axis=1)

    o_t = gate_row(0) * ocmp_scr[...] + gate_row(1) * osel_scr[...] + gate_row(2) * o_win
    _write_heads(o_ref, o_t, R)


def _nsa_attn(proj, aux_g, gate_b, k_cmp, v_cmp, strip, strip_amax, ustrip, *, batch, seq, blks):
    nq = seq // TQ
    G = N_KV_HEADS
    R = N_HEADS // G
    assert R == SUBLANES and 3 * N_HEADS <= LANES and WINDOW <= TK and WINDOW <= (NEAR_TILES - 1) * TQ
    assert (seq // CMP_STRIDE) % TQ == 0
    W = R * TQ
    ncp = seq // CMP_STRIDE
    n_cmp = (seq - CMP_BLOCK) // CMP_STRIDE + 1
    n_sel = seq // SEL_BLOCK
    cs = np.arange(ncp)[None, :] * CMP_STRIDE
    ss = np.arange(n_sel)[:, None] * SEL_BLOCK
    cover = ((cs < ss + SEL_BLOCK) & (cs + CMP_BLOCK - 1 >= ss) & (np.arange(ncp)[None, :] < n_cmp))
    cover = jnp.asarray(cover.astype(np.float32), dtype=BF16)
    gb = jnp.zeros((1, LANES), F32).at[0, :3 * N_HEADS].set(gate_b)
    kv = lambda blk: pl.BlockSpec((seq, HEAD_DIM), lambda b, g, i: (b, blk + g))
    cmp_spec = pl.BlockSpec((ncp, HEAD_DIM), lambda b, g, i: (b, g))
    kern = functools.partial(_nsa_attn_kernel, r_heads=R, n_pick=min(SEL_TOPN, n_sel))
    return pl.pallas_call(
        kern,
        grid=(batch, G, nq),
        in_specs=[pl.BlockSpec((TQ, W), lambda b, g, i: (b * nq + i, g)),
                  cmp_spec, cmp_spec, kv(blks["ks"]), kv(blks["vs"]), kv(blks["kw"]), kv(blks["vw"]),
                  pl.BlockSpec((TQ, aux_g.shape[1]), lambda b, g, i: (b * nq + i, 0)),
                  pl.BlockSpec((1, LANES), lambda b, g, i: (0, 0)),
                  pl.BlockSpec((STRIP_ROWS, W), lambda b, g, i: (0, g)),
                  pl.BlockSpec((1, W), lambda b, g, i: (0, g)),
                  pl.BlockSpec((NEAR_CMP, W), lambda b, g, i: (0, g)),
                  pl.BlockSpec((n_sel, ncp), lambda b, g, i: (0, 0))],
        out_specs=pl.BlockSpec((TQ, W), lambda b, g, i: (b * nq + i, g)),
        out_shape=jax.ShapeDtypeStruct((batch * seq, N_HEADS * HEAD_DIM), BF16),
        scratch_shapes=[pltpu.VMEM((W, HEAD_DIM), BF16),
                        pltpu.VMEM((NEAR_CMP - CPQ + ncp, W), F32),
                        pltpu.VMEM((n_sel, TQ), F32),
                        pltpu.VMEM((LANES, TQ), F32),
                        pltpu.VMEM((1, W), F32), pltpu.VMEM((1, W), F32), pltpu.VMEM((HEAD_DIM, W), F32),
                        pltpu.VMEM((HEAD_DIM, W), F32), pltpu.VMEM((HEAD_DIM, W), F32),
                        pltpu.VMEM((n_sel, TQ), F32),
                        pltpu.SMEM((2,), F32)],
        compiler_params=_cparams(("arbitrary", "arbitrary", "arbitrary")),
        name="nsa_attn",
    )(proj, k_cmp, v_cmp, proj, proj, proj, proj, aux_g, gb, strip, strip_amax, ustrip, cover)


def _nsa_mixer(xf, mod_all, tab, gain, w_in, gate_b, w_out, q_gain, k_gain, cmp_pos, cmp_w1, cmp_w2, strip, strip_amax,
               ustrip, *, batch, seq):
    AW, KW = N_HEADS * HEAD_DIM, N_KV_HEADS * HEAD_DIM
    tn = KW
    wp = _proj_layout(w_in, [(0, AW), (AW + 2 * KW, KW), (AW + 3 * KW, KW), (AW + 4 * KW, KW), (AW + 5 * KW, KW),
                             (AW, KW), (AW + KW, KW), (AW + 6 * KW, 3 * N_HEADS)], tn)
    nq_t = AW // tn
    gain_cols = jnp.concatenate([jnp.tile(q_gain * (HEAD_DIM ** -0.5 * LOG2E), N_HEADS), jnp.tile(k_gain[1], N_KV_HEADS),
                                 jnp.zeros((tn,), F32), jnp.tile(k_gain[2], N_KV_HEADS),
                                 jnp.zeros((4 * tn,), F32)]).reshape(1, -1)
    proj, kc, vc, aux_g = _norm_matmul(xf, mod_all, tab, gain, wp, mod_row=3, seq=seq, tn=tn, gain_cols=gain_cols,
                                       norm_tiles=tuple(range(nq_t)) + (nq_t, nq_t + 2), n_aux=3)
    k_cmp, v_cmp = _nsa_compress(kc, vc, cmp_pos, cmp_w1, cmp_w2, k_gain[0], batch=batch, seq=seq)
    per = tn // HEAD_DIM
    blks = {"ks": nq_t * per, "vs": (nq_t + 1) * per, "kw": (nq_t + 2) * per, "vw": (nq_t + 3) * per}
    o = _nsa_attn(proj, aux_g, gate_b, k_cmp, v_cmp, strip, strip_amax, ustrip, batch=batch, seq=seq, blks=blks)
    return _matmul_res(o, w_out.astype(BF16), xf, mod_all, tab, mod_row=5, seq=seq)


def _rglru_kernel(gbr_ref, xr_ref, cw_ref, cb_ref, gw_ref, gbias_ref, lam_ref, o_ref, h_scr, tail_scr, *, ts, bd):
    @pl.when(pl.program_id(2) == 0)
    def _():
        h_scr[...] = jnp.zeros(h_scr.shape, F32)
        tail_scr[...] = jnp.zeros(tail_scr.shape, F32)

    x = xr_ref[...]
    cbw = x.shape[1]
    xfull = jnp.concatenate([tail_scr[...], x], axis=0)
    xc = cw_ref[CONV_WIDTH - 1:CONV_WIDTH, :] * x + cb_ref[...]
    for w in range(1, CONV_WIDTH):
        xc = xc + cw_ref[CONV_WIDTH - 1 - w:CONV_WIDTH - w, :] * pltpu.roll(xfull, w, 0)[SUBLANES:, :]
    tail_scr[...] = x[ts - SUBLANES:ts, :]

    gr = []
    for gi in range(2):
        parts = [jnp.dot(xc[:, n * bd:(n + 1) * bd].astype(BF16), gw_ref[gi, n], preferred_element_type=F32)
                 for n in range(cbw // bd)]
        gr.append(jnp.concatenate(parts, axis=1) + gbias_ref[gi:gi + 1, :])
    r = jax.nn.sigmoid(gr[0])
    i_g = jax.nn.sigmoid(gr[1])
    nl = -lam_ref[...]
    softplus = jnp.maximum(nl, 0.0) + jnp.log1p(jnp.exp(-jnp.abs(nl)))
    log_a = -RG_C * r * softplus
    a = jnp.exp(log_a)
    u = jnp.sqrt(1.0 - jnp.exp(2.0 * log_a)) * (i_g * xc)

    row = lax.broadcasted_iota(I32, (ts, cbw), 0)
    sft = 1
    while sft < ts:
        keep = row >= sft
        a_sh = jnp.where(keep, pltpu.roll(a, sft, 0), 1.0)
        u_sh = jnp.where(keep, pltpu.roll(u, sft, 0), 0.0)
        u = u + a * u_sh
        a = a * a_sh
        sft *= 2
    hs = u + a * h_scr[...]
    h_scr[...] = hs[ts - 1:ts, :]
    o_ref[...] = (_gelu_tanh(gbr_ref[...]) * hs).astype(o_ref.dtype)


def _rglru_mixer(xf, mod_all, tab, gain, w_in, conv_w, conv_b, gate_w, gate_b, lam, w_out, *, batch, seq):
    d_rnn = conv_w.shape[1]
    bd = d_rnn // RG_BLOCKS
    tn = min(512, d_rnn)
    proj = _norm_matmul(xf, mod_all, tab, gain, w_in.astype(BF16), mod_row=3, seq=seq, tn=tn, out_dtype=F32)
    cbw = min(512, d_rnn)
    ncb = d_rnn // cbw
    ts = min(256, seq)
    nts = seq // ts
    kern = functools.partial(_rglru_kernel, ts=ts, bd=bd)
    y = pl.pallas_call(
        kern,
        grid=(batch, ncb, nts),
        in_specs=[pl.BlockSpec((ts, cbw), lambda b, c, t: (b * nts + t, c)),
                  pl.BlockSpec((ts, cbw), lambda b, c, t: (b * nts + t, ncb + c)),
                  pl.BlockSpec((CONV_WIDTH, cbw), lambda b, c, t: (0, c)),
                  pl.BlockSpec((1, cbw), lambda b, c, t: (0, c)),
                  pl.BlockSpec((2, cbw // bd, bd, bd), lambda b, c, t: (0, c, 0, 0)),
                  pl.BlockSpec((2, cbw), lambda b, c, t: (0, c)),
                  pl.BlockSpec((1, cbw), lambda b, c, t: (0, c))],
        out_specs=pl.BlockSpec((ts, cbw), lambda b, c, t: (b * nts + t, c)),
        out_shape=jax.ShapeDtypeStruct((batch * seq, d_rnn), BF16),
        scratch_shapes=[pltpu.VMEM((1, cbw), F32), pltpu.VMEM((SUBLANES, cbw), F32)],
        compiler_params=_cparams(("arbitrary", "arbitrary", "arbitrary")),
        name="rglru",
    )(proj, proj, conv_w, conv_b.reshape(1, d_rnn), gate_w.astype(BF16), gate_b, lam.reshape(1, d_rnn))
    return _matmul_res(y, w_out.astype(BF16), xf, mod_all, tab, mod_row=5, seq=seq)


def kernel(x, c, rel_bias, ada_w, ada_b, ada_table, norm_g, ffn_w_in, ffn_w_out, dsa_w_in, dsa_w_out, dsa_q_gain, dsa_k_gain, nsa_w_in, nsa_gate_b, nsa_w_out, nsa_q_gain, nsa_k_gain, nsa_cmp_pos, nsa_cmp_w1, nsa_cmp_w2, rg_w_in, rg_conv_w, rg_conv_b, rg_gate_w, rg_gate_b, rg_lambda, rg_w_out):
    B, S, D = x.shape
    depth = ada_table.shape[0]
    assert S % TK == 0 and S >= NEAR_TILES * TQ
    mod_all = _mod_all(c, ada_w, ada_b)
    strip, strip_amax, ustrip = _bias_strips(rel_bias)
    w1_all, w2_all, tf = _prep_ffn_weights(ffn_w_in, ffn_w_out)
    xf = x.reshape(B * S, D)
    for layer in range(depth):
        tab = ada_table[layer]
        xf = _ffn(xf, mod_all, tab, norm_g[layer, 0], w1_all, w2_all, tf, 2 * layer, mod_row=0, seq=S)
        kind, j = layer % 3, layer // 3
        if kind == 0:
            xf = _dsa_mixer(xf, mod_all, tab, norm_g[layer, 1], dsa_w_in[j], dsa_w_out[j], dsa_q_gain[j],
                            dsa_k_gain[j], strip, strip_amax, batch=B, seq=S)
        elif kind == 1:
            xf = _nsa_mixer(xf, mod_all, tab, norm_g[layer, 1], nsa_w_in[j], nsa_gate_b[j], nsa_w_out[j],
                            nsa_q_gain[j], nsa_k_gain[j], nsa_cmp_pos[j], nsa_cmp_w1[j], nsa_cmp_w2[j],
                            strip, strip_amax, ustrip, batch=B, seq=S)
        else:
            xf = _rglru_mixer(xf, mod_all, tab, norm_g[layer, 1], rg_w_in[j], rg_conv_w[j], rg_conv_b[j],
                              rg_gate_w[j], rg_gate_b[j], rg_lambda[j], rg_w_out[j], batch=B, seq=S)
        xf = _ffn(xf, mod_all, tab, norm_g[layer, 2], w1_all, w2_all, tf, 2 * layer + 1, mod_row=6, seq=S)
    return xf.reshape(B, S, D)
```

```python
import functools
import math

import numpy as np
import jax
import jax.numpy as jnp
from jax import lax
from jax.experimental import pallas as pl
from jax.experimental.pallas import tpu as pltpu

N_HEADS = 32
HEAD_DIM = 128
N_KV_HEADS = 4
FFN_RES = 0.5
N_MOD = 9
NUM_BUCKETS = 32
MAX_DISTANCE = 1024
EPS = 1e-6
NEG = -1e30
BIG = 1e30
IDX_HEADS = 16
IDX_DIM = 128
TOPK_KEYS = 256
CMP_BLOCK = 32
CMP_STRIDE = 16
SEL_BLOCK = 64
SEL_TOPN = 16
WINDOW = 512
RG_BLOCKS = 16
CONV_WIDTH = 4
RG_C = 8.0

LANES = 128
SUBLANES = 8
BF16_SUBLANES = 16
VMEM_LIMIT_BYTES = 56 * 1024 * 1024
BOUND_SLACK = 1.02

F32 = jnp.float32
BF16 = jnp.bfloat16
I32 = jnp.int32
INT_MIN = -2 ** 31
LOWEST = -3.0e38

TQ = 128
TK = 512
KPT = TK // TQ
SAFE_LOGIT = 60.0
LOG2E = 1.4426950408889634
FAR_DIST = int(math.ceil((NUM_BUCKETS // 2) * (MAX_DISTANCE / (NUM_BUCKETS // 2)) ** ((NUM_BUCKETS // 2 - 1) / (NUM_BUCKETS // 2)))) + 8
NEAR_TILES = -(-(FAR_DIST + TQ - 1) // TQ)
CPQ = TQ // CMP_STRIDE
CMP_LAST = CMP_STRIDE * (CPQ - 1) + CMP_BLOCK - 1
NEAR_CMP = -(-(-(-(FAR_DIST + CMP_LAST) // CMP_STRIDE)) // SUBLANES) * SUBLANES

STRIP_PAD = TK - TQ
STRIP_ROWS = NEAR_TILES * TQ + 2 * STRIP_PAD
NT_DIMS = (((1,), (1,)), ((), ()))
TN_DIMS = (((0,), (0,)), ((), ()))


def _round_up(n, m):
    return (n + m - 1) // m * m


def _cparams(sem):
    return pltpu.CompilerParams(dimension_semantics=sem, vmem_limit_bytes=VMEM_LIMIT_BYTES)


def _gelu_tanh(x):
    return 0.5 * x * (1.0 + jnp.tanh(0.7978845608028654 * (x + 0.044715 * x * x * x)))


def _mod_kernel(c_ref, w_ref, b_ref, o_ref):
    c = c_ref[...]
    s = c * jax.nn.sigmoid(c)
    o_ref[...] = jnp.dot(s.astype(BF16), w_ref[...].astype(BF16), preferred_element_type=F32) + b_ref[...]


def _mod_all(c, ada_w, ada_b):
    B, D = c.shape
    N = ada_w.shape[1]
    rows = BF16_SUBLANES
    tn = next(t for t in (512, 256, 128) if N % t == 0)
    cp = jnp.zeros((rows, D), F32).at[:B].set(c)
    out = pl.pallas_call(
        _mod_kernel,
        grid=(N // tn,),
        in_specs=[pl.BlockSpec((rows, D), lambda j: (0, 0)),
                  pl.BlockSpec((D, tn), lambda j: (0, j)),
                  pl.BlockSpec((1, tn), lambda j: (0, j))],
        out_specs=pl.BlockSpec((rows, tn), lambda j: (0, j)),
        out_shape=jax.ShapeDtypeStruct((rows, N), F32),
        compiler_params=_cparams(("arbitrary",)),
        name="ada_mod",
    )(cp, ada_w, ada_b.reshape(1, N))
    return out[:B].reshape(B, N_MOD, D)


def _norm_matmul_kernel(x_ref, mod_ref, tab_ref, g_ref, w_ref, gc_ref, *refs, mod_row, epilogue,
                        norm_tiles, n_aux, n_tiles):
    swiglu = epilogue == "swiglu"
    if swiglu:
        wu_ref, refs = refs[0], refs[1:]
    o_ref = refs[0]
    aux_ref = refs[1] if n_aux else None
    y_scr = refs[-1]
    j = pl.program_id(1)
    d_model = x_ref.shape[1]
    kc = min(512, d_model)

    def finish(h, u):
        if swiglu:
            o_ref[...] = (h * jax.nn.sigmoid(h) * u).astype(o_ref.dtype)
            return
        if norm_tiles:
            is_norm = functools.reduce(lambda a, b: a | b, [j == t for t in norm_tiles])
            for c in range(h.shape[1] // LANES):
                hc = h[:, c * LANES:(c + 1) * LANES]
                ms = jnp.mean(hc * hc, axis=-1, keepdims=True)
                factor = jnp.where(is_norm, lax.rsqrt(ms + EPS) * gc_ref[:, c * LANES:(c + 1) * LANES], 1.0)
                o_ref[:, c * LANES:(c + 1) * LANES] = (hc * factor).astype(o_ref.dtype)
        else:
            o_ref[...] = h.astype(o_ref.dtype)
        if n_aux:
            aux_ref[...] = h

    @pl.when(j == 0)
    def _():
        x = x_ref[...]
        inv = lax.rsqrt(jnp.sum(x * x, axis=-1, keepdims=True) * (1.0 / d_model) + EPS)
        mod = mod_ref[0]
        shift = mod[mod_row:mod_row + 1, :] + tab_ref[mod_row:mod_row + 1, :]
        gs = g_ref[...] * (1.0 + mod[mod_row + 1:mod_row + 2, :] + tab_ref[mod_row + 1:mod_row + 2, :])
        h = u = None
        for c0 in range(0, d_model, kc):
            yc = ((x[:, c0:c0 + kc] * inv) * gs[:, c0:c0 + kc] + shift[:, c0:c0 + kc]).astype(BF16)
            y_scr[:, c0:c0 + kc] = yc
            part = jnp.dot(yc, w_ref[c0:c0 + kc, :], preferred_element_type=F32)
            h = part if h is None else h + part
            if swiglu:
                part_u = jnp.dot(yc, wu_ref[c0:c0 + kc, :], preferred_element_type=F32)
                u = part_u if u is None else u + part_u
        finish(h, u)

    @pl.when(j != 0)
    def _():
        h = jnp.dot(y_scr[...], w_ref[...], preferred_element_type=F32)
        u = jnp.dot(y_scr[...], wu_ref[...], preferred_element_type=F32) if swiglu else None
        finish(h, u)


def _weight_spec(w, rows, cols, w_index, col_block):
    if w.ndim == 2:
        return pl.BlockSpec((rows, cols), lambda i, j: (0, col_block(j)))
    return pl.BlockSpec((None, rows, cols), lambda i, j: (w_index, 0, col_block(j)))


def _norm_matmul(x, mod_all, tab, gain, w, *, mod_row, seq, epilogue="plain", tn, tm=512,
                 out_dtype=BF16, gain_cols=None, norm_tiles=(), n_aux=0, w_index=0):
    T, D = x.shape
    swiglu = epilogue == "swiglu"
    n_tiles = w.shape[-1] // (2 * tn if swiglu else tn)
    n_out = n_tiles * tn
    tm = min(tm, seq)
    bpb = seq // tm
    if gain_cols is None:
        gain_cols = jnp.zeros((1, n_out), F32)
    out_shape = [jax.ShapeDtypeStruct((T, n_out), out_dtype)]
    out_specs = [pl.BlockSpec((tm, tn), lambda i, j: (i, j))]
    if n_aux:
        out_shape.append(jax.ShapeDtypeStruct((n_aux, T, tn), F32))
        out_specs.append(pl.BlockSpec((None, tm, tn), lambda i, j: (jnp.maximum(j - (n_tiles - n_aux), 0), i, 0)))
    kern = functools.partial(_norm_matmul_kernel, mod_row=mod_row, epilogue=epilogue,
                             norm_tiles=tuple(norm_tiles), n_aux=n_aux, n_tiles=n_tiles)
    in_specs = [pl.BlockSpec((tm, D), lambda i, j: (i, 0)),
                pl.BlockSpec((1, N_MOD, D), lambda i, j: (i // bpb, 0, 0)),
                pl.BlockSpec((N_MOD, D), lambda i, j: (0, 0)),
                pl.BlockSpec((1, D), lambda i, j: (0, 0)),
                _weight_spec(w, D, tn, w_index, lambda j: j),
                pl.BlockSpec((1, tn), lambda i, j: (0, j))]
    args = [x, mod_all, tab, gain.reshape(1, D), w, gain_cols]
    if swiglu:
        in_specs.append(_weight_spec(w, D, tn, w_index, lambda j: n_tiles + j))
        args.append(w)
    res = pl.pallas_call(
        kern,
        grid=(T // tm, n_tiles),
        in_specs=in_specs,
        out_specs=out_specs,
        out_shape=out_shape,
        scratch_shapes=[pltpu.VMEM((tm, D), BF16)],
        compiler_params=_cparams(("arbitrary", "arbitrary")),
        name="norm_matmul_" + epilogue,
    )(*args)
    return [res[0]] + [res[1][a] for a in range(n_aux)] if n_aux else res[0]


def _matmul_res_kernel(a_ref, w_ref, x_ref, mod_ref, tab_ref, o_ref, *, mod_row, res_scale):
    h = jnp.dot(a_ref[...], w_ref[...], preferred_element_type=F32)
    gate = mod_ref[0, mod_row:mod_row + 1, :] + tab_ref[mod_row:mod_row + 1, :]
    if res_scale != 1.0:
        gate = res_scale * gate
    o_ref[...] = x_ref[...] + gate * h


def _matmul_res(a, w, x, mod_all, tab, *, mod_row, seq, res_scale=1.0, tm=512, tn=1024, w_index=0):
    T, K = a.shape
    D = w.shape[-1]
    tm = min(tm, seq)
    tn = min(tn, D)
    bpb = seq // tm
    kern = functools.partial(_matmul_res_kernel, mod_row=mod_row, res_scale=res_scale)
    return pl.pallas_call(
        kern,
        grid=(T // tm, D // tn),
        in_specs=[pl.BlockSpec((tm, K), lambda i, j: (i, 0)),
                  _weight_spec(w, K, tn, w_index, lambda j: j),
                  pl.BlockSpec((tm, tn), lambda i, j: (i, j)),
                  pl.BlockSpec((1, N_MOD, tn), lambda i, j: (i // bpb, 0, j)),
                  pl.BlockSpec((N_MOD, tn), lambda i, j: (0, j))],
        out_specs=pl.BlockSpec((tm, tn), lambda i, j: (i, j)),
        out_shape=jax.ShapeDtypeStruct((T, D), F32),
        compiler_params=_cparams(("arbitrary", "arbitrary")),
        name="matmul_residual",
    )(a, w, x, mod_all, tab)


def _ffn_tile(d_ff):
    return 512 if d_ff >= 512 else _round_up(d_ff, LANES)


def _pad_cols_kernel(x_ref, o_ref):
    f = x_ref.shape[1]
    o_ref[:, :f] = x_ref[...].astype(o_ref.dtype)
    if o_ref.shape[1] > f:
        o_ref[:, f:] = jnp.zeros((o_ref.shape[0], o_ref.shape[1] - f), o_ref.dtype)


def _pad_rows_kernel(x_ref, o_ref, *, rows_valid):
    tr = o_ref.shape[0]
    row = pl.program_id(1) * tr + lax.broadcasted_iota(I32, o_ref.shape, 0)
    o_ref[...] = jnp.where(row < rows_valid, x_ref[...], 0.0).astype(o_ref.dtype)


def _prep_ffn_weights(ffn_w_in, ffn_w_out):
    L, two, D, two_f = ffn_w_in.shape
    F = two_f // 2
    tf = _ffn_tile(F)
    Fp = _round_up(F, tf)
    rows = L * two * D
    tr = 256
    w1 = pl.pallas_call(
        _pad_cols_kernel,
        grid=(rows // tr, 2),
        in_specs=[pl.BlockSpec((tr, F), lambda r, h: (r, h))],
        out_specs=pl.BlockSpec((tr, Fp), lambda r, h: (r, h)),
        out_shape=jax.ShapeDtypeStruct((rows, 2 * Fp), BF16),
        compiler_params=_cparams(("arbitrary", "arbitrary")),
        name="ffn_w_in_prep",
    )(ffn_w_in.reshape(rows, two_f))
    w2 = pl.pallas_call(
        functools.partial(_pad_rows_kernel, rows_valid=F),
        grid=(L * two, Fp // tf),
        in_specs=[pl.BlockSpec((None, tf, D), lambda l, r: (l, r, 0))],
        out_specs=pl.BlockSpec((None, tf, D), lambda l, r: (l, r, 0)),
        out_shape=jax.ShapeDtypeStruct((L * two, Fp, D), BF16),
        compiler_params=_cparams(("arbitrary", "arbitrary")),
        name="ffn_w_out_prep",
    )(ffn_w_out.reshape(L * two, F, D))
    return w1.reshape(L * two, D, 2 * Fp), w2, tf


def _ffn(x, mod_all, tab, gain, w1, w2, tf, w_index, *, mod_row, seq):
    a = _norm_matmul(x, mod_all, tab, gain, w1, mod_row=mod_row, seq=seq, epilogue="swiglu", tn=tf, w_index=w_index)
    return _matmul_res(a, w2, x, mod_all, tab, mod_row=mod_row + 2, seq=seq, res_scale=FFN_RES, w_index=w_index)


def _t5_bucket(dist):
    n = jnp.maximum(dist, 0)
    max_exact = NUM_BUCKETS // 2
    nf = jnp.maximum(n, 1).astype(F32)
    large = max_exact + (jnp.log(nf / max_exact) / math.log(MAX_DISTANCE / max_exact)
                         * (NUM_BUCKETS - max_exact)).astype(I32)
    large = jnp.minimum(large, NUM_BUCKETS - 1)
    return jnp.where(n < max_exact, n, large)


def _bias_strip_kernel(rb_ref, bkt_ref, o_ref, amax_ref):
    h = pl.program_id(0)
    bk = bkt_ref[...]
    far = rb_ref[NUM_BUCKETS - 1, h]
    acc = jnp.zeros(bk.shape, F32)
    for k in range(NUM_BUCKETS - 1):
        acc = jnp.where(bk == k, LOG2E * (rb_ref[k, h] - far), acc)
    o_ref[...] = acc
    amax_ref[...] = jnp.max(jnp.abs(acc), axis=0, keepdims=True)


def _bias_strip(rel_bias, bkt):
    rows = bkt.shape[0]
    H = rel_bias.shape[1]
    return pl.pallas_call(
        _bias_strip_kernel,
        grid=(H,),
        in_specs=[pl.BlockSpec(memory_space=pltpu.SMEM),
                  pl.BlockSpec((rows, TQ), lambda h: (0, 0))],
        out_specs=[pl.BlockSpec((rows, TQ), lambda h: (0, h)), pl.BlockSpec((1, TQ), lambda h: (0, h))],
        out_shape=[jax.ShapeDtypeStruct((rows, H * TQ), F32), jax.ShapeDtypeStruct((1, H * TQ), F32)],
        compiler_params=_cparams(("arbitrary",)),
        name="bias_strip",
    )(rel_bias, bkt)


def _bias_strips(rel_bias):
    iq = jnp.arange(TQ, dtype=I32)[None, :]
    x = jnp.arange(-STRIP_PAD, NEAR_TILES * TQ + STRIP_PAD, dtype=I32)[:, None]
    bkt_tok = _t5_bucket(iq - x + (NEAR_TILES - 1) * TQ)
    y = jnp.arange(NEAR_CMP, dtype=I32)[:, None]
    bkt_cmp = _t5_bucket(iq + CMP_STRIDE * (NEAR_CMP - 1 - y) - CMP_LAST)
    strip, strip_amax = _bias_strip(rel_bias, bkt_tok)
    ustrip, _ = _bias_strip(rel_bias, bkt_cmp)
    return strip, strip_amax, ustrip


def _stack_heads(q_ref, qs_scr, n):
    for r in range(n):
        qs_scr[r * TQ:(r + 1) * TQ, :] = q_ref[:, r * LANES:(r + 1) * LANES]


def _key_minus_query(rows):
    return lax.broadcasted_iota(I32, (rows, TQ), 0) - lax.broadcasted_iota(I32, (rows, TQ), 1)


def _num_far_tiles(i):
    return jnp.maximum(i - NEAR_TILES + 1, 0) // KPT


def _strip_row(i, kt):
    return pl.multiple_of((KPT + NEAR_TILES - 2 - (i - kt * KPT)) * TQ, TQ)


def _flash_init(m_scr, l_scr, acc_scr):
    m_scr[...] = jnp.full(m_scr.shape, NEG, F32)
    l_scr[...] = jnp.zeros(l_scr.shape, F32)
    acc_scr[...] = jnp.zeros(acc_scr.shape, F32)


def _max_row_sumsq(k_ref):
    rows = k_ref.shape[0]
    ch = min(rows, 1024)

    def body(c, best):
        kk = k_ref[pl.ds(pl.multiple_of(c * ch, ch), ch), :].astype(F32)
        return jnp.maximum(best, jnp.max(jnp.sum(kk * kk, axis=1, keepdims=True)))

    return lax.fori_loop(0, rows // ch, body, jnp.float32(0.0))


def _logits_bounded(qs_scr, k_sumsq, amax_row):
    qf = qs_scr[...].astype(F32)
    q_sumsq = lax.dot_general(jnp.ones((SUBLANES, HEAD_DIM), BF16), (qf * qf).astype(BF16), NT_DIMS,
                              preferred_element_type=F32)[0:1, :]
    bound = jnp.sqrt(q_sumsq * k_sumsq) * BOUND_SLACK + amax_row
    return jnp.max(bound) <= SAFE_LOGIT


def _flash_tile(k_tile, v_tile, qs_scr, madd, strip_ref, strip_row, m_scr, l_scr, acc_scr, n_heads, online):
    s = lax.dot_general(k_tile, qs_scr[...], NT_DIMS, preferred_element_type=F32) + jnp.tile(madd, (1, n_heads))
    if strip_ref is not None:
        s = s + strip_ref[pl.ds(strip_row, k_tile.shape[0]), :]
    if not online:
        p = jnp.exp2(s)
        l_scr[...] += jnp.sum(p, axis=0, keepdims=True)
        acc_scr[...] += lax.dot_general(v_tile, p.astype(BF16), TN_DIMS, preferred_element_type=F32)
        return
    m_prev = m_scr[...]
    m_new = jnp.maximum(m_prev, jnp.max(s, axis=0, keepdims=True))
    alpha = jnp.exp2(m_prev - m_new)
    p = jnp.exp2(s - m_new)
    l_scr[...] = alpha * l_scr[...] + jnp.sum(p, axis=0, keepdims=True)
    pv = lax.dot_general(v_tile, p.astype(BF16), TN_DIMS, preferred_element_type=F32)
    acc_scr[...] = alpha * acc_scr[...] + pv
    m_scr[...] = m_new


def _far_tiles(n_far, tile, online):
    if online:
        def body(kt, carry):
            tile(pl.multiple_of(kt * TK, TK), TK)
            return carry

        lax.fori_loop(0, n_far, body, 0)
        return

    def pair_body(u, carry):
        tile(pl.multiple_of(u * 2 * TK, 2 * TK), 2 * TK)
        return carry

    lax.fori_loop(0, n_far // 2, pair_body, 0)

    @pl.when(n_far % 2 == 1)
    def _():
        tile(pl.multiple_of((n_far - 1) * TK, TK), TK)


def _near_tiles(n_far, kt_last, tile, online):
    assert (NEAR_TILES + KPT - 2) // KPT + 1 <= 3
    if online:
        def body(kt, carry):
            tile(kt, TK)
            return carry

        lax.fori_loop(n_far, kt_last + 1, body, 0)
        return
    n_near = kt_last + 1 - n_far

    @pl.when(n_near == 3)
    def _():
        tile(kt_last - 2, TK)

    @pl.when(n_near >= 2)
    def _():
        tile(kt_last - 1, 2 * TK)

    @pl.when(n_near == 1)
    def _():
        tile(kt_last, TK)


def _flash_sweep(bounded, sweep, m_scr, l_scr, acc_scr):
    _flash_init(m_scr, l_scr, acc_scr)

    @pl.when(bounded)
    def _():
        sweep(False)

    @pl.when(jnp.logical_not(bounded))
    def _():
        sweep(True)
        l_scr[...] = jnp.where(m_scr[...] > 0.5 * NEG, l_scr[...], 0.0)

    l = l_scr[...]
    return acc_scr[...] * jnp.where(l > 0.0, 1.0 / l, 0.0)


def _write_heads(o_ref, o_t, n):
    for r in range(n):
        o_ref[:, r * LANES:(r + 1) * LANES] = o_t[:, r * TQ:(r + 1) * TQ].T.astype(o_ref.dtype)


def _dsa_index_kernel(qi_ref, ki_ref, wi_ref, mask_ref, keys_scr, qis_scr, j_scr, *, n_keep, ih, seq):
    i = pl.program_id(1)
    nkt = i // KPT + 1
    _stack_heads(qi_ref, qis_scr, ih)
    w_t = wi_ref[...].T * (ih ** -0.5 * IDX_DIM ** -0.5)
    w_rows = [w_t[h:h + 1, :] for h in range(ih)]
    q_pos = i * TQ + lax.broadcasted_iota(I32, (TQ, TQ), 1)
    k_iota = lax.broadcasted_iota(I32, (TQ, TQ), 0)

    def sub_rows(kt, u):
        return pl.multiple_of(kt * TK + u * TQ, TQ)

    def score_tile(kt, carry):
        k0 = pl.multiple_of(kt * TK, TK)
        d = lax.dot_general(ki_ref[pl.ds(k0, TK), :], qis_scr[...], NT_DIMS, preferred_element_type=F32)
        for u in range(KPT):
            sc = jnp.zeros((TQ, TQ), F32)
            for h in range(ih):
                sc = sc + jnp.maximum(d[u * TQ:(u + 1) * TQ, h * TQ:(h + 1) * TQ], 0.0) * w_rows[h]
            bits = pltpu.bitcast(sc, I32)
            key = jnp.where(bits < 0, bits ^ 0x7FFFFFFF, bits)
            r0 = sub_rows(kt, u)
            keys_scr[pl.ds(r0, TQ), :] = jnp.where(r0 + k_iota <= q_pos, key, INT_MIN)
        return carry

    lax.fori_loop(0, nkt, score_tile, 0)

    def count(pred):
        def body(kt, acc):
            for u in range(KPT):
                r0 = sub_rows(kt, u)
                acc = acc + jnp.where(pred(keys_scr[pl.ds(r0, TQ), :], r0 + k_iota), 1.0, 0.0)
            return acc
        acc = lax.fori_loop(0, nkt, body, jnp.zeros((TQ, TQ), F32))
        return jnp.sum(acc, axis=0, keepdims=True)

    def bit_body(b, lo):
        cand = lo + jnp.left_shift(jnp.int32(1), 31 - b)
        cnt = count(lambda kk, idx: kk >= cand)
        return jnp.where(cnt >= n_keep, cand, lo)

    thr = lax.fori_loop(0, 32, bit_body, jnp.full((1, TQ), INT_MIN, I32))
    cnt_ge = count(lambda kk, idx: kk >= thr)
    has_tie = (cnt_ge > n_keep) & (thr > INT_MIN)
    j_scr[...] = jnp.full((1, TQ), seq, I32)

    @pl.when(jnp.max(jnp.where(has_tie, 1.0, 0.0)) > 0.5)
    def _():
        need = n_keep - count(lambda kk, idx: kk > thr)

        def jb(b, lo):
            cand = lo + jnp.left_shift(jnp.int32(1), (seq.bit_length() - 2) - b)
            c = count(lambda kk, idx: (kk == thr) & (idx < cand))
            return jnp.where(c < need, cand, lo)

        j_last = lax.fori_loop(0, seq.bit_length() - 1, jb, jnp.zeros((1, TQ), I32))
        j_scr[...] = jnp.where(has_tie, j_last, seq)

    thr_c = jnp.maximum(thr, INT_MIN + 1)
    j_last = j_scr[...]

    def write_tile(kt, carry):
        for u in range(KPT):
            r0 = sub_rows(kt, u)
            kk = keys_scr[pl.ds(r0, TQ), :]
            sel = (kk > thr_c) | ((kk == thr_c) & (r0 + k_iota <= j_last))
            mask_ref[pl.ds(r0, TQ), :] = jnp.where(sel, 0.0, NEG).astype(BF16)
        return carry

    lax.fori_loop(0, nkt, write_tile, 0)

    def fill_tile(kt, carry):
        k0 = pl.multiple_of(kt * TK, TK)
        mask_ref[pl.ds(k0, TK), :] = jnp.full((TK, TQ), NEG, BF16)
        return carry

    lax.fori_loop(nkt, seq // TK, fill_tile, 0)


def _dsa_index(proj, aux, *, batch, seq, qi_blk, ki_blk, n_keep):
    nq = seq // TQ
    ihw = IDX_HEADS * IDX_DIM
    kern = functools.partial(_dsa_index_kernel, n_keep=n_keep, ih=IDX_HEADS, seq=seq)
    return pl.pallas_call(
        kern,
        grid=(batch, nq),
        in_specs=[pl.BlockSpec((TQ, ihw), lambda b, i: (b * nq + i, qi_blk)),
                  pl.BlockSpec((seq, IDX_DIM), lambda b, i: (b, ki_blk)),
                  pl.BlockSpec((TQ, LANES), lambda b, i: (b * nq + i, 1))],
        out_specs=pl.BlockSpec((None, seq, TQ), lambda b, i: (b, 0, i)),
        out_shape=jax.ShapeDtypeStruct((batch, seq, seq), BF16),
        scratch_shapes=[pltpu.VMEM((seq, TQ), I32),
                        pltpu.VMEM((IDX_HEADS * TQ, IDX_DIM), BF16),
                        pltpu.VMEM((1, TQ), I32)],
        compiler_params=_cparams(("arbitrary", "arbitrary")),
        name="dsa_index",
    )(proj, proj, aux)


def _dsa_attn_kernel(q_ref, k_ref, v_ref, mask_ref, strip_ref, amax_ref, o_ref, qs_scr, m_scr, l_scr, acc_scr,
                     ksq_scr, *, r_heads):
    i = pl.program_id(2)

    @pl.when(i == 0)
    def _():
        ksq_scr[0] = _max_row_sumsq(k_ref)

    _stack_heads(q_ref, qs_scr, r_heads)
    n_far = _num_far_tiles(i)

    def sweep(online):
        def far_tile(k0, rows):
            _flash_tile(k_ref[pl.ds(k0, rows), :], v_ref[pl.ds(k0, rows), :], qs_scr,
                        mask_ref[pl.ds(k0, rows), :].astype(F32), None, None, m_scr, l_scr, acc_scr, r_heads, online)

        def near_tile(kt, rows):
            k0 = pl.multiple_of(kt * TK, TK)
            _flash_tile(k_ref[pl.ds(k0, rows), :], v_ref[pl.ds(k0, rows), :], qs_scr,
                        mask_ref[pl.ds(k0, rows), :].astype(F32), strip_ref, _strip_row(i, kt),
                        m_scr, l_scr, acc_scr, r_heads, online)

        _far_tiles(n_far, far_tile, online)
        _near_tiles(n_far, i // KPT, near_tile, online)

    bounded = _logits_bounded(qs_scr, ksq_scr[0], amax_ref[...])
    _write_heads(o_ref, _flash_sweep(bounded, sweep, m_scr, l_scr, acc_scr), r_heads)


def _dsa_attn(proj, mask, strip, strip_amax, *, batch, seq, k_blk, v_blk):
    nq = seq // TQ
    G = N_KV_HEADS
    R = N_HEADS // G
    W = R * TQ
    kern = functools.partial(_dsa_attn_kernel, r_heads=R)
    return pl.pallas_call(
        kern,
        grid=(batch, G, nq),
        in_specs=[pl.BlockSpec((TQ, R * HEAD_DIM), lambda b, g, i: (b * nq + i, g)),
                  pl.BlockSpec((seq, HEAD_DIM), lambda b, g, i: (b, k_blk + g)),
                  pl.BlockSpec((seq, HEAD_DIM), lambda b, g, i: (b, v_blk + g)),
                  pl.BlockSpec((None, seq, TQ), lambda b, g, i: (b, 0, i)),
                  pl.BlockSpec((STRIP_ROWS, W), lambda b, g, i: (0, g)),
                  pl.BlockSpec((1, W), lambda b, g, i: (0, g))],
        out_specs=pl.BlockSpec((TQ, R * HEAD_DIM), lambda b, g, i: (b * nq + i, g)),
        out_shape=jax.ShapeDtypeStruct((batch * seq, N_HEADS * HEAD_DIM), BF16),
        scratch_shapes=[pltpu.VMEM((W, HEAD_DIM), BF16),
                        pltpu.VMEM((1, W), F32), pltpu.VMEM((1, W), F32), pltpu.VMEM((HEAD_DIM, W), F32),
                        pltpu.SMEM((1,), F32)],
        compiler_params=_cparams(("arbitrary", "arbitrary", "arbitrary")),
        name="dsa_attn",
    )(proj, proj, proj, mask, strip, strip_amax)


def _proj_layout(w_in, segments, tn):
    cols = []
    for start, width in segments:
        pad = _round_up(width, tn) - width
        cols.append(jnp.pad(w_in[:, start:start + width], ((0, 0), (0, pad))))
    return jnp.concatenate(cols, axis=1).astype(BF16)


def _dsa_mixer(xf, mod_all, tab, gain, w_in, w_out, q_gain, k_gain, strip, strip_amax, *, batch, seq):
    AW, KW = N_HEADS * HEAD_DIM, N_KV_HEADS * HEAD_DIM
    IW = IDX_HEADS * IDX_DIM
    tn = KW
    wp = _proj_layout(w_in, [(0, AW), (AW + 2 * KW, IW), (AW, KW), (AW + KW, KW), (AW + 2 * KW + IW, IDX_DIM + IDX_HEADS)], tn)
    nq_t, ni_t = AW // tn, IW // tn
    gain_cols = jnp.concatenate([jnp.tile(q_gain * (HEAD_DIM ** -0.5 * LOG2E), N_HEADS), jnp.zeros((IW,), F32),
                                 jnp.tile(k_gain, N_KV_HEADS), jnp.zeros((2 * tn,), F32)]).reshape(1, -1)
    proj, aux = _norm_matmul(xf, mod_all, tab, gain, wp, mod_row=3, seq=seq, tn=tn, gain_cols=gain_cols,
                             norm_tiles=tuple(range(nq_t)) + (nq_t + ni_t,), n_aux=1)
    k_blk = (nq_t + ni_t) * tn // HEAD_DIM
    mask = _dsa_index(proj, aux, batch=batch, seq=seq, qi_blk=AW // IW, ki_blk=(nq_t + ni_t + 2) * tn // IDX_DIM,
                      n_keep=min(TOPK_KEYS, seq // 4))
    o = _dsa_attn(proj, mask, strip, strip_amax, batch=batch, seq=seq, k_blk=k_blk, v_blk=k_blk + tn // HEAD_DIM)
    return _matmul_res(o, w_out.astype(BF16), xf, mod_all, tab, mod_row=5, seq=seq)


def _nsa_compress_kernel(xk_ref, xkn_ref, xv_ref, xvn_ref, pos_ref, w1_ref, w2_ref, gain_ref, ok_ref, ov_ref, *, groups, kw):
    half = CMP_BLOCK // 2
    rows = xk_ref.shape[0]
    row_id = lax.broadcasted_iota(I32, (rows, HEAD_DIM), 0)

    def branch(br, x_ref, xn_ref, o_ref):
        for g in range(groups):
            p1 = jnp.zeros((rows, HEAD_DIM), F32)
            p2 = jnp.zeros((rows, HEAD_DIM), F32)
            p2n = jnp.zeros((xn_ref.shape[0], HEAD_DIM), F32)
            for r in range(half):
                c0 = r * kw + g * HEAD_DIM
                xa = x_ref[:, c0:c0 + HEAD_DIM]
                w_lo = w1_ref[br, r * HEAD_DIM:(r + 1) * HEAD_DIM, :]
                w_hi = w1_ref[br, (half + r) * HEAD_DIM:(half + r + 1) * HEAD_DIM, :]
                p1 = p1 + jnp.dot((xa + pos_ref[br, r:r + 1, :]).astype(BF16), w_lo, preferred_element_type=F32)
                p2 = p2 + jnp.dot((xa + pos_ref[br, half + r:half + r + 1, :]).astype(BF16), w_hi,
                                  preferred_element_type=F32)
                xb = xn_ref[:, c0:c0 + HEAD_DIM]
                p2n = p2n + jnp.dot((xb + pos_ref[br, half + r:half + r + 1, :]).astype(BF16), w_hi,
                                    preferred_element_type=F32)
            p2s = jnp.where(row_id == rows - 1, p2n[0:1, :], pltpu.roll(p2, rows - 1, 0))
            hid = _gelu_tanh(p1 + p2s)
            out = jnp.dot(hid.astype(BF16), w2_ref[br], preferred_element_type=F32)
            if br == 0:
                ms = jnp.mean(out * out, axis=-1, keepdims=True)
                out = out * lax.rsqrt(ms + EPS) * gain_ref[...]
            o_ref[:, g * HEAD_DIM:(g + 1) * HEAD_DIM] = out.astype(o_ref.dtype)

    branch(0, xk_ref, xkn_ref, ok_ref)
    branch(1, xv_ref, xvn_ref, ov_ref)


def _nsa_compress(kc, vc, cmp_pos, cmp_w1, cmp_w2, k_gain0, *, batch, seq):
    assert CMP_BLOCK == 2 * CMP_STRIDE
    KW = N_KV_HEADS * HEAD_DIM
    ncp = seq // CMP_STRIDE
    rb = min(TQ, ncp)
    nrb = ncp // rb
    nxt = BF16_SUBLANES
    wide = CMP_STRIDE * KW
    xk = kc.reshape(batch * ncp, wide)
    xv = vc.reshape(batch * ncp, wide)
    last_nxt = batch * ncp // nxt - 1
    main = pl.BlockSpec((rb, wide), lambda b, r: (b * nrb + r, 0))
    ahead = pl.BlockSpec((nxt, wide), lambda b, r: (jnp.minimum((b * nrb + r + 1) * (rb // nxt), last_nxt), 0))
    kern = functools.partial(_nsa_compress_kernel, groups=N_KV_HEADS, kw=KW)
    return pl.pallas_call(
        kern,
        grid=(batch, nrb),
        in_specs=[main, ahead, main, ahead,
                  pl.BlockSpec((2, CMP_BLOCK, HEAD_DIM), lambda b, r: (0, 0, 0)),
                  pl.BlockSpec((2, CMP_BLOCK * HEAD_DIM, HEAD_DIM), lambda b, r: (0, 0, 0)),
                  pl.BlockSpec((2, HEAD_DIM, HEAD_DIM), lambda b, r: (0, 0, 0)),
                  pl.BlockSpec((1, HEAD_DIM), lambda b, r: (0, 0))],
        out_specs=[pl.BlockSpec((rb, KW), lambda b, r: (b * nrb + r, 0))] * 2,
        out_shape=[jax.ShapeDtypeStruct((batch * ncp, KW), BF16)] * 2,
        compiler_params=_cparams(("arbitrary", "arbitrary")),
        name="nsa_compress",
    )(xk, xk, xv, xv, cmp_pos, cmp_w1.astype(BF16), cmp_w2.astype(BF16), k_gain0.reshape(1, HEAD_DIM))


def _nsa_attn_kernel(q_ref, kc_ref, vc_ref, ks_ref, vs_ref, kw_ref, vw_ref, g_ref, gb_ref, strip_ref, amax_ref,
                     ustrip_ref, cover_ref, o_ref, qs_scr, lc_scr, sel_scr, gt_scr, m_scr, l_scr, acc_scr, ocmp_scr,
                     osel_scr, imp_scr, ksq_scr, *, r_heads, n_pick):
    g = pl.program_id(1)
    i = pl.program_id(2)
    R = r_heads

    @pl.when(i == 0)
    def _():
        ksq_scr[0] = _max_row_sumsq(ks_ref)
        ksq_scr[1] = _max_row_sumsq(kw_ref)

    ncp = kc_ref.shape[0]
    n_sel = cover_ref.shape[0]
    pad = NEAR_CMP - CPQ
    _stack_heads(q_ref, qs_scr, R)

    def compressed(rows):
        lc_scr[0:pad, :] = jnp.zeros((pad, R * TQ), F32)
        lc_scr[pad:pad + rows, :] = lax.dot_general(kc_ref[0:rows, :], qs_scr[...], NT_DIMS,
                                                    preferred_element_type=F32)
        off = pl.multiple_of(i * CPQ, SUBLANES)
        lc_scr[pl.ds(off, NEAR_CMP), :] += ustrip_ref[...]
        c_iota = lax.broadcasted_iota(I32, (rows, TQ), 0)
        t_pos = i * TQ + lax.broadcasted_iota(I32, (rows, TQ), 1)
        vis = jnp.where(c_iota * CMP_STRIDE + (CMP_BLOCK - 1) <= t_pos, 0.0, NEG)
        s = lc_scr[pad:pad + rows, :] + jnp.tile(vis, (1, R))
        m = jnp.max(s, axis=0, keepdims=True)
        p = jnp.exp2(s - m)
        inv = jnp.where(m > 0.5 * NEG, 1.0 / jnp.sum(p, axis=0, keepdims=True), 0.0)
        pc = p * inv
        ocmp_scr[...] = lax.dot_general(vc_ref[0:rows, :], pc.astype(BF16), TN_DIMS, preferred_element_type=F32)
        psum = pc[:, 0:TQ]
        for r in range(1, R):
            psum = psum + pc[:, r * TQ:(r + 1) * TQ]
        hi = psum.astype(BF16)
        lo = (psum - hi.astype(F32)).astype(BF16)
        imp_scr[...] = (jnp.dot(cover_ref[:, 0:rows], hi, preferred_element_type=F32)
                        + jnp.dot(cover_ref[:, 0:rows], lo, preferred_element_type=F32))

    n_chunks = (CPQ * (i + 1) - 1) // TQ + 1
    for n in range(1, ncp // TQ + 1):
        @pl.when(n_chunks == n)
        def _(n=n):
            compressed(n * TQ)

    imp = imp_scr[...]
    n_io = lax.broadcasted_iota(I32, (n_sel, TQ), 0)
    t_sel = i * TQ + lax.broadcasted_iota(I32, (n_sel, TQ), 1)
    cur = jnp.right_shift(t_sel, SEL_BLOCK.bit_length() - 1)
    forced = (n_io == 0) | (n_io == cur) | (n_io == cur - 1)
    val0 = jnp.where(forced, BIG, jnp.where(n_io * SEL_BLOCK <= t_sel, imp, NEG))
    n_f = n_io.astype(F32)

    def pick_round(_, carry):
        val, selm = carry
        cm = jnp.max(val, axis=0, keepdims=True)
        idx = jnp.min(jnp.where(val == cm, n_f, float(n_sel)), axis=0, keepdims=True)
        pick = n_f == idx
        return jnp.where(pick, LOWEST, val), jnp.where(pick, 0.0, selm)

    _, selm = lax.fori_loop(0, n_pick, pick_round, (val0, jnp.full((n_sel, TQ), NEG, F32)))
    sel_scr[...] = selm

    n_far = _num_far_tiles(i)
    kt_last = i // KPT

    def sel_sweep(online):
        def block_mask(k0, rows):
            b0 = k0 // SEL_BLOCK
            return jnp.concatenate([jnp.broadcast_to(sel_scr[pl.ds(b0 + b, 1), :], (SEL_BLOCK, TQ))
                                    for b in range(rows // SEL_BLOCK)], axis=0)

        def far_tile(k0, rows):
            _flash_tile(ks_ref[pl.ds(k0, rows), :], vs_ref[pl.ds(k0, rows), :], qs_scr, block_mask(k0, rows),
                        None, None, m_scr, l_scr, acc_scr, R, online)

        def near_tile(kt, rows):
            k0 = pl.multiple_of(kt * TK, TK)
            bm = jnp.where(_key_minus_query(rows) <= (i - kt * KPT) * TQ, block_mask(k0, rows), NEG)
            _flash_tile(ks_ref[pl.ds(k0, rows), :], vs_ref[pl.ds(k0, rows), :], qs_scr, bm,
                        strip_ref, _strip_row(i, kt), m_scr, l_scr, acc_scr, R, online)

        _far_tiles(n_far, far_tile, online)
        _near_tiles(n_far, kt_last, near_tile, online)

    sel_bounded = _logits_bounded(qs_scr, ksq_scr[0], amax_ref[...])
    osel_scr[...] = _flash_sweep(sel_bounded, sel_sweep, m_scr, l_scr, acc_scr)

    def win_sweep(online):
        def win_tile(kt, rows):
            k0 = pl.multiple_of(kt * TK, TK)
            dist = (i - kt * KPT) * TQ - _key_minus_query(rows)
            vis_w = jnp.where((dist >= 0) & (dist < WINDOW), 0.0, NEG)
            _flash_tile(kw_ref[pl.ds(k0, rows), :], vw_ref[pl.ds(k0, rows), :], qs_scr, vis_w,
                        strip_ref, _strip_row(i, kt), m_scr, l_scr, acc_scr, R, online)

        if online:
            def win_body(kt, carry):
                win_tile(kt, TK)
                return carry

            lax.fori_loop(jnp.maximum(kt_last - 1, 0), kt_last + 1, win_body, 0)
        else:
            @pl.when(kt_last >= 1)
            def _():
                win_tile(kt_last - 1, 2 * TK)

            @pl.when(kt_last == 0)
            def _():
                win_tile(kt_last, TK)

    win_bounded = _logits_bounded(qs_scr, ksq_scr[1], amax_ref[...])
    o_win = _flash_sweep(win_bounded, win_sweep, m_scr, l_scr, acc_scr)

    gt_scr[...] = jax.nn.sigmoid(g_ref[:, 0:LANES] + gb_ref[...]).T

    def gate_row(br):
        rows = gt_scr[pl.ds(pl.multiple_of(br * N_HEADS + g * R, SUBLANES), R), :]
        return jnp.concatenate([rows[r:r + 1, :] for r in range(R)], axis=1)

    o_t = gate_row(0) * ocmp_scr[...] + gate_row(1) * osel_scr[...] + gate_row(2) * o_win
    _write_heads(o_ref, o_t, R)


def _nsa_attn(proj, aux_g, gate_b, k_cmp, v_cmp, strip, strip_amax, ustrip, *, batch, seq, blks):
    nq = seq // TQ
    G = N_KV_HEADS
    R = N_HEADS // G
    assert R == SUBLANES and 3 * N_HEADS <= LANES and WINDOW <= TK and WINDOW <= (NEAR_TILES - 1) * TQ
    assert (seq // CMP_STRIDE) % TQ == 0
    W = R * TQ
    ncp = seq // CMP_STRIDE
    n_cmp = (seq - CMP_BLOCK) // CMP_STRIDE + 1
    n_sel = seq // SEL_BLOCK
    cs = np.arange(ncp)[None, :] * CMP_STRIDE
    ss = np.arange(n_sel)[:, None] * SEL_BLOCK
    cover = ((cs < ss + SEL_BLOCK) & (cs + CMP_BLOCK - 1 >= ss) & (np.arange(ncp)[None, :] < n_cmp))
    cover = jnp.asarray(cover.astype(np.float32), dtype=BF16)
    gb = jnp.zeros((1, LANES), F32).at[0, :3 * N_HEADS].set(gate_b)
    kv = lambda blk: pl.BlockSpec((seq, HEAD_DIM), lambda b, g, i: (b, blk + g))
    cmp_spec = pl.BlockSpec((ncp, HEAD_DIM), lambda b, g, i: (b, g))
    kern = functools.partial(_nsa_attn_kernel, r_heads=R, n_pick=min(SEL_TOPN, n_sel))
    return pl.pallas_call(
        kern,
        grid=(batch, G, nq),
        in_specs=[pl.BlockSpec((TQ, W), lambda b, g, i: (b * nq + i, g)),
                  cmp_spec, cmp_spec, kv(blks["ks"]), kv(blks["vs"]), kv(blks["kw"]), kv(blks["vw"]),
                  pl.BlockSpec((TQ, aux_g.shape[1]), lambda b, g, i: (b * nq + i, 0)),
                  pl.BlockSpec((1, LANES), lambda b, g, i: (0, 0)),
                  pl.BlockSpec((STRIP_ROWS, W), lambda b, g, i: (0, g)),
                  pl.BlockSpec((1, W), lambda b, g, i: (0, g)),
                  pl.BlockSpec((NEAR_CMP, W), lambda b, g, i: (0, g)),
                  pl.BlockSpec((n_sel, ncp), lambda b, g, i: (0, 0))],
        out_specs=pl.BlockSpec((TQ, W), lambda b, g, i: (b * nq + i, g)),
        out_shape=jax.ShapeDtypeStruct((batch * seq, N_HEADS * HEAD_DIM), BF16),
        scratch_shapes=[pltpu.VMEM((W, HEAD_DIM), BF16),
                        pltpu.VMEM((NEAR_CMP - CPQ + ncp, W), F32),
                        pltpu.VMEM((n_sel, TQ), F32),
                        pltpu.VMEM((LANES, TQ), F32),
                        pltpu.VMEM((1, W), F32), pltpu.VMEM((1, W), F32), pltpu.VMEM((HEAD_DIM, W), F32),
                        pltpu.VMEM((HEAD_DIM, W), F32), pltpu.VMEM((HEAD_DIM, W), F32),
                        pltpu.VMEM((n_sel, TQ), F32),
                        pltpu.SMEM((2,), F32)],
        compiler_params=_cparams(("arbitrary", "arbitrary", "arbitrary")),
        name="nsa_attn",
    )(proj, k_cmp, v_cmp, proj, proj, proj, proj, aux_g, gb, strip, strip_amax, ustrip, cover)


def _nsa_mixer(xf, mod_all, tab, gain, w_in, gate_b, w_out, q_gain, k_gain, cmp_pos, cmp_w1, cmp_w2, strip, strip_amax,
               ustrip, *, batch, seq):
    AW, KW = N_HEADS * HEAD_DIM, N_KV_HEADS * HEAD_DIM
    tn = KW
    wp = _proj_layout(w_in, [(0, AW), (AW + 2 * KW, KW), (AW + 3 * KW, KW), (AW + 4 * KW, KW), (AW + 5 * KW, KW),
                             (AW, KW), (AW + KW, KW), (AW + 6 * KW, 3 * N_HEADS)], tn)
    nq_t = AW // tn
    gain_cols = jnp.concatenate([jnp.tile(q_gain * (HEAD_DIM ** -0.5 * LOG2E), N_HEADS), jnp.tile(k_gain[1], N_KV_HEADS),
                                 jnp.zeros((tn,), F32), jnp.tile(k_gain[2], N_KV_HEADS),
                                 jnp.zeros((4 * tn,), F32)]).reshape(1, -1)
    proj, kc, vc, aux_g = _norm_matmul(xf, mod_all, tab, gain, wp, mod_row=3, seq=seq, tn=tn, gain_cols=gain_cols,
                                       norm_tiles=tuple(range(nq_t)) + (nq_t, nq_t + 2), n_aux=3)
    k_cmp, v_cmp = _nsa_compress(kc, vc, cmp_pos, cmp_w1, cmp_w2, k_gain[0], batch=batch, seq=seq)
    per = tn // HEAD_DIM
    blks = {"ks": nq_t * per, "vs": (nq_t + 1) * per, "kw": (nq_t + 2) * per, "vw": (nq_t + 3) * per}
    o = _nsa_attn(proj, aux_g, gate_b, k_cmp, v_cmp, strip, strip_amax, ustrip, batch=batch, seq=seq, blks=blks)
    return _matmul_res(o, w_out.astype(BF16), xf, mod_all, tab, mod_row=5, seq=seq)


def _rglru_kernel(gbr_ref, xr_ref, cw_ref, cb_ref, gw_ref, gbias_ref, lam_ref, o_ref, h_scr, tail_scr, *, ts, bd):
    @pl.when(pl.program_id(2) == 0)
    def _():
        h_scr[...] = jnp.zeros(h_scr.shape, F32)
        tail_scr[...] = jnp.zeros(tail_scr.shape, F32)

    x = xr_ref[...]
    cbw = x.shape[1]
    xfull = jnp.concatenate([tail_scr[...], x], axis=0)
    xc = cw_ref[CONV_WIDTH - 1:CONV_WIDTH, :] * x + cb_ref[...]
    for w in range(1, CONV_WIDTH):
        xc = xc + cw_ref[CONV_WIDTH - 1 - w:CONV_WIDTH - w, :] * pltpu.roll(xfull, w, 0)[SUBLANES:, :]
    tail_scr[...] = x[ts - SUBLANES:ts, :]

    gr = []
    for gi in range(2):
        parts = [jnp.dot(xc[:, n * bd:(n + 1) * bd].astype(BF16), gw_ref[gi, n], preferred_element_type=F32)
                 for n in range(cbw // bd)]
        gr.append(jnp.concatenate(parts, axis=1) + gbias_ref[gi:gi + 1, :])
    r = jax.nn.sigmoid(gr[0])
    i_g = jax.nn.sigmoid(gr[1])
    nl = -lam_ref[...]
    softplus = jnp.maximum(nl, 0.0) + jnp.log1p(jnp.exp(-jnp.abs(nl)))
    log_a = -RG_C * r * softplus
    a = jnp.exp(log_a)
    u = jnp.sqrt(1.0 - jnp.exp(2.0 * log_a)) * (i_g * xc)

    row = lax.broadcasted_iota(I32, (ts, cbw), 0)
    sft = 1
    while sft < ts:
        keep = row >= sft
        a_sh = jnp.where(keep, pltpu.roll(a, sft, 0), 1.0)
        u_sh = jnp.where(keep, pltpu.roll(u, sft, 0), 0.0)
        u = u + a * u_sh
        a = a * a_sh
        sft *= 2
    hs = u + a * h_scr[...]
    h_scr[...] = hs[ts - 1:ts, :]
    o_ref[...] = (_gelu_tanh(gbr_ref[...]) * hs).astype(o_ref.dtype)


def _rglru_mixer(xf, mod_all, tab, gain, w_in, conv_w, conv_b, gate_w, gate_b, lam, w_out, *, batch, seq):
    d_rnn = conv_w.shape[1]
    bd = d_rnn // RG_BLOCKS
    tn = min(512, d_rnn)
    proj = _norm_matmul(xf, mod_all, tab, gain, w_in.astype(BF16), mod_row=3, seq=seq, tn=tn, out_dtype=F32)
    cbw = min(512, d_rnn)
    ncb = d_rnn // cbw
    ts = min(256, seq)
    nts = seq // ts
    kern = functools.partial(_rglru_kernel, ts=ts, bd=bd)
    y = pl.pallas_call(
        kern,
        grid=(batch, ncb, nts),
        in_specs=[pl.BlockSpec((ts, cbw), lambda b, c, t: (b * nts + t, c)),
                  pl.BlockSpec((ts, cbw), lambda b, c, t: (b * nts + t, ncb + c)),
                  pl.BlockSpec((CONV_WIDTH, cbw), lambda b, c, t: (0, c)),
                  pl.BlockSpec((1, cbw), lambda b, c, t: (0, c)),
                  pl.BlockSpec((2, cbw // bd, bd, bd), lambda b, c, t: (0, c, 0, 0)),
                  pl.BlockSpec((2, cbw), lambda b, c, t: (0, c)),
                  pl.BlockSpec((1, cbw), lambda b, c, t: (0, c))],
        out_specs=pl.BlockSpec((ts, cbw), lambda b, c, t: (b * nts + t, c)),
        out_shape=jax.ShapeDtypeStruct((batch * seq, d_rnn), BF16),
        scratch_shapes=[pltpu.VMEM((1, cbw), F32), pltpu.VMEM((SUBLANES, cbw), F32)],
        compiler_params=_cparams(("arbitrary", "arbitrary", "arbitrary")),
        name="rglru",
    )(proj, proj, conv_w, conv_b.reshape(1, d_rnn), gate_w.astype(BF16), gate_b, lam.reshape(1, d_rnn))
    return _matmul_res(y, w_out.astype(BF16), xf, mod_all, tab, mod_row=5, seq=seq)


def kernel(x, c, rel_bias, ada_w, ada_b, ada_table, norm_g, ffn_w_in, ffn_w_out, dsa_w_in, dsa_w_out, dsa_q_gain, dsa_k_gain, nsa_w_in, nsa_gate_b, nsa_w_out, nsa_q_gain, nsa_k_gain, nsa_cmp_pos, nsa_cmp_w1, nsa_cmp_w2, rg_w_in, rg_conv_w, rg_conv_b, rg_gate_w, rg_gate_b, rg_lambda, rg_w_out):
    B, S, D = x.shape
    depth = ada_table.shape[0]
    assert S % TK == 0 and S >= NEAR_TILES * TQ
    mod_all = _mod_all(c, ada_w, ada_b)
    strip, strip_amax, ustrip = _bias_strips(rel_bias)
    w1_all, w2_all, tf = _prep_ffn_weights(ffn_w_in, ffn_w_out)
    xf = x.reshape(B * S, D)
    for layer in range(depth):
        tab = ada_table[layer]
        xf = _ffn(xf, mod_all, tab, norm_g[layer, 0], w1_all, w2_all, tf, 2 * layer, mod_row=0, seq=S)
        kind, j = layer % 3, layer // 3
        if kind == 0:
            xf = _dsa_mixer(xf, mod_all, tab, norm_g[layer, 1], dsa_w_in[j], dsa_w_out[j], dsa_q_gain[j],
                            dsa_k_gain[j], strip, strip_amax, batch=B, seq=S)
        elif kind == 1:
            xf = _nsa_mixer(xf, mod_all, tab, norm_g[layer, 1], nsa_w_in[j], nsa_gate_b[j], nsa_w_out[j],
                            nsa_q_gain[j], nsa_k_gain[j], nsa_cmp_pos[j], nsa_cmp_w1[j], nsa_cmp_w2[j],
                            strip, strip_amax, ustrip, batch=B, seq=S)
        else:
            xf = _rglru_mixer(xf, mod_all, tab, norm_g[layer, 1], rg_w_in[j], rg_conv_w[j], rg_conv_b[j],
                              rg_gate_w[j], rg_gate_b[j], rg_lambda[j], rg_w_out[j], batch=B, seq=S)
        xf = _ffn(xf, mod_all, tab, norm_g[layer, 2], w1_all, w2_all, tf, 2 * layer + 1, mod_row=6, seq=S)
    return xf.reshape(B, S, D)
```
